```python
import math
import jax, jax.numpy as jnp
from jax import lax
import numpy as np

D_MODEL = 2048
BATCH = 2
SEQ = 8192
DEPTH = 4

N_MIXERS = 3
N_LRU_LAYERS = (DEPTH + N_MIXERS - 1) // N_MIXERS
N_NSA_LAYERS = (DEPTH - 1 + N_MIXERS - 1) // N_MIXERS
N_RET_LAYERS = (DEPTH - 2 + N_MIXERS - 1) // N_MIXERS

DN_ALPHA = (2 * DEPTH) ** 0.25
DN_BETA = (8 * DEPTH) ** -0.25
LN_EPS = 1e-5
GN_EPS = 1e-5

D_FF = 4 * D_MODEL

LRU_WIDTH = D_MODEL
LRU_BLOCKS = 8
LRU_BLOCK_DIM = LRU_WIDTH // LRU_BLOCKS
CONV_WIDTH = 4
LRU_C = 8.0

NSA_HEADS = 16
NSA_HEAD_DIM = 128
NSA_KV_GROUPS = 4
NSA_HEADS_PER_GROUP = NSA_HEADS // NSA_KV_GROUPS
NSA_Q_WIDTH = NSA_HEADS * NSA_HEAD_DIM
NSA_KV_WIDTH = NSA_KV_GROUPS * NSA_HEAD_DIM
N_BRANCH = 3
NSA_IN_WIDTH = NSA_Q_WIDTH + 2 * N_BRANCH * NSA_KV_WIDTH + N_BRANCH * NSA_HEADS
NSA_SPLITS = tuple(NSA_Q_WIDTH + i * NSA_KV_WIDTH for i in range(2 * N_BRANCH + 1))
CMP_BLOCK = 32
CMP_STRIDE = 16
SEL_BLOCK = 64
SEL_RATIO = SEL_BLOCK // CMP_STRIDE
SEL_TOP_N = 16
SEL_FORCE = 1e6
WINDOW = 512
NSA_Q_BLOCK = 64
NEG_BIG = 1e30

T5_BUCKETS = 32
T5_MAX_DIST = 1024

RET_HEADS = 8
RET_QK_DIM = D_MODEL // RET_HEADS
RET_V_DIM = 2 * RET_QK_DIM
RET_QK_TOTAL = RET_HEADS * RET_QK_DIM
RET_V_TOTAL = RET_HEADS * RET_V_DIM
RET_IN_WIDTH = 2 * RET_QK_TOTAL + 2 * RET_V_TOTAL
RET_SPLITS = (RET_QK_TOTAL, 2 * RET_QK_TOTAL, 2 * RET_QK_TOTAL + RET_V_TOTAL)
RET_CHUNK = 128
ROPE_BASE = 10000.0

kernel_name = 'hybrid_rglru_nsa_retention_block'


def _layer_norm(x, w, b):
    xf = x.astype(jnp.float32)
    mu = xf.mean(-1, keepdims=True)
    var = jnp.mean(jnp.square(xf - mu), -1, keepdims=True)
    return ((xf - mu) * lax.rsqrt(var + LN_EPS) * w + b).astype(x.dtype)


def _masked_softmax(logits, mask):
    logits = logits.astype(jnp.float32)
    m = jnp.max(jnp.where(mask, logits, -NEG_BIG), axis=-1, keepdims=True)
    e = jnp.exp(jnp.where(mask, logits - m, -jnp.inf))
    return e / jnp.maximum(e.sum(-1, keepdims=True), 1e-30)


def _t5_bucket(dist):
    n = jnp.maximum(dist, 0)
    max_exact = T5_BUCKETS // 2
    log_ratio = jnp.log(jnp.maximum(n, 1).astype(jnp.float32) / max_exact) / math.log(T5_MAX_DIST / max_exact)
    large = jnp.minimum(max_exact + (log_ratio * (T5_BUCKETS - max_exact)).astype(jnp.int32), T5_BUCKETS - 1)
    return jnp.where(n < max_exact, n, large)


def _rotary(x, pos):
    half = x.shape[-1] // 2
    inv = ROPE_BASE ** (-jnp.arange(half, dtype=jnp.float32) / half)
    ang = pos.astype(jnp.float32)[:, None] * inv[None, :]
    cos = jnp.cos(ang)[None, :, None, :].astype(x.dtype)
    sin = jnp.sin(ang)[None, :, None, :].astype(x.dtype)
    x1, x2 = x[..., :half], x[..., half:]
    return jnp.concatenate([x1 * cos - x2 * sin, x1 * sin + x2 * cos], axis=-1)


def _linear_combine(lhs, rhs):
    a_l, b_l = lhs
    a_r, b_r = rhs
    return a_l * a_r, a_r * b_l + b_r


def rglru_mixer(h, w_in, conv_w, conv_b, gate_w, gate_b, lam, w_out):
    bsz, seq, _ = h.shape
    f32 = jnp.float32
    y_br, x_br = jnp.split(h @ w_in, 2, axis=-1)
    xp = jnp.pad(x_br, ((0, 0), (CONV_WIDTH - 1, 0), (0, 0)))
    xc = conv_b + sum(xp[:, k:k + seq] * conv_w[k] for k in range(CONV_WIDTH))
    xb = xc.reshape(bsz, seq, LRU_BLOCKS, LRU_BLOCK_DIM)
    gates = jnp.einsum('bsnd,jnde->jbsne', xb, gate_w).reshape(2, bsz, seq, LRU_WIDTH) + gate_b[:, None, None, :]
    r = jax.nn.sigmoid(gates[0].astype(f32))
    i = jax.nn.sigmoid(gates[1].astype(f32))
    log_a = -LRU_C * r * jax.nn.softplus(-lam.astype(f32))
    a = jnp.exp(log_a)
    u = jnp.sqrt(-jnp.expm1(2.0 * log_a)) * (i * xc.astype(f32))
    _, hs = lax.associative_scan(_linear_combine, (a, u), axis=1)
    return ((jax.nn.gelu(y_br) * hs.astype(h.dtype)) @ w_out).astype(h.dtype)


def nsa_mixer(h, rel_bias, w_in, cmp_pe, cmp_w1, cmp_b1, cmp_w2, w_out):
    bsz, seq, _ = h.shape
    f32 = jnp.float32
    G, HPG, DH, QB = NSA_KV_GROUPS, NSA_HEADS_PER_GROUP, NSA_HEAD_DIM, NSA_Q_BLOCK
    q, kc_tok, vc_tok, ks_tok, vs_tok, kw_tok, vw_tok, g = jnp.split(h @ w_in, list(NSA_SPLITS), axis=-1)
    q = (q * DH ** -0.5).reshape(bsz, seq, G, HPG, DH)
    gates = jax.nn.sigmoid(g.astype(f32)).reshape(bsz, seq, N_BRANCH, G, HPG).astype(h.dtype)

    def to_heads(t):
        return t.reshape(bsz, seq, G, DH)

    def compress(tok, j):
        chunks = to_heads(tok).reshape(bsz, seq // CMP_STRIDE, CMP_STRIDE, G, DH)
        blocks = jnp.concatenate([chunks[:, :-1], chunks[:, 1:]], axis=2) + cmp_pe[j][None, None, :, None, :]
        hid = jax.nn.gelu(jnp.einsum('bnlgd,lde->bnge', blocks, cmp_w1[j]) + cmp_b1[j])
        return jnp.einsum('bnge,ef->bngf', hid, cmp_w2[j])

    k_cmp, v_cmp = compress(kc_tok, 0), compress(vc_tok, 1)
    n_cmp = seq // CMP_STRIDE - 1
    cmp_end = jnp.arange(n_cmp) * CMP_STRIDE + CMP_BLOCK - 1
    n_sel = seq // SEL_BLOCK
    n_top = min(SEL_TOP_N, n_sel)

    def to_blocks(t):
        return to_heads(t).reshape(bsz, n_sel, SEL_BLOCK, G, DH).transpose(0, 3, 1, 2, 4)

    k_slc, v_slc = to_blocks(ks_tok), to_blocks(vs_tok)
    pad = ((0, 0), (WINDOW, 0), (0, 0), (0, 0))
    k_win, v_win = jnp.pad(to_heads(kw_tok), pad), jnp.pad(to_heads(vw_tok), pad)

    def bias_heads(b):
        return b.transpose(2, 0, 1).reshape(G, HPG, b.shape[0], b.shape[1])

    tbl_grp = rel_bias.T.reshape(G, HPG, T5_BUCKETS)
    b_ix = jnp.arange(bsz)[:, None, None, None]
    g_ix = jnp.arange(G)[None, :, None, None]
    h_ix = jnp.arange(HPG)[None, None, :, None, None]
    blk_ids = jnp.arange(n_sel)
    blk_start = blk_ids * SEL_BLOCK
    win_off = jnp.arange(WINDOW + QB) - WINDOW

    def block_fn(qb):
        t0 = qb * QB
        qpos = t0 + jnp.arange(QB)
        qblk = lax.dynamic_slice_in_dim(q, t0, QB, axis=1)
        gblk = lax.dynamic_slice_in_dim(gates, t0, QB, axis=1)
        dist_c = qpos[:, None] - cmp_end[None, :]
        s_c = jnp.einsum('bqghd,bngd->bghqn', qblk, k_cmp).astype(f32) + bias_heads(rel_bias[_t5_bucket(dist_c)])
        p_c = _masked_softmax(s_c, dist_c >= 0)
        o_c = jnp.einsum('bghqn,bngd->bqghd', p_c.astype(v_cmp.dtype), v_cmp)
        p_grp = jnp.pad(p_c.sum(axis=2), ((0, 0), (0, 0), (0, 0), (1, SEL_RATIO)))
        sel_score = p_grp[..., :SEL_RATIO * n_sel].reshape(bsz, G, QB, n_sel, SEL_RATIO).sum(-1) + p_grp[..., SEL_RATIO::SEL_RATIO]
        cur = (qpos // SEL_BLOCK)[:, None]
        visible = blk_start[None, :] <= qpos[:, None]
        forced = (blk_ids[None, :] == 0) | (blk_ids[None, :] == cur) | (blk_ids[None, :] == cur - 1)
        sel_score = jnp.where(forced, SEL_FORCE, jnp.where(visible, sel_score, -SEL_FORCE))
        _, sel_idx = lax.top_k(sel_score, n_top)
        ks = k_slc[b_ix, g_ix, sel_idx].reshape(bsz, G, QB, n_top * SEL_BLOCK, DH)
        vs = v_slc[b_ix, g_ix, sel_idx].reshape(bsz, G, QB, n_top * SEL_BLOCK, DH)
        kpos = (sel_idx[..., None] * SEL_BLOCK + jnp.arange(SEL_BLOCK)).reshape(bsz, G, QB, n_top * SEL_BLOCK)
        dist_s = qpos[None, None, :, None] - kpos
        bias_s = tbl_grp[g_ix[..., None], h_ix, _t5_bucket(dist_s)[:, :, None]]
        s_s = jnp.einsum('bqghd,bgqld->bghql', qblk, ks).astype(f32) + bias_s
        p_s = _masked_softmax(s_s, (dist_s >= 0)[:, :, None])
        o_s = jnp.einsum('bghql,bgqld->bqghd', p_s.astype(vs.dtype), vs)
        kw = lax.dynamic_slice_in_dim(k_win, t0, WINDOW + QB, axis=1)
        vw = lax.dynamic_slice_in_dim(v_win, t0, WINDOW + QB, axis=1)
        kpos_w = t0 + win_off
        dist_w = qpos[:, None] - kpos_w[None, :]
        mask_w = (dist_w >= 0) & (dist_w < WINDOW) & (kpos_w >= 0)[None, :]
        s_w = jnp.einsum('bqghd,bkgd->bghqk', qblk, kw).astype(f32) + bias_heads(rel_bias[_t5_bucket(dist_w)])
        p_w = _masked_softmax(s_w, mask_w)
        o_w = jnp.einsum('bghqk,bkgd->bqghd', p_w.astype(vw.dtype), vw)
        o = gblk[:, :, 0, :, :, None] * o_c + gblk[:, :, 1, :, :, None] * o_s + gblk[:, :, 2, :, :, None] * o_w
        return o.reshape(bsz, QB, NSA_Q_WIDTH)

    out = lax.map(block_fn, jnp.arange(seq // QB))
    out = out.transpose(1, 0, 2, 3).reshape(bsz, seq, NSA_Q_WIDTH)
    return (out @ w_out).astype(h.dtype)


def retention_mixer(h, w_in, gn_w, gn_b, w_out):
    bsz, seq, _ = h.shape
    f32 = jnp.float32
    q, k, v, g = jnp.split(h @ w_in, list(RET_SPLITS), axis=-1)
    pos = jnp.arange(seq)
    q = _rotary(q.reshape(bsz, seq, RET_HEADS, RET_QK_DIM), pos)
    k = _rotary(k.reshape(bsz, seq, RET_HEADS, RET_QK_DIM), pos) * RET_QK_DIM ** -0.5
    v = v.reshape(bsz, seq, RET_HEADS, RET_V_DIM)
    n_chunks = seq // RET_CHUNK

    def to_chunks(t):
        return t.astype(f32).reshape(bsz, n_chunks, RET_CHUNK, RET_HEADS, t.shape[-1]).transpose(1, 0, 3, 2, 4)

    log_g = jnp.log1p(-jnp.exp2(-5.0 - jnp.arange(RET_HEADS, dtype=f32)))
    idx = jnp.arange(RET_CHUNK, dtype=f32)
    diff = idx[:, None] - idx[None, :]
    decay = jnp.where(diff >= 0, jnp.exp(jnp.maximum(diff, 0.0) * log_g[:, None, None]), 0.0)
    xi = jnp.exp((idx + 1.0) * log_g[:, None])
    zeta = jnp.exp((RET_CHUNK - 1.0 - idx) * log_g[:, None])
    chunk_decay = jnp.exp(RET_CHUNK * log_g)

    def step(state, qkv):
        qc, kc, vc = qkv
        inner = jnp.einsum('bhqd,bhkd->bhqk', qc, kc) * decay
        out = jnp.einsum('bhqk,bhkv->bhqv', inner, vc) + jnp.einsum('bhqd,bhdv->bhqv', qc, state) * xi[:, :, None]
        state = state * chunk_decay[:, None, None] + jnp.einsum('bhkd,bhkv->bhdv', kc * zeta[:, :, None], vc)
        return state, out

    state0 = jnp.zeros((bsz, RET_HEADS, RET_QK_DIM, RET_V_DIM), f32)
    _, o = lax.scan(step, state0, (to_chunks(q), to_chunks(k), to_chunks(v)))
    o = o.transpose(1, 0, 3, 2, 4).reshape(bsz, seq, RET_HEADS, RET_V_DIM)
    mu = o.mean(-1, keepdims=True)
    var = jnp.mean(jnp.square(o - mu), -1, keepdims=True)
    o = ((o - mu) * lax.rsqrt(var + GN_EPS)).reshape(bsz, seq, RET_V_TOTAL) * gn_w + gn_b
    return ((jax.nn.silu(g) * o.astype(h.dtype)) @ w_out).astype(h.dtype)


def sq_relu_mlp(h, w1, w2):
    return (jnp.square(jax.nn.relu(h @ w1)) @ w2).astype(h.dtype)


def setup_inputs(seed: int = 0) -> dict:
    key = jax.random.key(seed)
    ks = jax.random.split(key, 26)
    f32 = jnp.float32

    def nrm(k, shape, scale):
        return jax.random.normal(k, shape, f32) * scale

    u = jax.random.uniform(ks[15], (N_LRU_LAYERS, LRU_WIDTH), f32, minval=0.9, maxval=0.999)
    a0 = u ** (1.0 / LRU_C)
    return {
        'x': nrm(ks[0], (BATCH, SEQ, D_MODEL), 1.0),
        'c': nrm(ks[1], (BATCH, D_MODEL), 1.0),
        'rel_bias': nrm(ks[2], (T5_BUCKETS, NSA_HEADS), 0.5),
        'ada_w': nrm(ks[3], (DEPTH, 2, D_MODEL, 3 * D_MODEL), 0.1 * D_MODEL ** -0.5),
        'ada_b': nrm(ks[4], (DEPTH, 2, 3 * D_MODEL), 0.02),
        'ln_w': 1.0 + nrm(ks[5], (DEPTH, 2, D_MODEL), 0.02),
        'ln_b': nrm(ks[6], (DEPTH, 2, D_MODEL), 0.02),
        'mlp_w1': nrm(ks[7], (DEPTH, D_MODEL, D_FF), D_MODEL ** -0.5),
        'mlp_w2': nrm(ks[8], (DEPTH, D_FF, D_MODEL), DN_BETA * D_FF ** -0.5),
        'lru_w_in': nrm(ks[9], (N_LRU_LAYERS, D_MODEL, 2 * LRU_WIDTH), D_MODEL ** -0.5),
        'lru_conv_w': nrm(ks[10], (N_LRU_LAYERS, CONV_WIDTH, LRU_WIDTH), CONV_WIDTH ** -0.5),
        'lru_conv_b': nrm(ks[11], (N_LRU_LAYERS, LRU_WIDTH), 0.02),
        'lru_gate_w': nrm(ks[12], (N_LRU_LAYERS, 2, LRU_BLOCKS, LRU_BLOCK_DIM, LRU_BLOCK_DIM), LRU_BLOCK_DIM ** -0.5),
        'lru_gate_b': nrm(ks[13], (N_LRU_LAYERS, 2, LRU_WIDTH), 0.02),
        'lru_lambda': jnp.log(a0) - jnp.log1p(-a0) + nrm(ks[14], (N_LRU_LAYERS, LRU_WIDTH), 0.01),
        'lru_w_out': nrm(ks[16], (N_LRU_LAYERS, LRU_WIDTH, D_MODEL), DN_BETA * LRU_WIDTH ** -0.5),
        'nsa_w_in': nrm(ks[17], (N_NSA_LAYERS, D_MODEL, NSA_IN_WIDTH), D_MODEL ** -0.5),
        'nsa_cmp_pe': nrm(ks[18], (N_NSA_LAYERS, 2, CMP_BLOCK, NSA_HEAD_DIM), 0.02),
        'nsa_cmp_w1': nrm(ks[19], (N_NSA_LAYERS, 2, CMP_BLOCK, NSA_HEAD_DIM, NSA_HEAD_DIM), (CMP_BLOCK * NSA_HEAD_DIM) ** -0.5),
        'nsa_cmp_b1': nrm(ks[20], (N_NSA_LAYERS, 2, NSA_HEAD_DIM), 0.02),
        'nsa_cmp_w2': nrm(ks[21], (N_NSA_LAYERS, 2, NSA_HEAD_DIM, NSA_HEAD_DIM), NSA_HEAD_DIM ** -0.5),
        'nsa_w_out': nrm(ks[22], (N_NSA_LAYERS, NSA_Q_WIDTH, D_MODEL), DN_BETA * NSA_Q_WIDTH ** -0.5),
        'ret_w_in': nrm(ks[23], (N_RET_LAYERS, D_MODEL, RET_IN_WIDTH), D_MODEL ** -0.5),
        'ret_gn_w': 1.0 + nrm(ks[24], (N_RET_LAYERS, 2, RET_V_TOTAL), 0.02)[:, 0],
        'ret_gn_b': nrm(ks[24], (N_RET_LAYERS, 2, RET_V_TOTAL), 0.02)[:, 1],
        'ret_w_out': nrm(ks[25], (N_RET_LAYERS, RET_V_TOTAL, D_MODEL), DN_BETA * RET_V_TOTAL ** -0.5),
    }


def reference(x, c, rel_bias, ada_w, ada_b, ln_w, ln_b, mlp_w1, mlp_w2,
              lru_w_in, lru_conv_w, lru_conv_b, lru_gate_w, lru_gate_b, lru_lambda, lru_w_out,
              nsa_w_in, nsa_cmp_pe, nsa_cmp_w1, nsa_cmp_b1, nsa_cmp_w2, nsa_w_out,
              ret_w_in, ret_gn_w, ret_gn_b, ret_w_out):
    cond = jax.nn.silu(c)
    for layer in range(DEPTH):
        mixer, inst = layer % N_MIXERS, layer // N_MIXERS
        mod = jnp.einsum('bd,jde->bje', cond, ada_w[layer]) + ada_b[layer]
        shift, scale, gate = jnp.split(mod, 3, axis=-1)
        h = x * (1.0 + scale[:, 0, None]) + shift[:, 0, None]
        if mixer == 0:
            y = rglru_mixer(h, lru_w_in[inst], lru_conv_w[inst], lru_conv_b[inst], lru_gate_w[inst],
                            lru_gate_b[inst], lru_lambda[inst], lru_w_out[inst])
        elif mixer == 1:
            y = nsa_mixer(h, rel_bias, nsa_w_in[inst], nsa_cmp_pe[inst], nsa_cmp_w1[inst],
                          nsa_cmp_b1[inst], nsa_cmp_w2[inst], nsa_w_out[inst])
        else:
            y = retention_mixer(h, ret_w_in[inst], ret_gn_w[inst], ret_gn_b[inst], ret_w_out[inst])
        x = _layer_norm(DN_ALPHA * x + (1.0 + gate[:, 0, None]) * y, ln_w[layer, 0], ln_b[layer, 0])
        h = x * (1.0 + scale[:, 1, None]) + shift[:, 1, None]
        y = sq_relu_mlp(h, mlp_w1[layer], mlp_w2[layer])
        x = _layer_norm(DN_ALPHA * x + (1.0 + gate[:, 1, None]) * y, ln_w[layer, 1], ln_b[layer, 1])
    return x
```

```python
import functools
import math

import numpy as np
import jax
import jax.numpy as jnp
from jax import lax
from jax.experimental import pallas as pl
from jax.experimental.pallas import tpu as pltpu

F32 = jnp.float32
BF16 = jnp.bfloat16

DN_DEPTH = 4
N_MIXERS = 3
DN_ALPHA = (2 * DN_DEPTH) ** 0.25
LN_EPS = 1e-5
GN_EPS = 1e-5
LRU_BLOCKS = 8
CONV_WIDTH = 4
LRU_C = 8.0
NSA_HEADS = 16
NSA_HEAD_DIM = 128
NSA_KV_GROUPS = 4
NSA_HPG = NSA_HEADS // NSA_KV_GROUPS
N_BRANCH = 3
CMP_BLOCK = 32
CMP_STRIDE = 16
SEL_BLOCK = 64
SEL_TOP_N = 16
SEL_FORCE = 1e6
WINDOW = 512
NEG_BIG = 1e30
T5_BUCKETS = 32
T5_MAX_DIST = 1024
RET_HEADS = 8
RET_CHUNK = 128
ROPE_BASE = 10000.0

LANES = 128
VMEM_LIMIT_BYTES = 56 * 2 ** 20

NEG_MASK = -(2.0 ** 100)
MASK_BUCKET = T5_BUCKETS

N_CMP_SLOTS = 4 * LANES
CMP_ROWS = CMP_STRIDE * N_CMP_SLOTS + 4 * CMP_STRIDE


def _cparams(*sem):
    return pltpu.CompilerParams(dimension_semantics=sem, vmem_limit_bytes=VMEM_LIMIT_BYTES)


def _sigmoid(x):
    return 1.0 / (1.0 + jnp.exp(-x))


def _silu(x):
    return x * _sigmoid(x)


def _gelu_tanh(x):
    return 0.5 * x * (1.0 + jnp.tanh(math.sqrt(2.0 / math.pi) * (x + 0.044715 * (x * x * x))))


def _dot(a, b):
    return jnp.dot(a, b, preferred_element_type=F32)


def _dot_nt(a, b):
    return lax.dot_general(a, b, (((1,), (1,)), ((), ())), preferred_element_type=F32)


def _dot_tn(a, b):
    return lax.dot_general(a, b, (((0,), (0,)), ((), ())), preferred_element_type=F32)


def _ada_kernel(c_ref, w_ref, b_ref, o_ref):
    cond = _silu(c_ref[...]).astype(BF16)
    o_ref[...] = _dot(cond, w_ref[...].astype(BF16)) + b_ref[...]


def _ada_mods(c, ada_w, ada_b):
    depth, _, d, d3 = ada_w.shape
    bsz = c.shape[0]
    n = depth * 2
    rows = 8
    c_pad = jnp.zeros((rows, d), F32).at[:bsz].set(c)
    tn = 1024
    out = pl.pallas_call(
        _ada_kernel,
        grid=(n, d3 // tn),
        in_specs=[
            pl.BlockSpec((rows, d), lambda s, j: (0, 0)),
            pl.BlockSpec((None, d, tn), lambda s, j: (s, 0, j)),
            pl.BlockSpec((None, 1, tn), lambda s, j: (s, 0, j)),
        ],
        out_specs=pl.BlockSpec((None, rows, tn), lambda s, j: (s, 0, j)),
        out_shape=jax.ShapeDtypeStruct((n, rows, d3), F32),
        compiler_params=_cparams("parallel", "parallel"),
        name="ada_mods",
    )(c_pad, ada_w.reshape(n, d, d3), ada_b.reshape(n, 1, d3))
    return out[:, :bsz].reshape(n, bsz, 3, d).transpose(0, 2, 1, 3).reshape(n, 3, bsz, 1, d)


SHIFT, SCALE, GATE = 0, 1, 2


def _mod_spec(d, sub, which, tiles_per_batch):
    return pl.BlockSpec((None, None, None, 1, d),
                        lambda i, *_: (sub, which, i // tiles_per_batch, 0, 0))


def _row_spec(d):
    return pl.BlockSpec((1, d), lambda *_: (0, 0))


def _modulate_kernel(x_ref, sc_ref, sh_ref, h_ref):
    h_ref[...] = (x_ref[...] * (1.0 + sc_ref[...]) + sh_ref[...]).astype(BF16)


def _modulate(x, mods, sub, seq, tm=512):
    t, d = x.shape
    tpb = seq // tm
    return pl.pallas_call(
        _modulate_kernel,
        grid=(t // tm,),
        in_specs=[pl.BlockSpec((tm, d), lambda i: (i, 0)),
                  _mod_spec(d, sub, SCALE, tpb), _mod_spec(d, sub, SHIFT, tpb)],
        out_specs=pl.BlockSpec((tm, d), lambda i: (i, 0)),
        out_shape=jax.ShapeDtypeStruct((t, d), BF16),
        compiler_params=_cparams("parallel"),
        name="modulate",
    )(x, mods, mods)


def _proj_kernel(segs, n_extra, h_ref, w_ref, *refs):
    extra = refs[:n_extra]
    outs = refs[n_extra:]
    j = pl.program_id(1)
    acc = _dot(h_ref[...], w_ref[...])
    for (lo, hi, epi), o_ref in zip(segs, outs):
        def _store(o_ref=o_ref, epi=epi):
            o_ref[...] = epi(acc, *[e[...] for e in extra]).astype(o_ref.dtype)
        if len(segs) == 1:
            _store()
        else:
            pl.when(jnp.logical_and(j >= lo, j < hi))(_store)


def _proj(h, w, segs, extra=(), extra_specs=(), tm=1024, tn=512, name="proj"):
    t, k = h.shape
    n = w.shape[1]
    bounds, lo = [], 0
    for n_cols, epi, _ in segs:
        assert n_cols % tn == 0
        bounds.append((lo, lo + n_cols // tn, epi))
        lo += n_cols // tn
    assert lo * tn == n
    out_specs = [
        pl.BlockSpec((tm, tn), functools.partial(
            lambda i, j, lo, hi: (i, jnp.clip(j - lo, 0, hi - lo - 1)), lo=lo_, hi=hi_))
        for lo_, hi_, _ in bounds]
    out_shape = [jax.ShapeDtypeStruct((t, n_cols), dt) for n_cols, _, dt in segs]
    return pl.pallas_call(
        functools.partial(_proj_kernel, bounds, len(extra)),
        grid=(t // tm, n // tn),
        in_specs=[pl.BlockSpec((tm, k), lambda i, j: (i, 0)),
                  pl.BlockSpec((k, tn), lambda i, j: (0, j))] + list(extra_specs),
        out_specs=out_specs,
        out_shape=out_shape,
        compiler_params=_cparams("parallel", "arbitrary"),
        name=name,
    )(h, w, *extra)


def _ln_mod_store(y, x_ref, gate_ref, lnw_ref, lnb_ref, sc_ref, sh_ref, xo_ref, ho_ref):
    z = DN_ALPHA * x_ref[...] + (1.0 + gate_ref[...]) * y
    mu = jnp.mean(z, axis=-1, keepdims=True)
    zc = z - mu
    var = jnp.mean(zc * zc, axis=-1, keepdims=True)
    xn = zc * lax.rsqrt(var + LN_EPS) * lnw_ref[...] + lnb_ref[...]
    xo_ref[...] = xn
    if ho_ref is not None:
        ho_ref[...] = (xn * (1.0 + sc_ref[...]) + sh_ref[...]).astype(BF16)


def _out_ln_kernel(n_a, has_next, *refs):
    a_refs = refs[:n_a]
    w_ref, x_ref, gate_ref, lnw_ref, lnb_ref = refs[n_a:n_a + 5]
    rest = refs[n_a + 5:]
    if has_next:
        sc_ref, sh_ref, xo_ref, ho_ref = rest
    else:
        (xo_ref,), sc_ref, sh_ref, ho_ref = rest, None, None, None
    a = a_refs[0][...]
    for r in a_refs[1:]:
        a = a + r[...]
    y = _dot(a.astype(BF16), w_ref[...])
    _ln_mod_store(y, x_ref, gate_ref, lnw_ref, lnb_ref, sc_ref, sh_ref, xo_ref, ho_ref)


def _out_ln(a_list, w, x, mods, sub, sub_next, ln_w, ln_b, seq, tm=256):
    t, d = x.shape
    k = w.shape[0]
    tpb = seq // tm
    has_next = sub_next is not None
    in_specs = [pl.BlockSpec((tm, k), lambda i: (i, 0)) for _ in a_list]
    in_specs += [pl.BlockSpec((k, d), lambda i: (0, 0), pipeline_mode=pl.Buffered(1)),
                 pl.BlockSpec((tm, d), lambda i: (i, 0)),
                 _mod_spec(d, sub, GATE, tpb), _row_spec(d), _row_spec(d)]
    args = list(a_list) + [w, x, mods, ln_w.reshape(1, d), ln_b.reshape(1, d)]
    out_specs = [pl.BlockSpec((tm, d), lambda i: (i, 0))]
    out_shape = [jax.ShapeDtypeStruct((t, d), F32)]
    if has_next:
        in_specs += [_mod_spec(d, sub_next, SCALE, tpb), _mod_spec(d, sub_next, SHIFT, tpb)]
        args += [mods, mods]
        out_specs.append(pl.BlockSpec((tm, d), lambda i: (i, 0)))
        out_shape.append(jax.ShapeDtypeStruct((t, d), BF16))
    res = pl.pallas_call(
        functools.partial(_out_ln_kernel, len(a_list), has_next),
        grid=(t // tm,),
        in_specs=in_specs, out_specs=out_specs, out_shape=out_shape,
        compiler_params=_cparams("parallel"),
        name="out_ln",
    )(*args)
    return (res[0], res[1]) if has_next else (res[0], None)


def _mlp_kernel(has_next, h_ref, w1_ref, w2_ref, x_ref, gate_ref, lnw_ref, lnb_ref, *rest):
    if has_next:
        sc_ref, sh_ref, xo_ref, ho_ref, acc_ref = rest
    else:
        (xo_ref, acc_ref), sc_ref, sh_ref, ho_ref = rest, None, None, None
    j = pl.program_id(1)

    @pl.when(j == 0)
    def _():
        acc_ref[...] = jnp.zeros_like(acc_ref)

    hid = _dot(h_ref[...], w1_ref[...])
    hid = jnp.square(jnp.maximum(hid, 0.0)).astype(BF16)
    acc_ref[...] += _dot(hid, w2_ref[...])

    @pl.when(j == pl.num_programs(1) - 1)
    def _():
        _ln_mod_store(acc_ref[...], x_ref, gate_ref, lnw_ref, lnb_ref, sc_ref, sh_ref, xo_ref, ho_ref)


def _mlp(h, w1, w2, x, mods, sub, sub_next, ln_w, ln_b, seq, tm=512, tf=1024):
    t, d = x.shape
    ff = w1.shape[1]
    tpb = seq // tm
    has_next = sub_next is not None
    in_specs = [pl.BlockSpec((tm, d), lambda i, j: (i, 0)),
                pl.BlockSpec((d, tf), lambda i, j: (0, j)),
                pl.BlockSpec((tf, d), lambda i, j: (j, 0)),
                pl.BlockSpec((tm, d), lambda i, j: (i, 0)),
                _mod_spec(d, sub, GATE, tpb), _row_spec(d), _row_spec(d)]
    args = [h, w1, w2, x, mods, ln_w.reshape(1, d), ln_b.reshape(1, d)]
    out_specs = [pl.BlockSpec((tm, d), lambda i, j: (i, 0))]
    out_shape = [jax.ShapeDtypeStruct((t, d), F32)]
    if has_next:
        in_specs += [_mod_spec(d, sub_next, SCALE, tpb), _mod_spec(d, sub_next, SHIFT, tpb)]
        args += [mods, mods]
        out_specs.append(pl.BlockSpec((tm, d), lambda i, j: (i, 0)))
        out_shape.append(jax.ShapeDtypeStruct((t, d), BF16))
    res = pl.pallas_call(
        functools.partial(_mlp_kernel, has_next),
        grid=(t // tm, ff // tf),
        in_specs=in_specs, out_specs=out_specs, out_shape=out_shape,
        scratch_shapes=[pltpu.VMEM((tm, d), F32)],
        compiler_params=_cparams("parallel", "arbitrary"),
        name="mlp",
    )(*args)
    return (res[0], res[1]) if has_next else (res[0], None)


HALO = 8


def _lru_scan_kernel(ts, gy_ref, xb_ref, cw_ref, cb_ref, gw_ref, gb_ref, lam_ref, o_ref,
                     buf, a_scr, u_scr, h_scr):
    i = pl.program_id(1)
    width = xb_ref.shape[-1]
    bd = width // LRU_BLOCKS

    @pl.when(i == 0)
    def _():
        buf[0:HALO, :] = jnp.zeros((HALO, width), F32)
        h_scr[...] = jnp.zeros_like(h_scr)

    @pl.when(i > 0)
    def _():
        buf[0:HALO, :] = buf[ts:ts + HALO, :]

    buf[HALO:HALO + ts, :] = xb_ref[...]
    cw = cw_ref[...]
    taps = [buf[HALO - (CONV_WIDTH - 1) + k:HALO - (CONV_WIDTH - 1) + k + ts, :] * cw[k:k + 1, :]
            for k in range(CONV_WIDTH)]
    conv = taps[0]
    for tap in taps[1:]:
        conv = conv + tap
    xc = cb_ref[...] + conv
    xcb = xc.astype(BF16)
    gates = []
    for jg in range(2):
        cols = [_dot(xcb[:, n * bd:(n + 1) * bd], gw_ref[jg, n]) for n in range(LRU_BLOCKS)]
        gates.append(jnp.concatenate(cols, axis=1) + gb_ref[jg:jg + 1, :])
    r = _sigmoid(gates[0])
    ig = _sigmoid(gates[1])
    neg_lam = -lam_ref[...]
    softplus = jnp.maximum(neg_lam, 0.0) + jnp.log1p(jnp.exp(-jnp.abs(neg_lam)))
    log_a = -LRU_C * r * softplus
    a_scr[...] = jnp.exp(log_a)
    th = jnp.tanh(log_a)
    u_scr[...] = jnp.sqrt(-2.0 * th / (1.0 - th)) * (ig * xc)

    def step(t, h):
        h = a_scr[pl.ds(t, 1), :] * h + u_scr[pl.ds(t, 1), :]
        u_scr[pl.ds(t, 1), :] = h
        return h

    h_scr[...] = lax.fori_loop(0, ts, step, h_scr[...], unroll=8)
    o_ref[...] = (gy_ref[...] * u_scr[...]).astype(BF16)


def _lru_scan(gy, xb, conv_w, conv_b, gate_w, gate_b, lam, ts=512):
    bsz, seq, width = xb.shape
    bd = width // LRU_BLOCKS
    tile = pl.BlockSpec((None, ts, width), lambda b, i: (b, i, 0))
    return pl.pallas_call(
        functools.partial(_lru_scan_kernel, ts),
        grid=(bsz, seq // ts),
        in_specs=[tile, tile,
                  pl.BlockSpec((CONV_WIDTH, width), lambda b, i: (0, 0)),
                  pl.BlockSpec((1, width), lambda b, i: (0, 0)),
                  pl.BlockSpec((2, LRU_BLOCKS, bd, bd), lambda b, i: (0, 0, 0, 0)),
                  pl.BlockSpec((2, width), lambda b, i: (0, 0)),
                  pl.BlockSpec((1, width), lambda b, i: (0, 0))],
        out_specs=tile,
        out_shape=jax.ShapeDtypeStruct((bsz, seq, width), BF16),
        scratch_shapes=[pltpu.VMEM((ts + HALO, width), F32), pltpu.VMEM((ts, width), F32),
                        pltpu.VMEM((ts, width), F32), pltpu.VMEM((1, width), F32)],
        compiler_params=_cparams("parallel", "arbitrary"),
        name="lru_scan",
    )(gy, xb, conv_w, conv_b.reshape(1, width), gate_w.astype(BF16), gate_b, lam.reshape(1, width))


def _lru_mixer(h, bsz, seq, w_in, conv_w, conv_b, gate_w, gate_b, lam):
    width = w_in.shape[1] // 2
    gy, xb = _proj(h, w_in.astype(BF16),
                   [(width, lambda acc: _gelu_tanh(acc), F32), (width, lambda acc: acc, F32)],
                   name="lru_proj")
    a = _lru_scan(gy.reshape(bsz, seq, width), xb.reshape(bsz, seq, width),
                  conv_w, conv_b, gate_w, gate_b, lam)
    return [a.reshape(bsz * seq, width)]


def _t5_bucket_np(dist):
    n = np.maximum(np.asarray(dist, np.int64), 0)
    max_exact = T5_BUCKETS // 2
    n_large = T5_BUCKETS - max_exact
    ratio = T5_MAX_DIST // max_exact
    thresholds = []
    for k in range(1, n_large):
        m = max_exact
        while m ** n_large < ratio ** k * max_exact ** n_large:
            m += 1
        thresholds.append(m)
    large = max_exact + sum((n >= th).astype(np.int64) for th in thresholds)
    return np.where(n < max_exact, n, large).astype(np.int32)


def _bias_table_kernel(sub_far, rb_ref, idx_ref, o_ref):
    hd = pl.program_id(0)
    idx = idx_ref[...]
    base = rb_ref[T5_BUCKETS - 1, hd] if sub_far else 0.0
    acc = jnp.full(idx.shape, NEG_MASK, F32)
    for b in range(T5_BUCKETS):
        acc = jnp.where(idx == b, rb_ref[b, hd] - base, acc)
    o_ref[...] = acc


def _bias_table(rel_bias, idx_np, sub_far, name):
    rows, cols = idx_np.shape
    tr = 128
    n_heads = rel_bias.shape[1]
    return pl.pallas_call(
        functools.partial(_bias_table_kernel, sub_far),
        grid=(n_heads, rows // tr),
        in_specs=[pl.BlockSpec(memory_space=pltpu.SMEM),
                  pl.BlockSpec((tr, cols), lambda hd, i: (i, 0))],
        out_specs=pl.BlockSpec((None, tr, cols), lambda hd, i: (hd, i, 0)),
        out_shape=jax.ShapeDtypeStruct((n_heads, rows, cols), F32),
        compiler_params=_cparams("parallel", "parallel"),
        name=name,
    )(rel_bias, jnp.asarray(idx_np))


def _compress_kernel(seq, x_ref, pe_ref, w1_ref, b1_ref, w2_ref, o_ref, xs, nat):
    xs[0:seq, :] = x_ref[...]
    xs[seq:CMP_ROWS, :] = jnp.zeros((CMP_ROWS - seq, LANES), F32)
    acc = jnp.zeros((N_CMP_SLOTS, LANES), F32)
    for l in range(CMP_BLOCK):
        rows = xs[pl.ds(l, N_CMP_SLOTS, stride=CMP_STRIDE), :] + pe_ref[l:l + 1, :]
        acc = acc + _dot(rows.astype(BF16), w1_ref[l])
    hid = _gelu_tanh(acc + b1_ref[...])
    nat[...] = _dot(hid.astype(BF16), w2_ref[...])
    for r in range(4):
        o_ref[r * LANES:(r + 1) * LANES, :] = nat[pl.ds(r, LANES, stride=4), :].astype(BF16)


def _compress(cmp_tok, pe, w1, b1, w2):
    bsz, seq, _ = cmp_tok.shape
    g = NSA_KV_GROUPS
    dh = NSA_HEAD_DIM
    return pl.pallas_call(
        functools.partial(_compress_kernel, seq),
        grid=(bsz, 2, g),
        in_specs=[pl.BlockSpec((None, seq, dh), lambda b, j, gi: (b, 0, j * g + gi)),
                  pl.BlockSpec((None, CMP_BLOCK, dh), lambda b, j, gi: (j, 0, 0)),
                  pl.BlockSpec((None, CMP_BLOCK, dh, dh), lambda b, j, gi: (j, 0, 0, 0)),
                  pl.BlockSpec((None, 1, dh), lambda b, j, gi: (j, 0, 0)),
                  pl.BlockSpec((None, dh, dh), lambda b, j, gi: (j, 0, 0))],
        out_specs=pl.BlockSpec((None, None, None, N_CMP_SLOTS, dh), lambda b, j, gi: (b, j, gi, 0, 0)),
        out_shape=jax.ShapeDtypeStruct((bsz, 2, g, N_CMP_SLOTS, dh), BF16),
        scratch_shapes=[pltpu.VMEM((CMP_ROWS, dh), F32), pltpu.VMEM((N_CMP_SLOTS, dh), F32)],
        compiler_params=_cparams("parallel", "parallel", "parallel"),
        name="nsa_compress",
    )(cmp_tok, pe, w1.astype(BF16), b1.reshape(2, 1, dh), w2.astype(BF16))


def _cmp_table_idx(tq):
    ql = np.arange(tq)[:, None, None]
    r = np.arange(4)[None, :, None]
    m = np.arange(LANES)[None, None, :]
    e = np.where(m <= (tq - CMP_BLOCK) // SEL_BLOCK, -m, LANES - m)
    dist = SEL_BLOCK * e + ql - CMP_STRIDE * r - (CMP_BLOCK - 1)
    return _t5_bucket_np(dist).reshape(tq, 4 * LANES)


def _cmp_attn_kernel(tq, n_top, q_ref, kc_ref, vc_ref, tb_ref, g_ref, o_ref, m_ref):
    i = pl.program_id(2)
    t0 = i * tq
    first_blk = i * (tq // SEL_BLOCK)
    kc = kc_ref[...]
    vc = vc_ref[...]
    gate = g_ref[...]
    ql = lax.broadcasted_iota(jnp.int32, (tq, N_CMP_SLOTS), 0)
    slot = lax.broadcasted_iota(jnp.int32, (tq, N_CMP_SLOTS), 1)
    phase = slot >> 7
    col = slot & (LANES - 1)
    dist = t0 + ql - SEL_BLOCK * col - CMP_STRIDE * phase - (CMP_BLOCK - 1)
    mask = dist >= 0
    p_sum = jnp.zeros((tq, N_CMP_SLOTS), F32)
    for hd in range(NSA_HPG):
        s = _dot_nt(q_ref[:, hd * LANES:(hd + 1) * LANES], kc)
        tb = tb_ref[hd]
        tb = jnp.concatenate(
            [pltpu.roll(tb[:, c * LANES:(c + 1) * LANES], first_blk, axis=1) for c in range(4)], axis=1)
        s = s + tb
        mx = jnp.max(jnp.where(mask, s, -NEG_BIG), axis=-1, keepdims=True)
        e = jnp.exp(jnp.where(mask, s - mx, -jnp.inf))
        p = e / jnp.maximum(jnp.sum(e, axis=-1, keepdims=True), 1e-30)
        p_sum = p_sum + p
        o_ref[:, hd * LANES:(hd + 1) * LANES] = _dot(p.astype(BF16), vc) * gate[:, hd:hd + 1]
    ph = [p_sum[:, c * LANES:(c + 1) * LANES] for c in range(4)]
    lane = lax.broadcasted_iota(jnp.int32, (tq, LANES), 1)
    prev = jnp.where(lane == 0, 0.0, pltpu.roll(ph[3], 1, axis=1))
    score = (((prev + ph[0]) + ph[1]) + ph[2]) + ph[3]
    cur = (t0 + lax.broadcasted_iota(jnp.int32, (tq, LANES), 0)) >> 6
    visible = lane <= cur
    forced = jnp.logical_or(lane == 0, jnp.logical_or(lane == cur, lane == cur - 1))
    val = jnp.where(forced, SEL_FORCE, jnp.where(visible, score, -SEL_FORCE))
    sel = jnp.zeros((tq, LANES), F32)
    for _ in range(n_top):
        mx = jnp.max(val, axis=-1, keepdims=True)
        first = jnp.min(jnp.where(val == mx, lane, LANES), axis=-1, keepdims=True)
        pick = lane == first
        sel = jnp.where(pick, 1.0, sel)
        val = jnp.where(pick, -jnp.inf, val)
    m_ref[...] = jnp.where(jnp.logical_and(sel > 0.0, visible), 0.0, NEG_MASK).astype(BF16)


def _cmp_attn(q, kvc, table, gates, n_top, tq=256):
    bsz, seq, _ = q.shape
    g = NSA_KV_GROUPS
    qw = NSA_HPG * NSA_HEAD_DIM
    return pl.pallas_call(
        functools.partial(_cmp_attn_kernel, tq, n_top),
        grid=(bsz, g, seq // tq),
        in_specs=[pl.BlockSpec((None, tq, qw), lambda b, gi, i: (b, i, gi)),
                  pl.BlockSpec((None, None, None, N_CMP_SLOTS, NSA_HEAD_DIM), lambda b, gi, i: (b, 0, gi, 0, 0)),
                  pl.BlockSpec((None, None, None, N_CMP_SLOTS, NSA_HEAD_DIM), lambda b, gi, i: (b, 1, gi, 0, 0)),
                  pl.BlockSpec((NSA_HPG, tq, N_CMP_SLOTS), lambda b, gi, i: (gi, 0, 0)),
                  pl.BlockSpec((None, tq, LANES), lambda b, gi, i: (b, i, gi))],
        out_specs=[pl.BlockSpec((None, tq, qw), lambda b, gi, i: (b, i, gi)),
                   pl.BlockSpec((None, None, tq, LANES), lambda b, gi, i: (b, gi, i, 0))],
        out_shape=[jax.ShapeDtypeStruct((bsz, seq, g * qw), F32),
                   jax.ShapeDtypeStruct((bsz, g, seq, LANES), BF16)],
        compiler_params=_cparams("parallel", "parallel", "parallel"),
        name="nsa_cmp_attn",
    )(q, kvc, kvc, table, gates)


SEL_TQ = 128
SEL_TK = 256
SEL_N_DELTA = -(-(T5_MAX_DIST - 1 + SEL_TK) // SEL_TQ)
SEL_N_NEAR = -(-SEL_N_DELTA // (SEL_TK // SEL_TQ))


def _sel_table_idx():
    d = np.arange(SEL_N_DELTA)[:, None, None]
    ql = np.arange(SEL_TQ)[None, :, None]
    kl = np.arange(SEL_TK)[None, None, :]
    dist = ql - kl + SEL_TQ * d
    idx = np.where(dist >= 0, _t5_bucket_np(dist), MASK_BUCKET)
    return idx.reshape(SEL_N_DELTA * SEL_TQ, SEL_TK).astype(np.int32)


def _sel_attn_kernel(q_ref, m_ref, k_ref, v_ref, oh_ref, nb_ref, g_ref, o_ref, qa, m_scr, l_scr, acc):
    tq, tk = SEL_TQ, SEL_TK
    ratio = tk // tq
    i = pl.program_id(2)
    kd = i // ratio
    sel_mask = m_ref[...]
    for hd in range(NSA_HPG):
        qa[hd * tq:(hd + 1) * tq, 0:LANES] = q_ref[:, hd * LANES:(hd + 1) * LANES]
        qa[hd * tq:(hd + 1) * tq, LANES:2 * LANES] = sel_mask
    m_scr[...] = jnp.full(m_scr.shape, -jnp.inf, F32)
    l_scr[...] = jnp.zeros_like(l_scr)
    acc[...] = jnp.zeros_like(acc)

    def tile(kt, delta):
        k0 = pl.multiple_of(kt * tk, tk)
        ka = jnp.concatenate([k_ref[pl.ds(k0, tk), :], oh_ref[pl.ds(k0, tk), :]], axis=1)
        s = _dot_nt(qa[...], ka)
        if delta is not None:
            s = s + jnp.concatenate([nb_ref[hd, delta] for hd in range(NSA_HPG)], axis=0)
        m_prev = m_scr[...]
        m_new = jnp.maximum(m_prev, jnp.max(s, axis=-1, keepdims=True))
        alpha = jnp.exp(m_prev - m_new)
        p = jnp.exp(s - m_new)
        l_scr[...] = alpha * l_scr[...] + jnp.sum(p, axis=-1, keepdims=True)
        acc[...] = alpha * acc[...] + _dot(p.astype(BF16), v_ref[pl.ds(k0, tk), :])
        m_scr[...] = m_new

    def far_tile(kt, carry):
        tile(kt, None)
        return carry

    lax.fori_loop(0, jnp.maximum(kd - (SEL_N_NEAR - 1), 0), far_tile, 0)
    for back in range(SEL_N_NEAR - 1, -1, -1):
        kt = kd - back

        @pl.when(kt >= 0)
        def _(kt=kt, back=back):
            tile(kt, (i - kd * ratio) + back * ratio)

    gate = g_ref[...]
    out = acc[...] / jnp.maximum(l_scr[...], 1e-30)
    for hd in range(NSA_HPG):
        o_ref[:, hd * LANES:(hd + 1) * LANES] = (
            out[hd * tq:(hd + 1) * tq, :] * gate[:, NSA_HPG + hd:NSA_HPG + hd + 1])


def _sel_attn(q, sel_mask, kv_tok, onehot, table, gates):
    bsz, seq, _ = q.shape
    g = NSA_KV_GROUPS
    dh = NSA_HEAD_DIM
    qw = NSA_HPG * dh
    tq, tk = SEL_TQ, SEL_TK
    rows = NSA_HPG * tq
    return pl.pallas_call(
        _sel_attn_kernel,
        grid=(bsz, g, seq // tq),
        in_specs=[pl.BlockSpec((None, tq, qw), lambda b, gi, i: (b, i, gi)),
                  pl.BlockSpec((None, None, tq, LANES), lambda b, gi, i: (b, gi, i, 0)),
                  pl.BlockSpec((None, seq, dh), lambda b, gi, i: (b, 0, gi)),
                  pl.BlockSpec((None, seq, dh), lambda b, gi, i: (b, 0, g + gi)),
                  pl.BlockSpec((seq, LANES), lambda b, gi, i: (0, 0)),
                  pl.BlockSpec((NSA_HPG, SEL_N_DELTA, tq, tk), lambda b, gi, i: (gi, 0, 0, 0)),
                  pl.BlockSpec((None, tq, LANES), lambda b, gi, i: (b, i, gi))],
        out_specs=pl.BlockSpec((None, tq, qw), lambda b, gi, i: (b, i, gi)),
        out_shape=jax.ShapeDtypeStruct((bsz, seq, g * qw), F32),
        scratch_shapes=[pltpu.VMEM((rows, 2 * LANES), BF16), pltpu.VMEM((rows, 1), F32),
                        pltpu.VMEM((rows, 1), F32), pltpu.VMEM((rows, dh), F32)],
        compiler_params=_cparams("parallel", "parallel", "parallel"),
        name="nsa_sel_attn",
    )(q, sel_mask, kv_tok, kv_tok, onehot, table, gates)


WIN_TQ = 128
WIN_KEYS = WINDOW + WIN_TQ


def _win_table_idx():
    ql = np.arange(WIN_TQ)[:, None]
    kl = np.arange(WIN_KEYS)[None, :]
    dist = ql + WINDOW - kl
    return np.where((dist >= 0) & (dist < WINDOW), _t5_bucket_np(dist), MASK_BUCKET).astype(np.int32)


def _win_attn_kernel(seq, q_ref, k_ref, v_ref, wb_ref, g_ref, o_ref, kpad, vpad, qs):
    tq = WIN_TQ
    i = pl.program_id(2)

    @pl.when(i == 0)
    def _():
        kpad[0:WINDOW, :] = jnp.zeros((WINDOW, LANES), BF16)
        vpad[0:WINDOW, :] = jnp.zeros((WINDOW, LANES), BF16)
        kpad[WINDOW:WINDOW + seq, :] = k_ref[...]
        vpad[WINDOW:WINDOW + seq, :] = v_ref[...]

    t0 = pl.multiple_of(i * tq, tq)
    for hd in range(NSA_HPG):
        qs[hd * tq:(hd + 1) * tq, :] = q_ref[:, hd * LANES:(hd + 1) * LANES]
    keys = kpad[pl.ds(t0, WIN_KEYS), :]
    vals = vpad[pl.ds(t0, WIN_KEYS), :]
    s = _dot_nt(qs[...], keys) + wb_ref[...].reshape(NSA_HPG * tq, WIN_KEYS)
    kl = lax.broadcasted_iota(jnp.int32, s.shape, 1)
    s = jnp.where(kl >= WINDOW - t0, s, NEG_MASK)
    mx = jnp.max(s, axis=-1, keepdims=True)
    e = jnp.exp(s - mx)
    p = e / jnp.maximum(jnp.sum(e, axis=-1, keepdims=True), 1e-30)
    out = _dot(p.astype(BF16), vals)
    gate = g_ref[...]
    for hd in range(NSA_HPG):
        o_ref[:, hd * LANES:(hd + 1) * LANES] = (
            out[hd * tq:(hd + 1) * tq, :] * gate[:, 2 * NSA_HPG + hd:2 * NSA_HPG + hd + 1])


def _win_attn(q, kv_tok, table, gates):
    bsz, seq, _ = q.shape
    g = NSA_KV_GROUPS
    dh = NSA_HEAD_DIM
    qw = NSA_HPG * dh
    tq = WIN_TQ
    return pl.pallas_call(
        functools.partial(_win_attn_kernel, seq),
        grid=(bsz, g, seq // tq),
        in_specs=[pl.BlockSpec((None, tq, qw), lambda b, gi, i: (b, i, gi)),
                  pl.BlockSpec((None, seq, dh), lambda b, gi, i: (b, 0, 2 * g + gi)),
                  pl.BlockSpec((None, seq, dh), lambda b, gi, i: (b, 0, 3 * g + gi)),
                  pl.BlockSpec((NSA_HPG, tq, WIN_KEYS), lambda b, gi, i: (gi, 0, 0)),
                  pl.BlockSpec((None, tq, LANES), lambda b, gi, i: (b, i, gi))],
        out_specs=pl.BlockSpec((None, tq, qw), lambda b, gi, i: (b, i, gi)),
        out_shape=jax.ShapeDtypeStruct((bsz, seq, g * qw), F32),
        scratch_shapes=[pltpu.VMEM((seq + WINDOW, dh), BF16), pltpu.VMEM((seq + WINDOW, dh), BF16),
                        pltpu.VMEM((NSA_HPG * tq, dh), BF16)],
        compiler_params=_cparams("parallel", "parallel", "arbitrary"),
        name="nsa_win_attn",
    )(q, kv_tok, kv_tok, table, gates)


def _nsa_mixer(h, bsz, seq, rel_bias, w_in, cmp_pe, cmp_w1, cmp_b1, cmp_w2):
    g, hpg, dh = NSA_KV_GROUPS, NSA_HPG, NSA_HEAD_DIM
    qw = NSA_HEADS * dh
    kvw = g * dh
    n_main = qw + 2 * N_BRANCH * kvw
    scale = dh ** -0.5
    q, cmp_tok, kv_tok = _proj(
        h, w_in[:, :n_main].astype(BF16),
        [(qw, lambda acc: acc * scale, BF16), (2 * kvw, lambda acc: acc, F32), (4 * kvw, lambda acc: acc, BF16)],
        name="nsa_proj")
    src = np.zeros((g * LANES,), np.int32)
    valid = np.zeros((g * LANES,), bool)
    for gi in range(g):
        for br in range(N_BRANCH):
            for hd in range(hpg):
                src[gi * LANES + br * hpg + hd] = n_main + br * NSA_HEADS + gi * hpg + hd
                valid[gi * LANES + br * hpg + hd] = True
    w_gate = jnp.where(jnp.asarray(valid)[None, :], w_in[:, src], 0.0).astype(BF16)
    (gates,) = _proj(h, w_gate, [(g * LANES, lambda acc: _sigmoid(acc), F32)], name="nsa_gate_proj")

    q = q.reshape(bsz, seq, qw)
    gates = gates.reshape(bsz, seq, g * LANES)
    kv_tok = kv_tok.reshape(bsz, seq, 4 * kvw)
    kvc = _compress(cmp_tok.reshape(bsz, seq, 2 * kvw), cmp_pe, cmp_w1, cmp_b1, cmp_w2)

    n_top = min(SEL_TOP_N, seq // SEL_BLOCK)
    cmp_tq = 256
    cmp_table = _bias_table(rel_bias, _cmp_table_idx(cmp_tq), False, "nsa_cmp_bias")
    o_c, sel_mask = _cmp_attn(q, kvc, cmp_table, gates, n_top, cmp_tq)

    sel_table = _bias_table(rel_bias, _sel_table_idx(), True, "nsa_sel_bias")
    sel_table = sel_table.reshape(NSA_HEADS, SEL_N_DELTA, SEL_TQ, SEL_TK)
    onehot = (jnp.arange(seq, dtype=jnp.int32)[:, None] // SEL_BLOCK
              == jnp.arange(LANES, dtype=jnp.int32)[None, :]).astype(BF16)
    o_s = _sel_attn(q, sel_mask, kv_tok, onehot, sel_table, gates)

    win_table = _bias_table(rel_bias, _win_table_idx(), False, "nsa_win_bias")
    o_w = _win_attn(q, kv_tok, win_table, gates)
    t = bsz * seq
    return [o_c.reshape(t, qw), o_s.reshape(t, qw), o_w.reshape(t, qw)]


def _rotary_epi(scale, acc, cos, sin):
    half = cos.shape[-1]
    outs = []
    for hd in range(acc.shape[1] // (2 * half)):
        x1 = acc[:, 2 * hd * half:(2 * hd + 1) * half]
        x2 = acc[:, (2 * hd + 1) * half:(2 * hd + 2) * half]
        outs += [x1 * cos - x2 * sin, x1 * sin + x2 * cos]
    rot = jnp.concatenate(outs, axis=1)
    return rot * scale if scale != 1.0 else rot


def _ret_kernel(tc, q_ref, k_ref, v_ref, g_ref, dec_ref, xi_ref, zeta_ref, cd_ref, gw_ref, gb_ref,
                o_ref, state):
    i = pl.program_id(2)

    @pl.when(i == 0)
    def _():
        state[...] = jnp.zeros_like(state)

    for c in range(tc // RET_CHUNK):
        rows = slice(c * RET_CHUNK, (c + 1) * RET_CHUNK)
        qc = q_ref[rows, :]
        kf = k_ref[rows, :]
        vc = v_ref[rows, :]
        inner = _dot_nt(qc, kf.astype(BF16)) * dec_ref[...]
        st = state[...]
        out = _dot(inner.astype(BF16), vc) + _dot(qc, st.astype(BF16)) * xi_ref[...]
        kz = (kf * zeta_ref[...]).astype(BF16)
        state[...] = st * cd_ref[...] + _dot_tn(kz, vc)
        mu = jnp.mean(out, axis=-1, keepdims=True)
        oc = out - mu
        var = jnp.mean(oc * oc, axis=-1, keepdims=True)
        y = (oc * lax.rsqrt(var + GN_EPS)) * gw_ref[...] + gb_ref[...]
        o_ref[rows, :] = (g_ref[rows, :] * y).astype(BF16)


def _ret_core(q, k, v, g, gn_w, gn_b, tc=512):
    bsz, seq, qk_total = q.shape
    v_total = v.shape[-1]
    nh = RET_HEADS
    dk = qk_total // nh
    dv = v_total // nh
    cs = RET_CHUNK
    log_g = jnp.log1p(-jnp.exp2(-5.0 - jnp.arange(nh, dtype=F32)))
    idx = jnp.arange(cs, dtype=F32)
    diff = idx[:, None] - idx[None, :]
    decay = jnp.where(diff >= 0, jnp.exp(jnp.maximum(diff, 0.0) * log_g[:, None, None]), 0.0)
    xi = jnp.exp((idx + 1.0) * log_g[:, None])[:, :, None]
    zeta = jnp.exp((cs - 1.0 - idx) * log_g[:, None])[:, :, None]
    chunk_decay = jnp.exp(cs * log_g).reshape(nh, 1, 1)
    return pl.pallas_call(
        functools.partial(_ret_kernel, tc),
        grid=(bsz, nh, seq // tc),
        in_specs=[pl.BlockSpec((None, tc, dk), lambda b, hd, i: (b, i, hd)),
                  pl.BlockSpec((None, tc, dk), lambda b, hd, i: (b, i, hd)),
                  pl.BlockSpec((None, tc, dv), lambda b, hd, i: (b, i, hd)),
                  pl.BlockSpec((None, tc, dv), lambda b, hd, i: (b, i, hd)),
                  pl.BlockSpec((None, cs, cs), lambda b, hd, i: (hd, 0, 0)),
                  pl.BlockSpec((None, cs, 1), lambda b, hd, i: (hd, 0, 0)),
                  pl.BlockSpec((None, cs, 1), lambda b, hd, i: (hd, 0, 0)),
                  pl.BlockSpec((None, 1, 1), lambda b, hd, i: (hd, 0, 0)),
                  pl.BlockSpec((1, dv), lambda b, hd, i: (0, hd)),
                  pl.BlockSpec((1, dv), lambda b, hd, i: (0, hd))],
        out_specs=pl.BlockSpec((None, tc, dv), lambda b, hd, i: (b, i, hd)),
        out_shape=jax.ShapeDtypeStruct((bsz, seq, v_total), BF16),
        scratch_shapes=[pltpu.VMEM((dk, dv), F32)],
        compiler_params=_cparams("parallel", "parallel", "arbitrary"),
        name="ret_core",
    )(q, k, v, g, decay, xi, zeta, chunk_decay, gn_w.reshape(1, v_total), gn_b.reshape(1, v_total))


def _ret_mixer(h, bsz, seq, w_in, gn_w, gn_b, tm=1024):
    qk_total = w_in.shape[0]
    v_total = (w_in.shape[1] - 2 * qk_total) // 2
    dk = qk_total // RET_HEADS
    half = dk // 2
    pos = jnp.arange(seq, dtype=F32)
    inv = ROPE_BASE ** (-jnp.arange(half, dtype=F32) / half)
    ang = pos[:, None] * inv[None, :]
    cos, sin = jnp.cos(ang), jnp.sin(ang)
    tpb = seq // tm
    rope_spec = pl.BlockSpec((tm, half), lambda i, j: (i % tpb, 0))
    q, k, v, g = _proj(
        h, w_in.astype(BF16),
        [(qk_total, functools.partial(_rotary_epi, 1.0), BF16),
         (qk_total, functools.partial(_rotary_epi, dk ** -0.5), F32),
         (v_total, lambda acc, cos, sin: acc, BF16),
         (v_total, lambda acc, cos, sin: _silu(acc), F32)],
        extra=(cos, sin), extra_specs=(rope_spec, rope_spec), tm=tm, name="ret_proj")
    a = _ret_core(q.reshape(bsz, seq, qk_total), k.reshape(bsz, seq, qk_total),
                  v.reshape(bsz, seq, v_total), g.reshape(bsz, seq, v_total), gn_w, gn_b)
    return [a.reshape(bsz * seq, v_total)]


def kernel(x, c, rel_bias, ada_w, ada_b, ln_w, ln_b, mlp_w1, mlp_w2, lru_w_in, lru_conv_w, lru_conv_b,
           lru_gate_w, lru_gate_b, lru_lambda, lru_w_out, nsa_w_in, nsa_cmp_pe, nsa_cmp_w1, nsa_cmp_b1,
           nsa_cmp_w2, nsa_w_out, ret_w_in, ret_gn_w, ret_gn_b, ret_w_out):
    bsz, seq, d = x.shape
    depth = ada_w.shape[0]
    mods = _ada_mods(c, ada_w, ada_b)
    xf = x.reshape(bsz * seq, d)
    h = _modulate(xf, mods, 0, seq)
    for layer in range(depth):
        mixer, inst = layer % N_MIXERS, layer // N_MIXERS
        sub = 2 * layer
        if mixer == 0:
            a_list = _lru_mixer(h, bsz, seq, lru_w_in[inst], lru_conv_w[inst], lru_conv_b[inst],
                                lru_gate_w[inst], lru_gate_b[inst], lru_lambda[inst])
            w_out = lru_w_out[inst]
        elif mixer == 1:
            a_list = _nsa_mixer(h, bsz, seq, rel_bias, nsa_w_in[inst], nsa_cmp_pe[inst], nsa_cmp_w1[inst],
                                nsa_cmp_b1[inst], nsa_cmp_w2[inst])
            w_out = nsa_w_out[inst]
        else:
            a_list = _ret_mixer(h, bsz, seq, ret_w_in[inst], ret_gn_w[inst], ret_gn_b[inst])
            w_out = ret_w_out[inst]
        xf, h = _out_ln(a_list, w_out.astype(BF16), xf, mods, sub, sub + 1, ln_w[layer, 0], ln_b[layer, 0], seq)
        sub_next = sub + 2 if layer + 1 < depth else None
        xf, h = _mlp(h, mlp_w1[layer].astype(BF16), mlp_w2[layer].astype(BF16), xf, mods, sub + 1, sub_next,
                     ln_w[layer, 1], ln_b[layer, 1], seq)
    return xf.reshape(bsz, seq, d)
```

```python
import functools
import math

import numpy as np
import jax
import jax.numpy as jnp
from jax import lax
from jax.experimental import pallas as pl
from jax.experimental.pallas import tpu as pltpu

F32 = jnp.float32
BF16 = jnp.bfloat16

DN_DEPTH = 4
N_MIXERS = 3
DN_ALPHA = (2 * DN_DEPTH) ** 0.25
LN_EPS = 1e-5
GN_EPS = 1e-5
LRU_BLOCKS = 8
CONV_WIDTH = 4
LRU_C = 8.0
NSA_HEADS = 16
NSA_HEAD_DIM = 128
NSA_KV_GROUPS = 4
NSA_HPG = NSA_HEADS // NSA_KV_GROUPS
N_BRANCH = 3
CMP_BLOCK = 32
CMP_STRIDE = 16
SEL_BLOCK = 64
SEL_TOP_N = 16
SEL_FORCE = 1e6
WINDOW = 512
NEG_BIG = 1e30
T5_BUCKETS = 32
T5_MAX_DIST = 1024
RET_HEADS = 8
RET_CHUNK = 128
ROPE_BASE = 10000.0

LANES = 128
VMEM_LIMIT_BYTES = 56 * 2 ** 20

NEG_MASK = -(2.0 ** 100)
MASK_BUCKET = T5_BUCKETS
F32_TINY = float(np.finfo(np.float32).tiny)

N_CMP_SLOTS = 4 * LANES
CMP_ROWS = CMP_STRIDE * N_CMP_SLOTS + 4 * CMP_STRIDE


def _cparams(*sem):
    return pltpu.CompilerParams(dimension_semantics=sem, vmem_limit_bytes=VMEM_LIMIT_BYTES)


def _sigmoid(x):
    return 1.0 / (1.0 + jnp.exp(-x))


def _silu(x):
    return x * _sigmoid(x)


def _gelu_tanh(x):
    return 0.5 * x * (1.0 + jnp.tanh(math.sqrt(2.0 / math.pi) * (x + 0.044715 * (x * x * x))))


def _dot(a, b):
    return jnp.dot(a, b, preferred_element_type=F32)


def _dot_nt(a, b):
    return lax.dot_general(a, b, (((1,), (1,)), ((), ())), preferred_element_type=F32)


def _dot_tn(a, b):
    return lax.dot_general(a, b, (((0,), (0,)), ((), ())), preferred_element_type=F32)


def _ada_kernel(c_ref, w_ref, b_ref, o_ref):
    cond = _silu(c_ref[...]).astype(BF16)
    o_ref[...] = _dot(cond, w_ref[...].astype(BF16)) + b_ref[...]


def _ada_mods(c, ada_w, ada_b):
    depth, _, d, d3 = ada_w.shape
    bsz = c.shape[0]
    n = depth * 2
    rows = 8
    c_pad = jnp.zeros((rows, d), F32).at[:bsz].set(c)
    tn = 1024
    out = pl.pallas_call(
        _ada_kernel,
        grid=(n, d3 // tn),
        in_specs=[
            pl.BlockSpec((rows, d), lambda s, j: (0, 0)),
            pl.BlockSpec((None, d, tn), lambda s, j: (s, 0, j)),
            pl.BlockSpec((None, 1, tn), lambda s, j: (s, 0, j)),
        ],
        out_specs=pl.BlockSpec((None, rows, tn), lambda s, j: (s, 0, j)),
        out_shape=jax.ShapeDtypeStruct((n, rows, d3), F32),
        compiler_params=_cparams("parallel", "parallel"),
        name="ada_mods",
    )(c_pad, ada_w.reshape(n, d, d3), ada_b.reshape(n, 1, d3))
    return out[:, :bsz].reshape(n, bsz, 3, d).transpose(0, 2, 1, 3).reshape(n, 3, bsz, 1, d)


SHIFT, SCALE, GATE = 0, 1, 2


def _mod_spec(d, sub, which, tiles_per_batch):
    return pl.BlockSpec((None, None, None, 1, d),
                        lambda i, *_: (sub, which, i // tiles_per_batch, 0, 0))


def _row_spec(d):
    return pl.BlockSpec((1, d), lambda *_: (0, 0))


def _modulate_kernel(x_ref, sc_ref, sh_ref, h_ref):
    h_ref[...] = (x_ref[...] * (1.0 + sc_ref[...]) + sh_ref[...]).astype(BF16)


def _modulate(x, mods, sub, seq, tm=512):
    t, d = x.shape
    tpb = seq // tm
    return pl.pallas_call(
        _modulate_kernel,
        grid=(t // tm,),
        in_specs=[pl.BlockSpec((tm, d), lambda i: (i, 0)),
                  _mod_spec(d, sub, SCALE, tpb), _mod_spec(d, sub, SHIFT, tpb)],
        out_specs=pl.BlockSpec((tm, d), lambda i: (i, 0)),
        out_shape=jax.ShapeDtypeStruct((t, d), BF16),
        compiler_params=_cparams("parallel"),
        name="modulate",
    )(x, mods, mods)


def _proj_kernel(segs, n_extra, h_ref, w_ref, *refs):
    extra = refs[:n_extra]
    outs = refs[n_extra:]
    j = pl.program_id(1)
    acc = _dot(h_ref[...], w_ref[...])
    for (lo, hi, epi), o_ref in zip(segs, outs):
        def _store(o_ref=o_ref, epi=epi):
            o_ref[...] = epi(acc, *[e[...] for e in extra]).astype(o_ref.dtype)
        if len(segs) == 1:
            _store()
        else:
            pl.when(jnp.logical_and(j >= lo, j < hi))(_store)


def _proj(h, w, segs, extra=(), extra_specs=(), tm=1024, tn=512, name="proj"):
    t, k = h.shape
    n = w.shape[1]
    bounds, lo = [], 0
    for n_cols, epi, _ in segs:
        assert n_cols % tn == 0
        bounds.append((lo, lo + n_cols // tn, epi))
        lo += n_cols // tn
    assert lo * tn == n
    out_specs = [
        pl.BlockSpec((tm, tn), functools.partial(
            lambda i, j, lo, hi: (i, jnp.clip(j - lo, 0, hi - lo - 1)), lo=lo_, hi=hi_))
        for lo_, hi_, _ in bounds]
    out_shape = [jax.ShapeDtypeStruct((t, n_cols), dt) for n_cols, _, dt in segs]
    return pl.pallas_call(
        functools.partial(_proj_kernel, bounds, len(extra)),
        grid=(t // tm, n // tn),
        in_specs=[pl.BlockSpec((tm, k), lambda i, j: (i, 0)),
                  pl.BlockSpec((k, tn), lambda i, j: (0, j))] + list(extra_specs),
        out_specs=out_specs,
        out_shape=out_shape,
        compiler_params=_cparams("parallel", "arbitrary"),
        name=name,
    )(h, w, *extra)


def _ln_mod_store(y, x_ref, gate_ref, lnw_ref, lnb_ref, sc_ref, sh_ref, xo_ref, ho_ref):
    z = DN_ALPHA * x_ref[...] + (1.0 + gate_ref[...]) * y
    mu = jnp.mean(z, axis=-1, keepdims=True)
    zc = z - mu
    var = jnp.mean(zc * zc, axis=-1, keepdims=True)
    xn = zc * lax.rsqrt(var + LN_EPS) * lnw_ref[...] + lnb_ref[...]
    xo_ref[...] = xn
    if ho_ref is not None:
        ho_ref[...] = (xn * (1.0 + sc_ref[...]) + sh_ref[...]).astype(BF16)


def _out_ln_kernel(n_a, has_next, *refs):
    a_refs = refs[:n_a]
    w_ref, x_ref, gate_ref, lnw_ref, lnb_ref = refs[n_a:n_a + 5]
    rest = refs[n_a + 5:]
    if has_next:
        sc_ref, sh_ref, xo_ref, ho_ref = rest
    else:
        (xo_ref,), sc_ref, sh_ref, ho_ref = rest, None, None, None
    a = a_refs[0][...]
    for r in a_refs[1:]:
        a = a + r[...]
    y = _dot(a.astype(BF16), w_ref[...])
    _ln_mod_store(y, x_ref, gate_ref, lnw_ref, lnb_ref, sc_ref, sh_ref, xo_ref, ho_ref)


def _out_ln(a_list, w, x, mods, sub, sub_next, ln_w, ln_b, seq, tm=256):
    t, d = x.shape
    k = w.shape[0]
    tpb = seq // tm
    has_next = sub_next is not None
    in_specs = [pl.BlockSpec((tm, k), lambda i: (i, 0)) for _ in a_list]
    in_specs += [pl.BlockSpec((k, d), lambda i: (0, 0), pipeline_mode=pl.Buffered(1)),
                 pl.BlockSpec((tm, d), lambda i: (i, 0)),
                 _mod_spec(d, sub, GATE, tpb), _row_spec(d), _row_spec(d)]
    args = list(a_list) + [w, x, mods, ln_w.reshape(1, d), ln_b.reshape(1, d)]
    out_specs = [pl.BlockSpec((tm, d), lambda i: (i, 0))]
    out_shape = [jax.ShapeDtypeStruct((t, d), F32)]
    if has_next:
        in_specs += [_mod_spec(d, sub_next, SCALE, tpb), _mod_spec(d, sub_next, SHIFT, tpb)]
        args += [mods, mods]
        out_specs.append(pl.BlockSpec((tm, d), lambda i: (i, 0)))
        out_shape.append(jax.ShapeDtypeStruct((t, d), BF16))
    res = pl.pallas_call(
        functools.partial(_out_ln_kernel, len(a_list), has_next),
        grid=(t // tm,),
        in_specs=in_specs, out_specs=out_specs, out_shape=out_shape,
        compiler_params=_cparams("parallel"),
        name="out_ln",
    )(*args)
    return (res[0], res[1]) if has_next else (res[0], None)


def _mlp_kernel(has_next, h_ref, w1_ref, w2_ref, x_ref, gate_ref, lnw_ref, lnb_ref, *rest):
    if has_next:
        sc_ref, sh_ref, xo_ref, ho_ref, acc_ref = rest
    else:
        (xo_ref, acc_ref), sc_ref, sh_ref, ho_ref = rest, None, None, None
    j = pl.program_id(1)

    @pl.when(j == 0)
    def _():
        acc_ref[...] = jnp.zeros_like(acc_ref)

    hid = _dot(h_ref[...], w1_ref[...])
    hid = jnp.square(jnp.maximum(hid, 0.0)).astype(BF16)
    acc_ref[...] += _dot(hid, w2_ref[...])

    @pl.when(j == pl.num_programs(1) - 1)
    def _():
        _ln_mod_store(acc_ref[...], x_ref, gate_ref, lnw_ref, lnb_ref, sc_ref, sh_ref, xo_ref, ho_ref)


def _mlp(h, w1, w2, x, mods, sub, sub_next, ln_w, ln_b, seq, tm=512, tf=1024):
    t, d = x.shape
    ff = w1.shape[1]
    tpb = seq // tm
    has_next = sub_next is not None
    in_specs = [pl.BlockSpec((tm, d), lambda i, j: (i, 0)),
                pl.BlockSpec((d, tf), lambda i, j: (0, j)),
                pl.BlockSpec((tf, d), lambda i, j: (j, 0)),
                pl.BlockSpec((tm, d), lambda i, j: (i, 0)),
                _mod_spec(d, sub, GATE, tpb), _row_spec(d), _row_spec(d)]
    args = [h, w1, w2, x, mods, ln_w.reshape(1, d), ln_b.reshape(1, d)]
    out_specs = [pl.BlockSpec((tm, d), lambda i, j: (i, 0))]
    out_shape = [jax.ShapeDtypeStruct((t, d), F32)]
    if has_next:
        in_specs += [_mod_spec(d, sub_next, SCALE, tpb), _mod_spec(d, sub_next, SHIFT, tpb)]
        args += [mods, mods]
        out_specs.append(pl.BlockSpec((tm, d), lambda i, j: (i, 0)))
        out_shape.append(jax.ShapeDtypeStruct((t, d), BF16))
    res = pl.pallas_call(
        functools.partial(_mlp_kernel, has_next),
        grid=(t // tm, ff // tf),
        in_specs=in_specs, out_specs=out_specs, out_shape=out_shape,
        scratch_shapes=[pltpu.VMEM((tm, d), F32)],
        compiler_params=_cparams("parallel", "arbitrary"),
        name="mlp",
    )(*args)
    return (res[0], res[1]) if has_next else (res[0], None)


HALO = 8


def _lru_scan_kernel(ts, gy_ref, xb_ref, cw_ref, cb_ref, gw_ref, gb_ref, lam_ref, o_ref,
                     buf, a_scr, u_scr, h_scr):
    i = pl.program_id(1)
    width = xb_ref.shape[-1]
    bd = width // LRU_BLOCKS

    @pl.when(i == 0)
    def _():
        buf[0:HALO, :] = jnp.zeros((HALO, width), F32)
        h_scr[...] = jnp.zeros_like(h_scr)

    @pl.when(i > 0)
    def _():
        buf[0:HALO, :] = buf[ts:ts + HALO, :]

    buf[HALO:HALO + ts, :] = xb_ref[...]
    cw = cw_ref[...]
    taps = [buf[HALO - (CONV_WIDTH - 1) + k:HALO - (CONV_WIDTH - 1) + k + ts, :] * cw[k:k + 1, :]
            for k in range(CONV_WIDTH)]
    conv = taps[0]
    for tap in taps[1:]:
        conv = conv + tap
    xc = cb_ref[...] + conv
    xcb = xc.astype(BF16)
    gates = []
    for jg in range(2):
        cols = [_dot(xcb[:, n * bd:(n + 1) * bd], gw_ref[jg, n]) for n in range(LRU_BLOCKS)]
        gates.append(jnp.concatenate(cols, axis=1) + gb_ref[jg:jg + 1, :])
    r = 0.5 * (1.0 + jnp.tanh(0.5 * gates[0]))
    ig = 0.5 * (1.0 + jnp.tanh(0.5 * gates[1]))
    neg_lam = -lam_ref[...]
    softplus = jnp.maximum(neg_lam, 0.0) + jnp.log1p(jnp.exp(-jnp.abs(neg_lam)))
    log_a = -LRU_C * r * softplus
    a = jnp.exp(log_a)
    a_scr[...] = a
    one_m_a2 = -jnp.tanh(log_a) * (1.0 + a * a)
    root = one_m_a2 * lax.rsqrt(jnp.maximum(one_m_a2, F32_TINY))
    u_scr[...] = root * (ig * xc)

    def step(t, h):
        h = a_scr[pl.ds(t, 1), :] * h + u_scr[pl.ds(t, 1), :]
        u_scr[pl.ds(t, 1), :] = h
        return h

    h_scr[...] = lax.fori_loop(0, ts, step, h_scr[...], unroll=8)
    o_ref[...] = (gy_ref[...] * u_scr[...]).astype(BF16)


def _lru_scan(gy, xb, conv_w, conv_b, gate_w, gate_b, lam, ts=512):
    bsz, seq, width = xb.shape
    bd = width // LRU_BLOCKS
    tile = pl.BlockSpec((None, ts, width), lambda b, i: (b, i, 0))
    return pl.pallas_call(
        functools.partial(_lru_scan_kernel, ts),
        grid=(bsz, seq // ts),
        in_specs=[tile, tile,
                  pl.BlockSpec((CONV_WIDTH, width), lambda b, i: (0, 0)),
                  pl.BlockSpec((1, width), lambda b, i: (0, 0)),
                  pl.BlockSpec((2, LRU_BLOCKS, bd, bd), lambda b, i: (0, 0, 0, 0)),
                  pl.BlockSpec((2, width), lambda b, i: (0, 0)),
                  pl.BlockSpec((1, width), lambda b, i: (0, 0))],
        out_specs=tile,
        out_shape=jax.ShapeDtypeStruct((bsz, seq, width), BF16),
        scratch_shapes=[pltpu.VMEM((ts + HALO, width), F32), pltpu.VMEM((ts, width), F32),
                        pltpu.VMEM((ts, width), F32), pltpu.VMEM((1, width), F32)],
        compiler_params=_cparams("parallel", "arbitrary"),
        name="lru_scan",
    )(gy, xb, conv_w, conv_b.reshape(1, width), gate_w.astype(BF16), gate_b, lam.reshape(1, width))


def _lru_mixer(h, bsz, seq, w_in, conv_w, conv_b, gate_w, gate_b, lam):
    width = w_in.shape[1] // 2
    gy, xb = _proj(h, w_in.astype(BF16),
                   [(width, lambda acc: _gelu_tanh(acc), F32), (width, lambda acc: acc, F32)],
                   name="lru_proj")
    a = _lru_scan(gy.reshape(bsz, seq, width), xb.reshape(bsz, seq, width),
                  conv_w, conv_b, gate_w, gate_b, lam)
    return [a.reshape(bsz * seq, width)]


def _t5_bucket_np(dist):
    n = np.maximum(np.asarray(dist, np.int64), 0)
    max_exact = T5_BUCKETS // 2
    n_large = T5_BUCKETS - max_exact
    ratio = T5_MAX_DIST // max_exact
    thresholds = []
    for k in range(1, n_large):
        m = max_exact
        while m ** n_large < ratio ** k * max_exact ** n_large:
            m += 1
        thresholds.append(m)
    large = max_exact + sum((n >= th).astype(np.int64) for th in thresholds)
    return np.where(n < max_exact, n, large).astype(np.int32)


def _bias_table_kernel(sub_far, rb_ref, idx_ref, o_ref):
    hd = pl.program_id(0)
    idx = idx_ref[...]
    base = rb_ref[T5_BUCKETS - 1, hd] if sub_far else 0.0
    acc = jnp.full(idx.shape, NEG_MASK, F32)
    for b in range(T5_BUCKETS):
        acc = jnp.where(idx == b, rb_ref[b, hd] - base, acc)
    o_ref[...] = acc


def _bias_table(rel_bias, idx_np, sub_far, name):
    rows, cols = idx_np.shape
    tr = 128
    n_heads = rel_bias.shape[1]
    return pl.pallas_call(
        functools.partial(_bias_table_kernel, sub_far),
        grid=(n_heads, rows // tr),
        in_specs=[pl.BlockSpec(memory_space=pltpu.SMEM),
                  pl.BlockSpec((tr, cols), lambda hd, i: (i, 0))],
        out_specs=pl.BlockSpec((None, tr, cols), lambda hd, i: (hd, i, 0)),
        out_shape=jax.ShapeDtypeStruct((n_heads, rows, cols), F32),
        compiler_params=_cparams("parallel", "parallel"),
        name=name,
    )(rel_bias, jnp.asarray(idx_np))


def _compress_kernel(seq, x_ref, pe_ref, w1_ref, b1_ref, w2_ref, o_ref, xs, nat):
    xs[0:seq, :] = x_ref[...]
    xs[seq:CMP_ROWS, :] = jnp.zeros((CMP_ROWS - seq, LANES), F32)
    acc = jnp.zeros((N_CMP_SLOTS, LANES), F32)
    for l in range(CMP_BLOCK):
        rows = xs[pl.ds(l, N_CMP_SLOTS, stride=CMP_STRIDE), :] + pe_ref[l:l + 1, :]
        acc = acc + _dot(rows.astype(BF16), w1_ref[l])
    hid = _gelu_tanh(acc + b1_ref[...])
    nat[...] = _dot(hid.astype(BF16), w2_ref[...])
    for r in range(4):
        o_ref[r * LANES:(r + 1) * LANES, :] = nat[pl.ds(r, LANES, stride=4), :].astype(BF16)


def _compress(cmp_tok, pe, w1, b1, w2):
    bsz, seq, _ = cmp_tok.shape
    g = NSA_KV_GROUPS
    dh = NSA_HEAD_DIM
    return pl.pallas_call(
        functools.partial(_compress_kernel, seq),
        grid=(bsz, 2, g),
        in_specs=[pl.BlockSpec((None, seq, dh), lambda b, j, gi: (b, 0, j * g + gi)),
                  pl.BlockSpec((None, CMP_BLOCK, dh), lambda b, j, gi: (j, 0, 0)),
                  pl.BlockSpec((None, CMP_BLOCK, dh, dh), lambda b, j, gi: (j, 0, 0, 0)),
                  pl.BlockSpec((None, 1, dh), lambda b, j, gi: (j, 0, 0)),
                  pl.BlockSpec((None, dh, dh), lambda b, j, gi: (j, 0, 0))],
        out_specs=pl.BlockSpec((None, None, None, N_CMP_SLOTS, dh), lambda b, j, gi: (b, j, gi, 0, 0)),
        out_shape=jax.ShapeDtypeStruct((bsz, 2, g, N_CMP_SLOTS, dh), BF16),
        scratch_shapes=[pltpu.VMEM((CMP_ROWS, dh), F32), pltpu.VMEM((N_CMP_SLOTS, dh), F32)],
        compiler_params=_cparams("parallel", "parallel", "parallel"),
        name="nsa_compress",
    )(cmp_tok, pe, w1.astype(BF16), b1.reshape(2, 1, dh), w2.astype(BF16))


def _cmp_table_idx(tq):
    ql = np.arange(tq)[:, None, None]
    r = np.arange(4)[None, :, None]
    m = np.arange(LANES)[None, None, :]
    e = np.where(m <= (tq - CMP_BLOCK) // SEL_BLOCK, -m, LANES - m)
    dist = SEL_BLOCK * e + ql - CMP_STRIDE * r - (CMP_BLOCK - 1)
    return _t5_bucket_np(dist).reshape(tq, 4 * LANES)


def _cmp_attn_kernel(tq, q_ref, kc_ref, vc_ref, tb_ref, g_ref, o_ref, val_ref):
    i = pl.program_id(2)
    t0 = i * tq
    first_blk = i * (tq // SEL_BLOCK)
    kc = kc_ref[...]
    vc = vc_ref[...]
    gate = g_ref[...]
    ql = lax.broadcasted_iota(jnp.int32, (tq, N_CMP_SLOTS), 0)
    slot = lax.broadcasted_iota(jnp.int32, (tq, N_CMP_SLOTS), 1)
    phase = slot >> 7
    col = slot & (LANES - 1)
    dist = t0 + ql - SEL_BLOCK * col - CMP_STRIDE * phase - (CMP_BLOCK - 1)
    causal = jnp.where(dist >= 0, 0.0, NEG_MASK)
    row_ok = (t0 + lax.broadcasted_iota(jnp.int32, (tq, 1), 0)) >= CMP_BLOCK - 1
    p_sum = jnp.zeros((tq, N_CMP_SLOTS), F32)
    for hd in range(NSA_HPG):
        tb = tb_ref[hd]
        tb = jnp.concatenate(
            [pltpu.roll(tb[:, c * LANES:(c + 1) * LANES], first_blk, axis=1) for c in range(4)], axis=1)
        s = _dot_nt(q_ref[:, hd * LANES:(hd + 1) * LANES], kc) + tb + causal
        e = jnp.exp(s - jnp.max(s, axis=-1, keepdims=True))
        p = e * jnp.where(row_ok, 1.0 / jnp.sum(e, axis=-1, keepdims=True), 0.0)
        p_sum = p_sum + p
        o_ref[:, hd * LANES:(hd + 1) * LANES] = _dot(p.astype(BF16), vc) * gate[:, hd:hd + 1]
    ph = [p_sum[:, c * LANES:(c + 1) * LANES] for c in range(4)]
    lane = lax.broadcasted_iota(jnp.int32, (tq, LANES), 1)
    prev = jnp.where(lane == 0, 0.0, pltpu.roll(ph[3], 1, axis=1))
    score = (((prev + ph[0]) + ph[1]) + ph[2]) + ph[3]
    cur = (t0 + lax.broadcasted_iota(jnp.int32, (tq, LANES), 0)) >> 6
    forced = jnp.logical_or(lane == 0, jnp.logical_or(lane == cur, lane == cur - 1))
    val_ref[...] = jnp.where(forced, SEL_FORCE, jnp.where(lane <= cur, score, -SEL_FORCE))


def _topk_mask_kernel(n_top, val_ref, m_ref):
    val0 = val_ref[...]
    lane = lax.broadcasted_iota(jnp.int32, val0.shape, 1).astype(F32)
    val = val0
    sel = jnp.zeros(val0.shape, F32)
    for _ in range(n_top):
        mx = jnp.max(val, axis=-1, keepdims=True)
        first = jnp.min(jnp.where(val == mx, lane, float(LANES)), axis=-1, keepdims=True)
        pick = lane == first
        sel = jnp.where(pick, 1.0, sel)
        val = jnp.where(pick, -jnp.inf, val)
    m_ref[...] = jnp.where(jnp.logical_and(sel > 0.0, val0 >= 0.0), 0.0, NEG_MASK).astype(BF16)


def _topk_mask(val, n_top, tr=1024):
    rows = val.shape[0]
    return pl.pallas_call(
        functools.partial(_topk_mask_kernel, n_top),
        grid=(rows // tr,),
        in_specs=[pl.BlockSpec((tr, LANES), lambda i: (i, 0))],
        out_specs=pl.BlockSpec((tr, LANES), lambda i: (i, 0)),
        out_shape=jax.ShapeDtypeStruct((rows, LANES), BF16),
        compiler_params=_cparams("parallel"),
        name="nsa_topk_mask",
    )(val)


def _cmp_attn(q, kvc, table, gates, tq=256):
    bsz, seq, _ = q.shape
    g = NSA_KV_GROUPS
    qw = NSA_HPG * NSA_HEAD_DIM
    return pl.pallas_call(
        functools.partial(_cmp_attn_kernel, tq),
        grid=(bsz, g, seq // tq),
        in_specs=[pl.BlockSpec((None, tq, qw), lambda b, gi, i: (b, i, gi)),
                  pl.BlockSpec((None, None, None, N_CMP_SLOTS, NSA_HEAD_DIM), lambda b, gi, i: (b, 0, gi, 0, 0)),
                  pl.BlockSpec((None, None, None, N_CMP_SLOTS, NSA_HEAD_DIM), lambda b, gi, i: (b, 1, gi, 0, 0)),
                  pl.BlockSpec((NSA_HPG, tq, N_CMP_SLOTS), lambda b, gi, i: (gi, 0, 0)),
                  pl.BlockSpec((None, tq, LANES), lambda b, gi, i: (b, i, gi))],
        out_specs=[pl.BlockSpec((None, tq, qw), lambda b, gi, i: (b, i, gi)),
                   pl.BlockSpec((None, None, tq, LANES), lambda b, gi, i: (b, gi, i, 0))],
        out_shape=[jax.ShapeDtypeStruct((bsz, seq, g * qw), F32),
                   jax.ShapeDtypeStruct((bsz, g, seq, LANES), F32)],
        compiler_params=_cparams("parallel", "parallel", "parallel"),
        name="nsa_cmp_attn",
    )(q, kvc, kvc, table, gates)


SEL_TQ = 128
SEL_TK = 256
SEL_N_DELTA = -(-(T5_MAX_DIST - 1 + SEL_TK) // SEL_TQ)
SEL_N_NEAR = -(-SEL_N_DELTA // (SEL_TK // SEL_TQ))
SEL_STREAMS = 4


def _sel_table_idx():
    d = np.arange(SEL_N_DELTA)[:, None, None]
    ql = np.arange(SEL_TQ)[None, :, None]
    kl = np.arange(SEL_TK)[None, None, :]
    dist = ql - kl + SEL_TQ * d
    idx = np.where(dist >= 0, _t5_bucket_np(dist), MASK_BUCKET)
    return idx.reshape(SEL_N_DELTA * SEL_TQ, SEL_TK).astype(np.int32)


def _sel_attn_kernel(q_ref, m_ref, k_ref, v_ref, oh_ref, nb_ref, g_ref, o_ref, qa, m_scr, l_scr, acc):
    tq, tk = SEL_TQ, SEL_TK
    ratio = tk // tq
    i = pl.program_id(2)
    kd = i // ratio
    sel_mask = m_ref[...]
    for hd in range(NSA_HPG):
        qa[hd * tq:(hd + 1) * tq, 0:LANES] = q_ref[:, hd * LANES:(hd + 1) * LANES]
        qa[hd * tq:(hd + 1) * tq, LANES:2 * LANES] = sel_mask
    m_scr[...] = jnp.full(m_scr.shape, -jnp.inf, F32)
    l_scr[...] = jnp.zeros_like(l_scr)
    acc[...] = jnp.zeros_like(acc)
    n_kt = k_ref.shape[0] // tk

    def tile(st, kt, valid, delta):
        k0 = pl.multiple_of(jnp.clip(kt, 0, n_kt - 1) * tk, tk)
        ka = jnp.concatenate([k_ref[pl.ds(k0, tk), :], oh_ref[pl.ds(k0, tk), :]], axis=1)
        s = _dot_nt(qa[...], ka) + jnp.where(valid, 0.0, NEG_MASK)
        if delta is not None:
            s = s + jnp.concatenate([nb_ref[hd, delta] for hd in range(NSA_HPG)], axis=0)
        m_prev = m_scr[st]
        m_new = jnp.maximum(m_prev, jnp.max(s, axis=-1, keepdims=True))
        alpha = jnp.exp(m_prev - m_new)
        p = jnp.exp(s - jnp.concatenate([m_new] * (tk // LANES), axis=1))
        l_scr[st] = alpha * l_scr[st] + jnp.sum(p, axis=-1, keepdims=True)
        acc[st] = alpha * acc[st] + _dot(p.astype(BF16), v_ref[pl.ds(k0, tk), :])
        m_scr[st] = m_new

    n_far = jnp.maximum(kd - (SEL_N_NEAR - 1), 0)

    def far_group(it, carry):
        for st in range(SEL_STREAMS):
            kt = it * SEL_STREAMS + st
            tile(st, kt, kt < n_far, None)
        return carry

    lax.fori_loop(0, (n_far + SEL_STREAMS - 1) // SEL_STREAMS, far_group, 0)
    for back in range(SEL_N_NEAR):
        kt = kd - back
        tile(back % SEL_STREAMS, kt, kt >= 0, (i - kd * ratio) + back * ratio)

    m_all = m_scr[0]
    for st in range(1, SEL_STREAMS):
        m_all = jnp.maximum(m_all, m_scr[st])
    num = jnp.zeros((NSA_HPG * tq, LANES), F32)
    den = jnp.zeros((NSA_HPG * tq, LANES), F32)
    for st in range(SEL_STREAMS):
        w = jnp.exp(m_scr[st] - m_all)
        num = num + w * acc[st]
        den = den + w * l_scr[st]
    gate = g_ref[...]
    out = num / jnp.maximum(den, 1e-30)
    for hd in range(NSA_HPG):
        o_ref[:, hd * LANES:(hd + 1) * LANES] = (
            out[hd * tq:(hd + 1) * tq, :] * gate[:, NSA_HPG + hd:NSA_HPG + hd + 1])


def _sel_attn(q, sel_mask, kv_tok, onehot, table, gates):
    bsz, seq, _ = q.shape
    g = NSA_KV_GROUPS
    dh = NSA_HEAD_DIM
    qw = NSA_HPG * dh
    tq, tk = SEL_TQ, SEL_TK
    rows = NSA_HPG * tq
    return pl.pallas_call(
        _sel_attn_kernel,
        grid=(bsz, g, seq // tq),
        in_specs=[pl.BlockSpec((None, tq, qw), lambda b, gi, i: (b, i, gi)),
                  pl.BlockSpec((None, None, tq, LANES), lambda b, gi, i: (b, gi, i, 0)),
                  pl.BlockSpec((None, seq, dh), lambda b, gi, i: (b, 0, gi)),
                  pl.BlockSpec((None, seq, dh), lambda b, gi, i: (b, 0, g + gi)),
                  pl.BlockSpec((seq, LANES), lambda b, gi, i: (0, 0)),
                  pl.BlockSpec((NSA_HPG, SEL_N_DELTA, tq, tk), lambda b, gi, i: (gi, 0, 0, 0)),
                  pl.BlockSpec((None, tq, LANES), lambda b, gi, i: (b, i, gi))],
        out_specs=pl.BlockSpec((None, tq, qw), lambda b, gi, i: (b, i, gi)),
        out_shape=jax.ShapeDtypeStruct((bsz, seq, g * qw), F32),
        scratch_shapes=[pltpu.VMEM((rows, 2 * LANES), BF16), pltpu.VMEM((SEL_STREAMS, rows, LANES), F32),
                        pltpu.VMEM((SEL_STREAMS, rows, LANES), F32), pltpu.VMEM((SEL_STREAMS, rows, dh), F32)],
        compiler_params=_cparams("parallel", "parallel", "parallel"),
        name="nsa_sel_attn",
    )(q, sel_mask, kv_tok, kv_tok, onehot, table, gates)


WIN_TQ = 128
WIN_SUB = 4
WIN_KEYS = WINDOW + WIN_TQ


def _win_table_idx():
    ql = np.arange(WIN_TQ)[:, None]
    kl = np.arange(WIN_KEYS)[None, :]
    dist = ql + WINDOW - kl
    return np.where((dist >= 0) & (dist < WINDOW), _t5_bucket_np(dist), MASK_BUCKET).astype(np.int32)


def _win_attn_kernel(seq, q_ref, k_ref, v_ref, wb_ref, g_ref, o_ref, kpad, vpad, qs):
    tq = WIN_TQ
    i = pl.program_id(2)

    @pl.when(i == 0)
    def _():
        kpad[0:WINDOW, :] = jnp.zeros((WINDOW, LANES), BF16)
        vpad[0:WINDOW, :] = jnp.zeros((WINDOW, LANES), BF16)
        kpad[WINDOW:WINDOW + seq, :] = k_ref[...]
        vpad[WINDOW:WINDOW + seq, :] = v_ref[...]

    bias = wb_ref[...].reshape(NSA_HPG * tq, WIN_KEYS)
    for sub in range(WIN_SUB):
        rows = slice(sub * tq, (sub + 1) * tq)
        t0 = pl.multiple_of((i * WIN_SUB + sub) * tq, tq)
        for hd in range(NSA_HPG):
            qs[sub, hd * tq:(hd + 1) * tq, :] = q_ref[rows, hd * LANES:(hd + 1) * LANES]
        keys = kpad[pl.ds(t0, WIN_KEYS), :]
        vals = vpad[pl.ds(t0, WIN_KEYS), :]
        s = _dot_nt(qs[sub], keys) + bias
        kl = lax.broadcasted_iota(jnp.int32, s.shape, 1)
        s = jnp.where(kl >= WINDOW - t0, s, NEG_MASK)
        e = jnp.exp(s - jnp.max(s, axis=-1, keepdims=True))
        p = e * (1.0 / jnp.sum(e, axis=-1, keepdims=True))
        out = _dot(p.astype(BF16), vals)
        gate = g_ref[rows, :]
        for hd in range(NSA_HPG):
            o_ref[rows, hd * LANES:(hd + 1) * LANES] = (
                out[hd * tq:(hd + 1) * tq, :] * gate[:, 2 * NSA_HPG + hd:2 * NSA_HPG + hd + 1])


def _win_attn(q, kv_tok, table, gates):
    bsz, seq, _ = q.shape
    g = NSA_KV_GROUPS
    dh = NSA_HEAD_DIM
    qw = NSA_HPG * dh
    tq = WIN_TQ
    rows = tq * WIN_SUB
    return pl.pallas_call(
        functools.partial(_win_attn_kernel, seq),
        grid=(bsz, g, seq // rows),
        in_specs=[pl.BlockSpec((None, rows, qw), lambda b, gi, i: (b, i, gi)),
                  pl.BlockSpec((None, seq, dh), lambda b, gi, i: (b, 0, 2 * g + gi)),
                  pl.BlockSpec((None, seq, dh), lambda b, gi, i: (b, 0, 3 * g + gi)),
                  pl.BlockSpec((NSA_HPG, tq, WIN_KEYS), lambda b, gi, i: (gi, 0, 0)),
                  pl.BlockSpec((None, rows, LANES), lambda b, gi, i: (b, i, gi))],
        out_specs=pl.BlockSpec((None, rows, qw), lambda b, gi, i: (b, i, gi)),
        out_shape=jax.ShapeDtypeStruct((bsz, seq, g * qw), F32),
        scratch_shapes=[pltpu.VMEM((seq + WINDOW, dh), BF16), pltpu.VMEM((seq + WINDOW, dh), BF16),
                        pltpu.VMEM((WIN_SUB, NSA_HPG * tq, dh), BF16)],
        compiler_params=_cparams("parallel", "parallel", "arbitrary"),
        name="nsa_win_attn",
    )(q, kv_tok, kv_tok, table, gates)


def _nsa_mixer(h, bsz, seq, rel_bias, w_in, cmp_pe, cmp_w1, cmp_b1, cmp_w2):
    g, hpg, dh = NSA_KV_GROUPS, NSA_HPG, NSA_HEAD_DIM
    qw = NSA_HEADS * dh
    kvw = g * dh
    n_main = qw + 2 * N_BRANCH * kvw
    scale = dh ** -0.5
    q, cmp_tok, kv_tok = _proj(
        h, w_in[:, :n_main].astype(BF16),
        [(qw, lambda acc: acc * scale, BF16), (2 * kvw, lambda acc: acc, F32), (4 * kvw, lambda acc: acc, BF16)],
        name="nsa_proj")
    src = np.zeros((g * LANES,), np.int32)
    valid = np.zeros((g * LANES,), bool)
    for gi in range(g):
        for br in range(N_BRANCH):
            for hd in range(hpg):
                src[gi * LANES + br * hpg + hd] = n_main + br * NSA_HEADS + gi * hpg + hd
                valid[gi * LANES + br * hpg + hd] = True
    w_gate = jnp.where(jnp.asarray(valid)[None, :], w_in[:, src], 0.0).astype(BF16)
    (gates,) = _proj(h, w_gate, [(g * LANES, lambda acc: _sigmoid(acc), F32)], name="nsa_gate_proj")

    q = q.reshape(bsz, seq, qw)
    gates = gates.reshape(bsz, seq, g * LANES)
    kv_tok = kv_tok.reshape(bsz, seq, 4 * kvw)
    kvc = _compress(cmp_tok.reshape(bsz, seq, 2 * kvw), cmp_pe, cmp_w1, cmp_b1, cmp_w2)

    n_top = min(SEL_TOP_N, seq // SEL_BLOCK)
    cmp_tq = 256
    cmp_table = _bias_table(rel_bias, _cmp_table_idx(cmp_tq), False, "nsa_cmp_bias")
    o_c, sel_val = _cmp_attn(q, kvc, cmp_table, gates, cmp_tq)
    sel_mask = _topk_mask(sel_val.reshape(bsz * g * seq, LANES), n_top).reshape(bsz, g, seq, LANES)

    sel_table = _bias_table(rel_bias, _sel_table_idx(), True, "nsa_sel_bias")
    sel_table = sel_table.reshape(NSA_HEADS, SEL_N_DELTA, SEL_TQ, SEL_TK)
    onehot = (jnp.arange(seq, dtype=jnp.int32)[:, None] // SEL_BLOCK
              == jnp.arange(LANES, dtype=jnp.int32)[None, :]).astype(BF16)
    o_s = _sel_attn(q, sel_mask, kv_tok, onehot, sel_table, gates)

    win_table = _bias_table(rel_bias, _win_table_idx(), False, "nsa_win_bias")
    o_w = _win_attn(q, kv_tok, win_table, gates)
    t = bsz * seq
    return [o_c.reshape(t, qw), o_s.reshape(t, qw), o_w.reshape(t, qw)]


def _rotary_epi(scale, acc, cos, sin):
    half = cos.shape[-1]
    outs = []
    for hd in range(acc.shape[1] // (2 * half)):
        x1 = acc[:, 2 * hd * half:(2 * hd + 1) * half]
        x2 = acc[:, (2 * hd + 1) * half:(2 * hd + 2) * half]
        outs += [x1 * cos - x2 * sin, x1 * sin + x2 * cos]
    rot = jnp.concatenate(outs, axis=1)
    return rot * scale if scale != 1.0 else rot


def _ret_kernel(tc, q_ref, k_ref, v_ref, g_ref, dec_ref, xi_ref, zeta_ref, cd_ref, gw_ref, gb_ref,
                o_ref, state):
    i = pl.program_id(2)

    @pl.when(i == 0)
    def _():
        state[...] = jnp.zeros_like(state)

    for c in range(tc // RET_CHUNK):
        rows = slice(c * RET_CHUNK, (c + 1) * RET_CHUNK)
        qc = q_ref[rows, :]
        kf = k_ref[rows, :]
        vc = v_ref[rows, :]
        inner = _dot_nt(qc, kf.astype(BF16)) * dec_ref[...]
        st = state[...]
        out = _dot(inner.astype(BF16), vc) + _dot(qc, st.astype(BF16)) * xi_ref[...]
        kz = (kf * zeta_ref[...]).astype(BF16)
        state[...] = st * cd_ref[...] + _dot_tn(kz, vc)
        mu = jnp.mean(out, axis=-1, keepdims=True)
        oc = out - mu
        var = jnp.mean(oc * oc, axis=-1, keepdims=True)
        y = (oc * lax.rsqrt(var + GN_EPS)) * gw_ref[...] + gb_ref[...]
        o_ref[rows, :] = (g_ref[rows, :] * y).astype(BF16)


def _ret_core(q, k, v, g, gn_w, gn_b, tc=512):
    bsz, seq, qk_total = q.shape
    v_total = v.shape[-1]
    nh = RET_HEADS
    dk = qk_total // nh
    dv = v_total // nh
    cs = RET_CHUNK
    log_g = jnp.log1p(-jnp.exp2(-5.0 - jnp.arange(nh, dtype=F32)))
    idx = jnp.arange(cs, dtype=F32)
    diff = idx[:, None] - idx[None, :]
    decay = jnp.where(diff >= 0, jnp.exp(jnp.maximum(diff, 0.0) * log_g[:, None, None]), 0.0)
    xi = jnp.exp((idx + 1.0) * log_g[:, None])[:, :, None]
    zeta = jnp.exp((cs - 1.0 - idx) * log_g[:, None])[:, :, None]
    chunk_decay = jnp.exp(cs * log_g).reshape(nh, 1, 1)
    return pl.pallas_call(
        functools.partial(_ret_kernel, tc),
        grid=(bsz, nh, seq // tc),
        in_specs=[pl.BlockSpec((None, tc, dk), lambda b, hd, i: (b, i, hd)),
                  pl.BlockSpec((None, tc, dk), lambda b, hd, i: (b, i, hd)),
                  pl.BlockSpec((None, tc, dv), lambda b, hd, i: (b, i, hd)),
                  pl.BlockSpec((None, tc, dv), lambda b, hd, i: (b, i, hd)),
                  pl.BlockSpec((None, cs, cs), lambda b, hd, i: (hd, 0, 0)),
                  pl.BlockSpec((None, cs, 1), lambda b, hd, i: (hd, 0, 0)),
                  pl.BlockSpec((None, cs, 1), lambda b, hd, i: (hd, 0, 0)),
                  pl.BlockSpec((None, 1, 1), lambda b, hd, i: (hd, 0, 0)),
                  pl.BlockSpec((1, dv), lambda b, hd, i: (0, hd)),
                  pl.BlockSpec((1, dv), lambda b, hd, i: (0, hd))],
        out_specs=pl.BlockSpec((None, tc, dv), lambda b, hd, i: (b, i, hd)),
        out_shape=jax.ShapeDtypeStruct((bsz, seq, v_total), BF16),
        scratch_shapes=[pltpu.VMEM((dk, dv), F32)],
        compiler_params=_cparams("parallel", "parallel", "arbitrary"),
        name="ret_core",
    )(q, k, v, g, decay, xi, zeta, chunk_decay, gn_w.reshape(1, v_total), gn_b.reshape(1, v_total))


def _ret_mixer(h, bsz, seq, w_in, gn_w, gn_b, tm=1024):
    qk_total = w_in.shape[0]
    v_total = (w_in.shape[1] - 2 * qk_total) // 2
    dk = qk_total // RET_HEADS
    half = dk // 2
    pos = jnp.arange(seq, dtype=F32)
    inv = ROPE_BASE ** (-jnp.arange(half, dtype=F32) / half)
    ang = pos[:, None] * inv[None, :]
    cos, sin = jnp.cos(ang), jnp.sin(ang)
    tpb = seq // tm
    rope_spec = pl.BlockSpec((tm, half), lambda i, j: (i % tpb, 0))
    q, k, v, g = _proj(
        h, w_in.astype(BF16),
        [(qk_total, functools.partial(_rotary_epi, 1.0), BF16),
         (qk_total, functools.partial(_rotary_epi, dk ** -0.5), F32),
         (v_total, lambda acc, cos, sin: acc, BF16),
         (v_total, lambda acc, cos, sin: _silu(acc), F32)],
        extra=(cos, sin), extra_specs=(rope_spec, rope_spec), tm=tm, name="ret_proj")
    a = _ret_core(q.reshape(bsz, seq, qk_total), k.reshape(bsz, seq, qk_total),
                  v.reshape(bsz, seq, v_total), g.reshape(bsz, seq, v_total), gn_w, gn_b)
    return [a.reshape(bsz * seq, v_total)]


def kernel(x, c, rel_bias, ada_w, ada_b, ln_w, ln_b, mlp_w1, mlp_w2, lru_w_in, lru_conv_w, lru_conv_b,
           lru_gate_w, lru_gate_b, lru_lambda, lru_w_out, nsa_w_in, nsa_cmp_pe, nsa_cmp_w1, nsa_cmp_b1,
           nsa_cmp_w2, nsa_w_out, ret_w_in, ret_gn_w, ret_gn_b, ret_w_out):
    bsz, seq, d = x.shape
    depth = ada_w.shape[0]
    mods = _ada_mods(c, ada_w, ada_b)
    xf = x.reshape(bsz * seq, d)
    h = _modulate(xf, mods, 0, seq)
    for layer in range(depth):
        mixer, inst = layer % N_MIXERS, layer // N_MIXERS
        sub = 2 * layer
        if mixer == 0:
            a_list = _lru_mixer(h, bsz, seq, lru_w_in[inst], lru_conv_w[inst], lru_conv_b[inst],
                                lru_gate_w[inst], lru_gate_b[inst], lru_lambda[inst])
            w_out = lru_w_out[inst]
        elif mixer == 1:
            a_list = _nsa_mixer(h, bsz, seq, rel_bias, nsa_w_in[inst], nsa_cmp_pe[inst], nsa_cmp_w1[inst],
                                nsa_cmp_b1[inst], nsa_cmp_w2[inst])
            w_out = nsa_w_out[inst]
        else:
            a_list = _ret_mixer(h, bsz, seq, ret_w_in[inst], ret_gn_w[inst], ret_gn_b[inst])
            w_out = ret_w_out[inst]
        xf, h = _out_ln(a_list, w_out.astype(BF16), xf, mods, sub, sub + 1, ln_w[layer, 0], ln_b[layer, 0], seq)
        sub_next = sub + 2 if layer + 1 < depth else None
        xf, h = _mlp(h, mlp_w1[layer].astype(BF16), mlp_w2[layer].astype(BF16), xf, mods, sub + 1, sub_next,
                     ln_w[layer, 1], ln_b[layer, 1], seq)
    return xf.reshape(bsz, seq, d)
```

```python
import functools
import math

import numpy as np
import jax
import jax.numpy as jnp
from jax import lax
from jax.experimental import pallas as pl
from jax.experimental.pallas import tpu as pltpu

F32 = jnp.float32
BF16 = jnp.bfloat16

DN_DEPTH = 4
N_MIXERS = 3
DN_ALPHA = (2 * DN_DEPTH) ** 0.25
LN_EPS = 1e-5
GN_EPS = 1e-5
LRU_BLOCKS = 8
CONV_WIDTH = 4
LRU_C = 8.0
NSA_HEADS = 16
NSA_HEAD_DIM = 128
NSA_KV_GROUPS = 4
NSA_HPG = NSA_HEADS // NSA_KV_GROUPS
N_BRANCH = 3
CMP_BLOCK = 32
CMP_STRIDE = 16
SEL_BLOCK = 64
SEL_TOP_N = 16
SEL_FORCE = 1e6
WINDOW = 512
NEG_BIG = 1e30
T5_BUCKETS = 32
T5_MAX_DIST = 1024
RET_HEADS = 8
RET_CHUNK = 128
ROPE_BASE = 10000.0

LANES = 128
VMEM_LIMIT_BYTES = 56 * 2 ** 20

NEG_MASK = -(2.0 ** 100)
MASK_BUCKET = T5_BUCKETS
F32_TINY = float(np.finfo(np.float32).tiny)

N_CMP_SLOTS = 4 * LANES
CMP_ROWS = CMP_STRIDE * N_CMP_SLOTS + 4 * CMP_STRIDE


def _cparams(*sem):
    return pltpu.CompilerParams(dimension_semantics=sem, vmem_limit_bytes=VMEM_LIMIT_BYTES)


def _sigmoid(x):
    return 1.0 / (1.0 + jnp.exp(-x))


def _silu(x):
    return x * _sigmoid(x)


def _gelu_tanh(x):
    return 0.5 * x * (1.0 + jnp.tanh(math.sqrt(2.0 / math.pi) * (x + 0.044715 * (x * x * x))))


def _dot(a, b):
    return jnp.dot(a, b, preferred_element_type=F32)


def _dot_nt(a, b):
    return lax.dot_general(a, b, (((1,), (1,)), ((), ())), preferred_element_type=F32)


def _dot_tn(a, b):
    return lax.dot_general(a, b, (((0,), (0,)), ((), ())), preferred_element_type=F32)


def _ada_kernel(c_ref, w_ref, b_ref, o_ref):
    cond = _silu(c_ref[...]).astype(BF16)
    o_ref[...] = _dot(cond, w_ref[...].astype(BF16)) + b_ref[...]


def _ada_mods(c, ada_w, ada_b):
    depth, _, d, d3 = ada_w.shape
    bsz = c.shape[0]
    n = depth * 2
    rows = 8
    c_pad = jnp.zeros((rows, d), F32).at[:bsz].set(c)
    tn = 1024
    out = pl.pallas_call(
        _ada_kernel,
        grid=(n, d3 // tn),
        in_specs=[
            pl.BlockSpec((rows, d), lambda s, j: (0, 0)),
            pl.BlockSpec((None, d, tn), lambda s, j: (s, 0, j)),
            pl.BlockSpec((None, 1, tn), lambda s, j: (s, 0, j)),
        ],
        out_specs=pl.BlockSpec((None, rows, tn), lambda s, j: (s, 0, j)),
        out_shape=jax.ShapeDtypeStruct((n, rows, d3), F32),
        compiler_params=_cparams("parallel", "parallel"),
        name="ada_mods",
    )(c_pad, ada_w.reshape(n, d, d3), ada_b.reshape(n, 1, d3))
    return out[:, :bsz].reshape(n, bsz, 3, d).transpose(0, 2, 1, 3).reshape(n, 3, bsz, 1, d)


SHIFT, SCALE, GATE = 0, 1, 2


def _mod_spec(d, sub, which, tiles_per_batch):
    return pl.BlockSpec((None, None, None, 1, d),
                        lambda i, *_: (sub, which, i // tiles_per_batch, 0, 0))


def _row_spec(d):
    return pl.BlockSpec((1, d), lambda *_: (0, 0))


def _modulate_kernel(x_ref, sc_ref, sh_ref, h_ref):
    h_ref[...] = (x_ref[...] * (1.0 + sc_ref[...]) + sh_ref[...]).astype(BF16)


def _modulate(x, mods, sub, seq, tm=512):
    t, d = x.shape
    tpb = seq // tm
    return pl.pallas_call(
        _modulate_kernel,
        grid=(t // tm,),
        in_specs=[pl.BlockSpec((tm, d), lambda i: (i, 0)),
                  _mod_spec(d, sub, SCALE, tpb), _mod_spec(d, sub, SHIFT, tpb)],
        out_specs=pl.BlockSpec((tm, d), lambda i: (i, 0)),
        out_shape=jax.ShapeDtypeStruct((t, d), BF16),
        compiler_params=_cparams("parallel"),
        name="modulate",
    )(x, mods, mods)


def _proj_kernel(segs, n_extra, h_ref, w_ref, *refs):
    extra = refs[:n_extra]
    outs = refs[n_extra:]
    j = pl.program_id(1)
    for (lo, hi, epi), o_ref in zip(segs, outs):
        def _store(o_ref=o_ref, epi=epi):
            acc = _dot(h_ref[...], w_ref[...])
            o_ref[...] = epi(acc, *[e[...] for e in extra]).astype(o_ref.dtype)
        if len(segs) == 1:
            _store()
        else:
            pl.when(jnp.logical_and(j >= lo, j < hi))(_store)


def _proj(h, w, segs, extra=(), extra_specs=(), tm=1024, tn=512, name="proj"):
    t, k = h.shape
    n = w.shape[1]
    bounds, lo = [], 0
    for n_cols, epi, _ in segs:
        assert n_cols % tn == 0
        bounds.append((lo, lo + n_cols // tn, epi))
        lo += n_cols // tn
    assert lo * tn == n
    out_specs = [
        pl.BlockSpec((tm, tn), functools.partial(
            lambda i, j, lo, hi: (i, jnp.clip(j - lo, 0, hi - lo - 1)), lo=lo_, hi=hi_))
        for lo_, hi_, _ in bounds]
    out_shape = [jax.ShapeDtypeStruct((t, n_cols), dt) for n_cols, _, dt in segs]
    return pl.pallas_call(
        functools.partial(_proj_kernel, bounds, len(extra)),
        grid=(t // tm, n // tn),
        in_specs=[pl.BlockSpec((tm, k), lambda i, j: (i, 0)),
                  pl.BlockSpec((k, tn), lambda i, j: (0, j))] + list(extra_specs),
        out_specs=out_specs,
        out_shape=out_shape,
        compiler_params=_cparams("parallel", "arbitrary"),
        name=name,
    )(h, w, *extra)


def _ln_mod_store(y, x_ref, gate_ref, lnw_ref, lnb_ref, sc_ref, sh_ref, xo_ref, ho_ref):
    z = DN_ALPHA * x_ref[...] + (1.0 + gate_ref[...]) * y
    mu = jnp.mean(z, axis=-1, keepdims=True)
    zc = z - mu
    var = jnp.mean(zc * zc, axis=-1, keepdims=True)
    xn = zc * lax.rsqrt(var + LN_EPS) * lnw_ref[...] + lnb_ref[...]
    xo_ref[...] = xn
    if ho_ref is not None:
        ho_ref[...] = (xn * (1.0 + sc_ref[...]) + sh_ref[...]).astype(BF16)


def _out_ln_kernel(n_a, has_next, *refs):
    a_refs = refs[:n_a]
    w_ref, x_ref, gate_ref, lnw_ref, lnb_ref = refs[n_a:n_a + 5]
    rest = refs[n_a + 5:]
    if has_next:
        sc_ref, sh_ref, xo_ref, ho_ref = rest
    else:
        (xo_ref,), sc_ref, sh_ref, ho_ref = rest, None, None, None
    a = a_refs[0][...]
    for r in a_refs[1:]:
        a = a + r[...]
    y = _dot(a.astype(BF16), w_ref[...])
    _ln_mod_store(y, x_ref, gate_ref, lnw_ref, lnb_ref, sc_ref, sh_ref, xo_ref, ho_ref)


def _out_ln(a_list, w, x, mods, sub, sub_next, ln_w, ln_b, seq, tm=256):
    t, d = x.shape
    k = w.shape[0]
    tpb = seq // tm
    has_next = sub_next is not None
    in_specs = [pl.BlockSpec((tm, k), lambda i: (i, 0)) for _ in a_list]
    in_specs += [pl.BlockSpec((k, d), lambda i: (0, 0), pipeline_mode=pl.Buffered(1)),
                 pl.BlockSpec((tm, d), lambda i: (i, 0)),
                 _mod_spec(d, sub, GATE, tpb), _row_spec(d), _row_spec(d)]
    args = list(a_list) + [w, x, mods, ln_w.reshape(1, d), ln_b.reshape(1, d)]
    out_specs = [pl.BlockSpec((tm, d), lambda i: (i, 0))]
    out_shape = [jax.ShapeDtypeStruct((t, d), F32)]
    if has_next:
        in_specs += [_mod_spec(d, sub_next, SCALE, tpb), _mod_spec(d, sub_next, SHIFT, tpb)]
        args += [mods, mods]
        out_specs.append(pl.BlockSpec((tm, d), lambda i: (i, 0)))
        out_shape.append(jax.ShapeDtypeStruct((t, d), BF16))
    res = pl.pallas_call(
        functools.partial(_out_ln_kernel, len(a_list), has_next),
        grid=(t // tm,),
        in_specs=in_specs, out_specs=out_specs, out_shape=out_shape,
        compiler_params=_cparams("parallel"),
        name="out_ln",
    )(*args)
    return (res[0], res[1]) if has_next else (res[0], None)


def _mlp_kernel(has_next, h_ref, w1_ref, w2_ref, x_ref, gate_ref, lnw_ref, lnb_ref, *rest):
    if has_next:
        sc_ref, sh_ref, xo_ref, ho_ref, acc_ref = rest
    else:
        (xo_ref, acc_ref), sc_ref, sh_ref, ho_ref = rest, None, None, None
    j = pl.program_id(1)

    @pl.when(j == 0)
    def _():
        acc_ref[...] = jnp.zeros_like(acc_ref)

    hid = _dot(h_ref[...], w1_ref[...])
    hid = jnp.square(jnp.maximum(hid, 0.0)).astype(BF16)
    acc_ref[...] += _dot(hid, w2_ref[...])

    @pl.when(j == pl.num_programs(1) - 1)
    def _():
        _ln_mod_store(acc_ref[...], x_ref, gate_ref, lnw_ref, lnb_ref, sc_ref, sh_ref, xo_ref, ho_ref)


def _mlp(h, w1, w2, x, mods, sub, sub_next, ln_w, ln_b, seq, tm=512, tf=1024):
    t, d = x.shape
    ff = w1.shape[1]
    tpb = seq // tm
    has_next = sub_next is not None
    in_specs = [pl.BlockSpec((tm, d), lambda i, j: (i, 0)),
                pl.BlockSpec((d, tf), lambda i, j: (0, j)),
                pl.BlockSpec((tf, d), lambda i, j: (j, 0)),
                pl.BlockSpec((tm, d), lambda i, j: (i, 0)),
                _mod_spec(d, sub, GATE, tpb), _row_spec(d), _row_spec(d)]
    args = [h, w1, w2, x, mods, ln_w.reshape(1, d), ln_b.reshape(1, d)]
    out_specs = [pl.BlockSpec((tm, d), lambda i, j: (i, 0))]
    out_shape = [jax.ShapeDtypeStruct((t, d), F32)]
    if has_next:
        in_specs += [_mod_spec(d, sub_next, SCALE, tpb), _mod_spec(d, sub_next, SHIFT, tpb)]
        args += [mods, mods]
        out_specs.append(pl.BlockSpec((tm, d), lambda i, j: (i, 0)))
        out_shape.append(jax.ShapeDtypeStruct((t, d), BF16))
    res = pl.pallas_call(
        functools.partial(_mlp_kernel, has_next),
        grid=(t // tm, ff // tf),
        in_specs=in_specs, out_specs=out_specs, out_shape=out_shape,
        scratch_shapes=[pltpu.VMEM((tm, d), F32)],
        compiler_params=_cparams("parallel", "arbitrary"),
        name="mlp",
    )(*args)
    return (res[0], res[1]) if has_next else (res[0], None)


HALO = 8
SCAN_SEGS = 8


def _lru_scan_kernel(ts, gy_ref, xb_ref, cw_ref, cb_ref, gw_ref, gb_ref, lam_ref, o_ref,
                     buf, a_scr, u_scr, h_scr):
    i = pl.program_id(1)
    width = xb_ref.shape[-1]
    bd = width // LRU_BLOCKS

    @pl.when(i == 0)
    def _():
        buf[0:HALO, :] = jnp.zeros((HALO, width), F32)
        h_scr[...] = jnp.zeros_like(h_scr)

    @pl.when(i > 0)
    def _():
        buf[0:HALO, :] = buf[ts:ts + HALO, :]

    buf[HALO:HALO + ts, :] = xb_ref[...]
    cw = cw_ref[...]
    taps = [buf[HALO - (CONV_WIDTH - 1) + k:HALO - (CONV_WIDTH - 1) + k + ts, :] * cw[k:k + 1, :]
            for k in range(CONV_WIDTH)]
    conv = taps[0]
    for tap in taps[1:]:
        conv = conv + tap
    xc = cb_ref[...] + conv
    xcb = xc.astype(BF16)
    gates = []
    for jg in range(2):
        cols = [_dot(xcb[:, n * bd:(n + 1) * bd], gw_ref[jg, n]) for n in range(LRU_BLOCKS)]
        gates.append(jnp.concatenate(cols, axis=1) + gb_ref[jg:jg + 1, :])
    r = 0.5 * (1.0 + jnp.tanh(0.5 * gates[0]))
    ig = 0.5 * (1.0 + jnp.tanh(0.5 * gates[1]))
    neg_lam = -lam_ref[...]
    softplus = jnp.maximum(neg_lam, 0.0) + jnp.log1p(jnp.exp(-jnp.abs(neg_lam)))
    log_a = -LRU_C * r * softplus
    a = jnp.exp(log_a)
    one_m_a2 = -jnp.tanh(log_a) * (1.0 + a * a)
    root = one_m_a2 * lax.rsqrt(jnp.maximum(one_m_a2, F32_TINY))
    u = root * (ig * xc)

    seg = ts // SCAN_SEGS
    n_chunks = width // LANES

    def chunk(c):
        return slice(c * LANES, (c + 1) * LANES)

    for s in range(SCAN_SEGS):
        for c in range(n_chunks):
            a_scr[c, pl.ds(s, seg, stride=SCAN_SEGS), :] = a[s * seg:(s + 1) * seg, chunk(c)]
            u_scr[c, pl.ds(s, seg, stride=SCAN_SEGS), :] = u[s * seg:(s + 1) * seg, chunk(c)]

    def step(t, carry):
        h_loc, a_cum = carry
        rows = pl.ds(pl.multiple_of(t * SCAN_SEGS, SCAN_SEGS), SCAN_SEGS)
        a_t = jnp.concatenate([a_scr[c, rows, :] for c in range(n_chunks)], axis=1)
        u_t = jnp.concatenate([u_scr[c, rows, :] for c in range(n_chunks)], axis=1)
        h_loc = a_t * h_loc + u_t
        a_cum = a_t * a_cum
        for c in range(n_chunks):
            u_scr[c, rows, :] = h_loc[:, chunk(c)]
            a_scr[c, rows, :] = a_cum[:, chunk(c)]
        return h_loc, a_cum

    zero = jnp.zeros((SCAN_SEGS, width), F32)
    h_end, a_end = lax.fori_loop(0, seg, step, (zero, zero + 1.0), unroll=2)
    h_in = h_scr[...]
    for s in range(SCAN_SEGS):
        rows = slice(s * seg, (s + 1) * seg)
        h_loc = jnp.concatenate([u_scr[c, pl.ds(s, seg, stride=SCAN_SEGS), :] for c in range(n_chunks)], axis=1)
        a_cum = jnp.concatenate([a_scr[c, pl.ds(s, seg, stride=SCAN_SEGS), :] for c in range(n_chunks)], axis=1)
        o_ref[rows, :] = (gy_ref[rows, :] * (h_loc + a_cum * h_in)).astype(BF16)
        h_in = h_end[s:s + 1, :] + a_end[s:s + 1, :] * h_in
    h_scr[...] = h_in


def _lru_scan(gy, xb, conv_w, conv_b, gate_w, gate_b, lam, ts=512):
    bsz, seq, width = xb.shape
    bd = width // LRU_BLOCKS
    tile = pl.BlockSpec((None, ts, width), lambda b, i: (b, i, 0))
    return pl.pallas_call(
        functools.partial(_lru_scan_kernel, ts),
        grid=(bsz, seq // ts),
        in_specs=[tile, tile,
                  pl.BlockSpec((CONV_WIDTH, width), lambda b, i: (0, 0)),
                  pl.BlockSpec((1, width), lambda b, i: (0, 0)),
                  pl.BlockSpec((2, LRU_BLOCKS, bd, bd), lambda b, i: (0, 0, 0, 0)),
                  pl.BlockSpec((2, width), lambda b, i: (0, 0)),
                  pl.BlockSpec((1, width), lambda b, i: (0, 0))],
        out_specs=tile,
        out_shape=jax.ShapeDtypeStruct((bsz, seq, width), BF16),
        scratch_shapes=[pltpu.VMEM((ts + HALO, width), F32), pltpu.VMEM((width // LANES, ts, LANES), F32),
                        pltpu.VMEM((width // LANES, ts, LANES), F32), pltpu.VMEM((1, width), F32)],
        compiler_params=_cparams("parallel", "arbitrary"),
        name="lru_scan",
    )(gy, xb, conv_w, conv_b.reshape(1, width), gate_w.astype(BF16), gate_b, lam.reshape(1, width))


def _lru_mixer(h, bsz, seq, w_in, conv_w, conv_b, gate_w, gate_b, lam):
    width = w_in.shape[1] // 2
    gy, xb = _proj(h, w_in.astype(BF16),
                   [(width, lambda acc: _gelu_tanh(acc), F32), (width, lambda acc: acc, F32)],
                   name="lru_proj")
    a = _lru_scan(gy.reshape(bsz, seq, width), xb.reshape(bsz, seq, width),
                  conv_w, conv_b, gate_w, gate_b, lam)
    return [a.reshape(bsz * seq, width)]


def _t5_bucket_np(dist):
    n = np.maximum(np.asarray(dist, np.int64), 0)
    max_exact = T5_BUCKETS // 2
    n_large = T5_BUCKETS - max_exact
    ratio = T5_MAX_DIST // max_exact
    thresholds = []
    for k in range(1, n_large):
        m = max_exact
        while m ** n_large < ratio ** k * max_exact ** n_large:
            m += 1
        thresholds.append(m)
    large = max_exact + sum((n >= th).astype(np.int64) for th in thresholds)
    return np.where(n < max_exact, n, large).astype(np.int32)


def _bias_table_kernel(sub_far, rb_ref, idx_ref, o_ref):
    hd = pl.program_id(0)
    idx = idx_ref[...]
    base = rb_ref[T5_BUCKETS - 1, hd] if sub_far else 0.0
    acc = jnp.full(idx.shape, NEG_MASK, F32)
    for b in range(T5_BUCKETS):
        acc = jnp.where(idx == b, rb_ref[b, hd] - base, acc)
    o_ref[...] = acc


def _bias_table(rel_bias, idx_np, sub_far, name):
    rows, cols = idx_np.shape
    tr = 128
    n_heads = rel_bias.shape[1]
    return pl.pallas_call(
        functools.partial(_bias_table_kernel, sub_far),
        grid=(n_heads, rows // tr),
        in_specs=[pl.BlockSpec(memory_space=pltpu.SMEM),
                  pl.BlockSpec((tr, cols), lambda hd, i: (i, 0))],
        out_specs=pl.BlockSpec((None, tr, cols), lambda hd, i: (hd, i, 0)),
        out_shape=jax.ShapeDtypeStruct((n_heads, rows, cols), F32),
        compiler_params=_cparams("parallel", "parallel"),
        name=name,
    )(rel_bias, jnp.asarray(idx_np))


def _compress_kernel(seq, x_ref, pe_ref, w1_ref, b1_ref, w2_ref, o_ref, xs, nat):
    xs[0:seq, :] = x_ref[...]
    xs[seq:CMP_ROWS, :] = jnp.zeros((CMP_ROWS - seq, LANES), F32)
    acc = jnp.zeros((N_CMP_SLOTS, LANES), F32)
    for l in range(CMP_BLOCK):
        rows = xs[pl.ds(l, N_CMP_SLOTS, stride=CMP_STRIDE), :] + pe_ref[l:l + 1, :]
        acc = acc + _dot(rows.astype(BF16), w1_ref[l])
    hid = _gelu_tanh(acc + b1_ref[...])
    nat[...] = _dot(hid.astype(BF16), w2_ref[...])
    for r in range(4):
        o_ref[r * LANES:(r + 1) * LANES, :] = nat[pl.ds(r, LANES, stride=4), :].astype(BF16)


def _compress(cmp_tok, pe, w1, b1, w2):
    bsz, seq, _ = cmp_tok.shape
    g = NSA_KV_GROUPS
    dh = NSA_HEAD_DIM
    return pl.pallas_call(
        functools.partial(_compress_kernel, seq),
        grid=(bsz, 2, g),
        in_specs=[pl.BlockSpec((None, seq, dh), lambda b, j, gi: (b, 0, j * g + gi)),
                  pl.BlockSpec((None, CMP_BLOCK, dh), lambda b, j, gi: (j, 0, 0)),
                  pl.BlockSpec((None, CMP_BLOCK, dh, dh), lambda b, j, gi: (j, 0, 0, 0)),
                  pl.BlockSpec((None, 1, dh), lambda b, j, gi: (j, 0, 0)),
                  pl.BlockSpec((None, dh, dh), lambda b, j, gi: (j, 0, 0))],
        out_specs=pl.BlockSpec((None, None, None, N_CMP_SLOTS, dh), lambda b, j, gi: (b, j, gi, 0, 0)),
        out_shape=jax.ShapeDtypeStruct((bsz, 2, g, N_CMP_SLOTS, dh), BF16),
        scratch_shapes=[pltpu.VMEM((CMP_ROWS, dh), F32), pltpu.VMEM((N_CMP_SLOTS, dh), F32)],
        compiler_params=_cparams("parallel", "parallel", "parallel"),
        name="nsa_compress",
    )(cmp_tok, pe, w1.astype(BF16), b1.reshape(2, 1, dh), w2.astype(BF16))


def _cmp_table_idx(tq):
    ql = np.arange(tq)[:, None, None]
    r = np.arange(4)[None, :, None]
    m = np.arange(LANES)[None, None, :]
    e = np.where(m <= (tq - CMP_BLOCK) // SEL_BLOCK, -m, LANES - m)
    dist = SEL_BLOCK * e + ql - CMP_STRIDE * r - (CMP_BLOCK - 1)
    return _t5_bucket_np(dist).reshape(tq, 4 * LANES)


def _cmp_attn_kernel(tq, q_ref, kc_ref, vc_ref, tb_ref, g_ref, o_ref, val_ref):
    i = pl.program_id(2)
    t0 = i * tq
    first_blk = i * (tq // SEL_BLOCK)
    kc = kc_ref[...]
    vc = vc_ref[...]
    gate = g_ref[...]
    ql = lax.broadcasted_iota(jnp.int32, (tq, N_CMP_SLOTS), 0)
    slot = lax.broadcasted_iota(jnp.int32, (tq, N_CMP_SLOTS), 1)
    phase = slot >> 7
    col = slot & (LANES - 1)
    dist = t0 + ql - SEL_BLOCK * col - CMP_STRIDE * phase - (CMP_BLOCK - 1)
    causal = jnp.where(dist >= 0, 0.0, NEG_MASK)
    row_ok = (t0 + lax.broadcasted_iota(jnp.int32, (tq, 1), 0)) >= CMP_BLOCK - 1
    p_sum = jnp.zeros((tq, N_CMP_SLOTS), F32)
    for hd in range(NSA_HPG):
        tb = tb_ref[hd]
        tb = jnp.concatenate(
            [pltpu.roll(tb[:, c * LANES:(c + 1) * LANES], first_blk, axis=1) for c in range(4)], axis=1)
        s = _dot_nt(q_ref[:, hd * LANES:(hd + 1) * LANES], kc) + tb + causal
        e = jnp.exp(s - jnp.max(s, axis=-1, keepdims=True))
        p = e * jnp.where(row_ok, 1.0 / jnp.sum(e, axis=-1, keepdims=True), 0.0)
        p_sum = p_sum + p
        o_ref[:, hd * LANES:(hd + 1) * LANES] = _dot(p.astype(BF16), vc) * gate[:, hd:hd + 1]
    ph = [p_sum[:, c * LANES:(c + 1) * LANES] for c in range(4)]
    lane = lax.broadcasted_iota(jnp.int32, (tq, LANES), 1)
    prev = jnp.where(lane == 0, 0.0, pltpu.roll(ph[3], 1, axis=1))
    score = (((prev + ph[0]) + ph[1]) + ph[2]) + ph[3]
    cur = (t0 + lax.broadcasted_iota(jnp.int32, (tq, LANES), 0)) >> 6
    forced = jnp.logical_or(lane == 0, jnp.logical_or(lane == cur, lane == cur - 1))
    val_ref[...] = jnp.where(forced, SEL_FORCE, jnp.where(lane <= cur, score, -SEL_FORCE))


def _topk_mask_kernel(n_top, val_ref, m_ref):
    val0 = val_ref[...]
    lane = lax.broadcasted_iota(jnp.int32, val0.shape, 1).astype(F32)
    val = val0
    sel = jnp.zeros(val0.shape, F32)
    for _ in range(n_top):
        mx = jnp.max(val, axis=-1, keepdims=True)
        first = jnp.min(jnp.where(val == mx, lane, float(LANES)), axis=-1, keepdims=True)
        pick = lane == first
        sel = jnp.where(pick, 1.0, sel)
        val = jnp.where(pick, -jnp.inf, val)
    m_ref[...] = jnp.where(jnp.logical_and(sel > 0.0, val0 >= 0.0), 0.0, NEG_MASK).astype(BF16)


def _topk_mask(val, n_top, tr=1024):
    rows = val.shape[0]
    return pl.pallas_call(
        functools.partial(_topk_mask_kernel, n_top),
        grid=(rows // tr,),
        in_specs=[pl.BlockSpec((tr, LANES), lambda i: (i, 0))],
        out_specs=pl.BlockSpec((tr, LANES), lambda i: (i, 0)),
        out_shape=jax.ShapeDtypeStruct((rows, LANES), BF16),
        compiler_params=_cparams("parallel"),
        name="nsa_topk_mask",
    )(val)


def _cmp_attn(q, kvc, table, gates, tq=256):
    bsz, seq, _ = q.shape
    g = NSA_KV_GROUPS
    qw = NSA_HPG * NSA_HEAD_DIM
    return pl.pallas_call(
        functools.partial(_cmp_attn_kernel, tq),
        grid=(bsz, g, seq // tq),
        in_specs=[pl.BlockSpec((None, tq, qw), lambda b, gi, i: (b, i, gi)),
                  pl.BlockSpec((None, None, None, N_CMP_SLOTS, NSA_HEAD_DIM), lambda b, gi, i: (b, 0, gi, 0, 0)),
                  pl.BlockSpec((None, None, None, N_CMP_SLOTS, NSA_HEAD_DIM), lambda b, gi, i: (b, 1, gi, 0, 0)),
                  pl.BlockSpec((NSA_HPG, tq, N_CMP_SLOTS), lambda b, gi, i: (gi, 0, 0)),
                  pl.BlockSpec((None, tq, LANES), lambda b, gi, i: (b, i, gi))],
        out_specs=[pl.BlockSpec((None, tq, qw), lambda b, gi, i: (b, i, gi)),
                   pl.BlockSpec((None, None, tq, LANES), lambda b, gi, i: (b, gi, i, 0))],
        out_shape=[jax.ShapeDtypeStruct((bsz, seq, g * qw), F32),
                   jax.ShapeDtypeStruct((bsz, g, seq, LANES), F32)],
        compiler_params=_cparams("parallel", "parallel", "parallel"),
        name="nsa_cmp_attn",
    )(q, kvc, kvc, table, gates)


SEL_TQ = 256
SEL_TK = 256
SEL_N_DELTA = -(-(T5_MAX_DIST - 1 + SEL_TK) // SEL_TQ)
SEL_N_NEAR = -(-SEL_N_DELTA // (SEL_TK // SEL_TQ))
SEL_STREAMS = 4


def _sel_table_idx():
    d = np.arange(SEL_N_DELTA)[:, None, None]
    ql = np.arange(SEL_TQ)[None, :, None]
    kl = np.arange(SEL_TK)[None, None, :]
    dist = ql - kl + SEL_TQ * d
    idx = np.where(dist >= 0, _t5_bucket_np(dist), MASK_BUCKET)
    return idx.reshape(SEL_N_DELTA * SEL_TQ, SEL_TK).astype(np.int32)


def _sel_attn_kernel(q_ref, m_ref, k_ref, v_ref, oh_ref, nb_ref, g_ref, o_ref, qa, m_scr, acc):
    tq, tk = SEL_TQ, SEL_TK
    ratio = tk // tq
    i = pl.program_id(2)
    kd = i // ratio
    sel_mask = m_ref[...]
    for hd in range(NSA_HPG):
        qa[hd * tq:(hd + 1) * tq, 0:LANES] = q_ref[:, hd * LANES:(hd + 1) * LANES]
        qa[hd * tq:(hd + 1) * tq, LANES:2 * LANES] = sel_mask
    m_scr[...] = jnp.full(m_scr.shape, -jnp.inf, F32)
    acc[...] = jnp.zeros_like(acc)
    n_kt = k_ref.shape[0] // tk
    ones = jnp.ones((tk, LANES), BF16)

    def update(st, tiles):
        scores, values = [], []
        for kt, valid, delta in tiles:
            k0 = pl.multiple_of(jnp.clip(kt, 0, n_kt - 1) * tk, tk)
            ka = jnp.concatenate([k_ref[pl.ds(k0, tk), :], oh_ref[pl.ds(k0, tk), :]], axis=1)
            s = _dot_nt(qa[...], ka)
            if valid is not None:
                s = s + jnp.where(valid, 0.0, NEG_MASK)
            if delta is not None:
                s = s + jnp.concatenate([nb_ref[hd, delta] for hd in range(NSA_HPG)], axis=0)
            scores.append(s)
            values.append(jnp.concatenate([v_ref[pl.ds(k0, tk), :], ones], axis=1))
        s_max = scores[0]
        for s in scores[1:]:
            s_max = jnp.maximum(s_max, s)
        m_prev = m_scr[st]
        m_new = jnp.maximum(m_prev, jnp.max(s_max, axis=-1, keepdims=True))
        alpha = jnp.exp(m_prev - m_new)
        m_rep = jnp.concatenate([m_new] * (tk // LANES), axis=1)
        pv = None
        for s, va in zip(scores, values):
            part = _dot(jnp.exp(s - m_rep).astype(BF16), va)
            pv = part if pv is None else pv + part
        acc[st] = jnp.concatenate([alpha, alpha], axis=1) * acc[st] + pv
        m_scr[st] = m_new

    n_far = jnp.maximum(kd - (SEL_N_NEAR - 1), 0)
    pair = 2 * SEL_STREAMS

    def far_pairs(it, carry):
        for st in range(SEL_STREAMS):
            kt = it * pair + 2 * st
            update(st, [(kt, None, None), (kt + 1, None, None)])
        return carry

    lax.fori_loop(0, n_far // pair, far_pairs, 0)
    base = (n_far // pair) * pair
    rem = n_far - base

    @pl.when(rem >= SEL_STREAMS)
    def _():
        for st in range(SEL_STREAMS):
            update(st, [(base + st, None, None)])

    base = base + jnp.where(rem >= SEL_STREAMS, SEL_STREAMS, 0)
    left = n_far - base
    slots = [(base + f, f < left, None) for f in range(SEL_STREAMS - 1)]
    for back in range(SEL_N_NEAR - 1, -1, -1):
        kt = kd - back
        slots.append((kt, kt >= 0, (i - kd * ratio) + back * ratio))
    for st in range(SEL_STREAMS):
        update(st, slots[st::SEL_STREAMS])

    m_all = m_scr[0]
    for st in range(1, SEL_STREAMS):
        m_all = jnp.maximum(m_all, m_scr[st])
    tot = jnp.zeros((NSA_HPG * tq, 2 * LANES), F32)
    for st in range(SEL_STREAMS):
        w = jnp.exp(m_scr[st] - m_all)
        tot = tot + jnp.concatenate([w, w], axis=1) * acc[st]
    gate = g_ref[...]
    out = tot[:, :LANES] / jnp.maximum(tot[:, LANES:], 1e-30)
    for hd in range(NSA_HPG):
        o_ref[:, hd * LANES:(hd + 1) * LANES] = (
            out[hd * tq:(hd + 1) * tq, :] * gate[:, NSA_HPG + hd:NSA_HPG + hd + 1])


def _sel_attn(q, sel_mask, kv_tok, onehot, table, gates):
    bsz, seq, _ = q.shape
    g = NSA_KV_GROUPS
    dh = NSA_HEAD_DIM
    qw = NSA_HPG * dh
    tq, tk = SEL_TQ, SEL_TK
    rows = NSA_HPG * tq
    return pl.pallas_call(
        _sel_attn_kernel,
        grid=(bsz, g, seq // tq),
        in_specs=[pl.BlockSpec((None, tq, qw), lambda b, gi, i: (b, i, gi)),
                  pl.BlockSpec((None, None, tq, LANES), lambda b, gi, i: (b, gi, i, 0)),
                  pl.BlockSpec((None, seq, dh), lambda b, gi, i: (b, 0, gi)),
                  pl.BlockSpec((None, seq, dh), lambda b, gi, i: (b, 0, g + gi)),
                  pl.BlockSpec((seq, LANES), lambda b, gi, i: (0, 0)),
                  pl.BlockSpec((NSA_HPG, SEL_N_DELTA, tq, tk), lambda b, gi, i: (gi, 0, 0, 0)),
                  pl.BlockSpec((None, tq, LANES), lambda b, gi, i: (b, i, gi))],
        out_specs=pl.BlockSpec((None, tq, qw), lambda b, gi, i: (b, i, gi)),
        out_shape=jax.ShapeDtypeStruct((bsz, seq, g * qw), F32),
        scratch_shapes=[pltpu.VMEM((rows, 2 * LANES), BF16), pltpu.VMEM((SEL_STREAMS, rows, LANES), F32),
                        pltpu.VMEM((SEL_STREAMS, rows, 2 * dh), F32)],
        compiler_params=_cparams("parallel", "parallel", "parallel"),
        name="nsa_sel_attn",
    )(q, sel_mask, kv_tok, kv_tok, onehot, table, gates)


WIN_TQ = 128
WIN_SUB = 4
WIN_KEYS = WINDOW + WIN_TQ


def _win_table_idx():
    ql = np.arange(WIN_TQ)[:, None]
    kl = np.arange(WIN_KEYS)[None, :]
    dist = ql + WINDOW - kl
    return np.where((dist >= 0) & (dist < WINDOW), _t5_bucket_np(dist), MASK_BUCKET).astype(np.int32)


def _win_attn_kernel(seq, q_ref, k_ref, v_ref, wb_ref, g_ref, o_ref, kpad, vpad, qs):
    tq = WIN_TQ
    i = pl.program_id(2)

    @pl.when(i == 0)
    def _():
        kpad[0:WINDOW, :] = jnp.zeros((WINDOW, LANES), BF16)
        vpad[0:WINDOW, :] = jnp.zeros((WINDOW, LANES), BF16)
        kpad[WINDOW:WINDOW + seq, :] = k_ref[...]
        vpad[WINDOW:WINDOW + seq, :] = v_ref[...]

    bias = wb_ref[...].reshape(NSA_HPG * tq, WIN_KEYS)
    for sub in range(WIN_SUB):
        rows = slice(sub * tq, (sub + 1) * tq)
        t0 = pl.multiple_of((i * WIN_SUB + sub) * tq, tq)
        for hd in range(NSA_HPG):
            qs[sub, hd * tq:(hd + 1) * tq, :] = q_ref[rows, hd * LANES:(hd + 1) * LANES]
        keys = kpad[pl.ds(t0, WIN_KEYS), :]
        vals = vpad[pl.ds(t0, WIN_KEYS), :]
        s = _dot_nt(qs[sub], keys) + bias
        kl = lax.broadcasted_iota(jnp.int32, s.shape, 1)
        s = jnp.where(kl >= WINDOW - t0, s, NEG_MASK)
        e = jnp.exp(s - jnp.max(s, axis=-1, keepdims=True))
        p = e * (1.0 / jnp.sum(e, axis=-1, keepdims=True))
        out = _dot(p.astype(BF16), vals)
        gate = g_ref[rows, :]
        for hd in range(NSA_HPG):
            o_ref[rows, hd * LANES:(hd + 1) * LANES] = (
                out[hd * tq:(hd + 1) * tq, :] * gate[:, 2 * NSA_HPG + hd:2 * NSA_HPG + hd + 1])


def _win_attn(q, kv_tok, table, gates):
    bsz, seq, _ = q.shape
    g = NSA_KV_GROUPS
    dh = NSA_HEAD_DIM
    qw = NSA_HPG * dh
    tq = WIN_TQ
    rows = tq * WIN_SUB
    return pl.pallas_call(
        functools.partial(_win_attn_kernel, seq),
        grid=(bsz, g, seq // rows),
        in_specs=[pl.BlockSpec((None, rows, qw), lambda b, gi, i: (b, i, gi)),
                  pl.BlockSpec((None, seq, dh), lambda b, gi, i: (b, 0, 2 * g + gi)),
                  pl.BlockSpec((None, seq, dh), lambda b, gi, i: (b, 0, 3 * g + gi)),
                  pl.BlockSpec((NSA_HPG, tq, WIN_KEYS), lambda b, gi, i: (gi, 0, 0)),
                  pl.BlockSpec((None, rows, LANES), lambda b, gi, i: (b, i, gi))],
        out_specs=pl.BlockSpec((None, rows, qw), lambda b, gi, i: (b, i, gi)),
        out_shape=jax.ShapeDtypeStruct((bsz, seq, g * qw), F32),
        scratch_shapes=[pltpu.VMEM((seq + WINDOW, dh), BF16), pltpu.VMEM((seq + WINDOW, dh), BF16),
                        pltpu.VMEM((WIN_SUB, NSA_HPG * tq, dh), BF16)],
        compiler_params=_cparams("parallel", "parallel", "arbitrary"),
        name="nsa_win_attn",
    )(q, kv_tok, kv_tok, table, gates)


def _nsa_mixer(h, bsz, seq, rel_bias, w_in, cmp_pe, cmp_w1, cmp_b1, cmp_w2):
    g, hpg, dh = NSA_KV_GROUPS, NSA_HPG, NSA_HEAD_DIM
    qw = NSA_HEADS * dh
    kvw = g * dh
    n_main = qw + 2 * N_BRANCH * kvw
    scale = dh ** -0.5
    q, cmp_tok, kv_tok = _proj(
        h, w_in[:, :n_main].astype(BF16),
        [(qw, lambda acc: acc * scale, BF16), (2 * kvw, lambda acc: acc, F32), (4 * kvw, lambda acc: acc, BF16)],
        name="nsa_proj")
    src = np.zeros((g * LANES,), np.int32)
    valid = np.zeros((g * LANES,), bool)
    for gi in range(g):
        for br in range(N_BRANCH):
            for hd in range(hpg):
                src[gi * LANES + br * hpg + hd] = n_main + br * NSA_HEADS + gi * hpg + hd
                valid[gi * LANES + br * hpg + hd] = True
    w_gate = jnp.where(jnp.asarray(valid)[None, :], w_in[:, src], 0.0).astype(BF16)
    (gates,) = _proj(h, w_gate, [(g * LANES, lambda acc: _sigmoid(acc), F32)], name="nsa_gate_proj")

    q = q.reshape(bsz, seq, qw)
    gates = gates.reshape(bsz, seq, g * LANES)
    kv_tok = kv_tok.reshape(bsz, seq, 4 * kvw)
    kvc = _compress(cmp_tok.reshape(bsz, seq, 2 * kvw), cmp_pe, cmp_w1, cmp_b1, cmp_w2)

    n_top = min(SEL_TOP_N, seq // SEL_BLOCK)
    cmp_tq = 256
    cmp_table = _bias_table(rel_bias, _cmp_table_idx(cmp_tq), False, "nsa_cmp_bias")
    o_c, sel_val = _cmp_attn(q, kvc, cmp_table, gates, cmp_tq)
    sel_mask = _topk_mask(sel_val.reshape(bsz * g * seq, LANES), n_top).reshape(bsz, g, seq, LANES)

    sel_table = _bias_table(rel_bias, _sel_table_idx(), True, "nsa_sel_bias")
    sel_table = sel_table.reshape(NSA_HEADS, SEL_N_DELTA, SEL_TQ, SEL_TK)
    onehot = (jnp.arange(seq, dtype=jnp.int32)[:, None] // SEL_BLOCK
              == jnp.arange(LANES, dtype=jnp.int32)[None, :]).astype(BF16)
    o_s = _sel_attn(q, sel_mask, kv_tok, onehot, sel_table, gates)

    win_table = _bias_table(rel_bias, _win_table_idx(), False, "nsa_win_bias")
    o_w = _win_attn(q, kv_tok, win_table, gates)
    t = bsz * seq
    return [o_c.reshape(t, qw), o_s.reshape(t, qw), o_w.reshape(t, qw)]


def _rotary_epi(scale, acc, cos, sin):
    half = cos.shape[-1]
    outs = []
    for hd in range(acc.shape[1] // (2 * half)):
        x1 = acc[:, 2 * hd * half:(2 * hd + 1) * half]
        x2 = acc[:, (2 * hd + 1) * half:(2 * hd + 2) * half]
        outs += [x1 * cos - x2 * sin, x1 * sin + x2 * cos]
    rot = jnp.concatenate(outs, axis=1)
    return rot * scale if scale != 1.0 else rot


def _ret_kernel(tc, q_ref, k_ref, v_ref, g_ref, dec_ref, xi_ref, zeta_ref, cd_ref, gw_ref, gb_ref,
                o_ref, state):
    i = pl.program_id(2)

    @pl.when(i == 0)
    def _():
        state[...] = jnp.zeros_like(state)

    for c in range(tc // RET_CHUNK):
        rows = slice(c * RET_CHUNK, (c + 1) * RET_CHUNK)
        qc = q_ref[rows, :]
        kf = k_ref[rows, :]
        vc = v_ref[rows, :]
        inner = _dot_nt(qc, kf.astype(BF16)) * dec_ref[...]
        st = state[...]
        out = _dot(inner.astype(BF16), vc) + _dot(qc, st.astype(BF16)) * xi_ref[...]
        kz = (kf * zeta_ref[...]).astype(BF16)
        state[...] = st * cd_ref[...] + _dot_tn(kz, vc)
        mu = jnp.mean(out, axis=-1, keepdims=True)
        oc = out - mu
        var = jnp.mean(oc * oc, axis=-1, keepdims=True)
        y = (oc * lax.rsqrt(var + GN_EPS)) * gw_ref[...] + gb_ref[...]
        o_ref[rows, :] = (g_ref[rows, :] * y).astype(BF16)


def _ret_core(q, k, v, g, gn_w, gn_b, tc=512):
    bsz, seq, qk_total = q.shape
    v_total = v.shape[-1]
    nh = RET_HEADS
    dk = qk_total // nh
    dv = v_total // nh
    cs = RET_CHUNK
    log_g = jnp.log1p(-jnp.exp2(-5.0 - jnp.arange(nh, dtype=F32)))
    idx = jnp.arange(cs, dtype=F32)
    diff = idx[:, None] - idx[None, :]
    decay = jnp.where(diff >= 0, jnp.exp(jnp.maximum(diff, 0.0) * log_g[:, None, None]), 0.0)
    xi = jnp.exp((idx + 1.0) * log_g[:, None])[:, :, None]
    zeta = jnp.exp((cs - 1.0 - idx) * log_g[:, None])[:, :, None]
    chunk_decay = jnp.exp(cs * log_g).reshape(nh, 1, 1)
    return pl.pallas_call(
        functools.partial(_ret_kernel, tc),
        grid=(bsz, nh, seq // tc),
        in_specs=[pl.BlockSpec((None, tc, dk), lambda b, hd, i: (b, i, hd)),
                  pl.BlockSpec((None, tc, dk), lambda b, hd, i: (b, i, hd)),
                  pl.BlockSpec((None, tc, dv), lambda b, hd, i: (b, i, hd)),
                  pl.BlockSpec((None, tc, dv), lambda b, hd, i: (b, i, hd)),
                  pl.BlockSpec((None, cs, cs), lambda b, hd, i: (hd, 0, 0)),
                  pl.BlockSpec((None, cs, 1), lambda b, hd, i: (hd, 0, 0)),
                  pl.BlockSpec((None, cs, 1), lambda b, hd, i: (hd, 0, 0)),
                  pl.BlockSpec((None, 1, 1), lambda b, hd, i: (hd, 0, 0)),
                  pl.BlockSpec((1, dv), lambda b, hd, i: (0, hd)),
                  pl.BlockSpec((1, dv), lambda b, hd, i: (0, hd))],
        out_specs=pl.BlockSpec((None, tc, dv), lambda b, hd, i: (b, i, hd)),
        out_shape=jax.ShapeDtypeStruct((bsz, seq, v_total), BF16),
        scratch_shapes=[pltpu.VMEM((dk, dv), F32)],
        compiler_params=_cparams("parallel", "parallel", "arbitrary"),
        name="ret_core",
    )(q, k, v, g, decay, xi, zeta, chunk_decay, gn_w.reshape(1, v_total), gn_b.reshape(1, v_total))


def _ret_mixer(h, bsz, seq, w_in, gn_w, gn_b, tm=1024):
    qk_total = w_in.shape[0]
    v_total = (w_in.shape[1] - 2 * qk_total) // 2
    dk = qk_total // RET_HEADS
    half = dk // 2
    pos = jnp.arange(seq, dtype=F32)
    inv = ROPE_BASE ** (-jnp.arange(half, dtype=F32) / half)
    ang = pos[:, None] * inv[None, :]
    cos, sin = jnp.cos(ang), jnp.sin(ang)
    tpb = seq // tm
    rope_spec = pl.BlockSpec((tm, half), lambda i, j: (i % tpb, 0))
    q, k, v, g = _proj(
        h, w_in.astype(BF16),
        [(qk_total, functools.partial(_rotary_epi, 1.0), BF16),
         (qk_total, functools.partial(_rotary_epi, dk ** -0.5), F32),
         (v_total, lambda acc, cos, sin: acc, BF16),
         (v_total, lambda acc, cos, sin: _silu(acc), F32)],
        extra=(cos, sin), extra_specs=(rope_spec, rope_spec), tm=tm, name="ret_proj")
    a = _ret_core(q.reshape(bsz, seq, qk_total), k.reshape(bsz, seq, qk_total),
                  v.reshape(bsz, seq, v_total), g.reshape(bsz, seq, v_total), gn_w, gn_b)
    return [a.reshape(bsz * seq, v_total)]


def kernel(x, c, rel_bias, ada_w, ada_b, ln_w, ln_b, mlp_w1, mlp_w2, lru_w_in, lru_conv_w, lru_conv_b,
           lru_gate_w, lru_gate_b, lru_lambda, lru_w_out, nsa_w_in, nsa_cmp_pe, nsa_cmp_w1, nsa_cmp_b1,
           nsa_cmp_w2, nsa_w_out, ret_w_in, ret_gn_w, ret_gn_b, ret_w_out):
    bsz, seq, d = x.shape
    depth = ada_w.shape[0]
    mods = _ada_mods(c, ada_w, ada_b)
    xf = x.reshape(bsz * seq, d)
    h = _modulate(xf, mods, 0, seq)
    for layer in range(depth):
        mixer, inst = layer % N_MIXERS, layer // N_MIXERS
        sub = 2 * layer
        if mixer == 0:
            a_list = _lru_mixer(h, bsz, seq, lru_w_in[inst], lru_conv_w[inst], lru_conv_b[inst],
                                lru_gate_w[inst], lru_gate_b[inst], lru_lambda[inst])
            w_out = lru_w_out[inst]
        elif mixer == 1:
            a_list = _nsa_mixer(h, bsz, seq, rel_bias, nsa_w_in[inst], nsa_cmp_pe[inst], nsa_cmp_w1[inst],
                                nsa_cmp_b1[inst], nsa_cmp_w2[inst])
            w_out = nsa_w_out[inst]
        else:
            a_list = _ret_mixer(h, bsz, seq, ret_w_in[inst], ret_gn_w[inst], ret_gn_b[inst])
            w_out = ret_w_out[inst]
        xf, h = _out_ln(a_list, w_out.astype(BF16), xf, mods, sub, sub + 1, ln_w[layer, 0], ln_b[layer, 0], seq)
        sub_next = sub + 2 if layer + 1 < depth else None
        xf, h = _mlp(h, mlp_w1[layer].astype(BF16), mlp_w2[layer].astype(BF16), xf, mods, sub + 1, sub_next,
                     ln_w[layer, 1], ln_b[layer, 1], seq)
    return xf.reshape(bsz, seq, d)
```

```python
import functools
import math

import numpy as np
import jax
import jax.numpy as jnp
from jax import lax
from jax.experimental import pallas as pl
from jax.experimental.pallas import tpu as pltpu

F32 = jnp.float32
BF16 = jnp.bfloat16

DN_DEPTH = 4
N_MIXERS = 3
DN_ALPHA = (2 * DN_DEPTH) ** 0.25
LN_EPS = 1e-5
GN_EPS = 1e-5
LRU_BLOCKS = 8
CONV_WIDTH = 4
LRU_C = 8.0
NSA_HEADS = 16
NSA_HEAD_DIM = 128
NSA_KV_GROUPS = 4
NSA_HPG = NSA_HEADS // NSA_KV_GROUPS
N_BRANCH = 3
CMP_BLOCK = 32
CMP_STRIDE = 16
SEL_BLOCK = 64
SEL_TOP_N = 16
SEL_FORCE = 1e6
WINDOW = 512
NEG_BIG = 1e30
T5_BUCKETS = 32
T5_MAX_DIST = 1024
RET_HEADS = 8
RET_CHUNK = 128
ROPE_BASE = 10000.0

LANES = 128
VMEM_LIMIT_BYTES = 56 * 2 ** 20

NEG_MASK = -(2.0 ** 100)
MASK_BUCKET = T5_BUCKETS
F32_TINY = float(np.finfo(np.float32).tiny)

N_CMP_SLOTS = 4 * LANES
CMP_ROWS = CMP_STRIDE * N_CMP_SLOTS + 4 * CMP_STRIDE


def _cparams(*sem):
    return pltpu.CompilerParams(dimension_semantics=sem, vmem_limit_bytes=VMEM_LIMIT_BYTES)


def _sigmoid(x):
    return 1.0 / (1.0 + jnp.exp(-x))


def _silu(x):
    return x * _sigmoid(x)


def _gelu_tanh(x):
    return 0.5 * x * (1.0 + jnp.tanh(math.sqrt(2.0 / math.pi) * (x + 0.044715 * (x * x * x))))


def _dot(a, b):
    return jnp.dot(a, b, preferred_element_type=F32)


def _dot_nt(a, b):
    return lax.dot_general(a, b, (((1,), (1,)), ((), ())), preferred_element_type=F32)


def _dot_tn(a, b):
    return lax.dot_general(a, b, (((0,), (0,)), ((), ())), preferred_element_type=F32)


def _ada_kernel(c_ref, w_ref, b_ref, o_ref):
    cond = _silu(c_ref[...]).astype(BF16)
    o_ref[...] = _dot(cond, w_ref[...].astype(BF16)) + b_ref[...]


def _ada_mods(c, ada_w, ada_b):
    depth, _, d, d3 = ada_w.shape
    bsz = c.shape[0]
    n = depth * 2
    rows = 8
    c_pad = jnp.zeros((rows, d), F32).at[:bsz].set(c)
    tn = 1024
    out = pl.pallas_call(
        _ada_kernel,
        grid=(n, d3 // tn),
        in_specs=[
            pl.BlockSpec((rows, d), lambda s, j: (0, 0)),
            pl.BlockSpec((None, d, tn), lambda s, j: (s, 0, j)),
            pl.BlockSpec((None, 1, tn), lambda s, j: (s, 0, j)),
        ],
        out_specs=pl.BlockSpec((None, rows, tn), lambda s, j: (s, 0, j)),
        out_shape=jax.ShapeDtypeStruct((n, rows, d3), F32),
        compiler_params=_cparams("parallel", "parallel"),
        name="ada_mods",
    )(c_pad, ada_w.reshape(n, d, d3), ada_b.reshape(n, 1, d3))
    return out[:, :bsz].reshape(n, bsz, 3, d).transpose(0, 2, 1, 3).reshape(n, 3, bsz, 1, d)


SHIFT, SCALE, GATE = 0, 1, 2


def _mod_spec(d, sub, which, tiles_per_batch):
    return pl.BlockSpec((None, None, None, 1, d),
                        lambda i, *_: (sub, which, i // tiles_per_batch, 0, 0))


def _row_spec(d):
    return pl.BlockSpec((1, d), lambda *_: (0, 0))


def _modulate_kernel(x_ref, sc_ref, sh_ref, h_ref):
    h_ref[...] = (x_ref[...] * (1.0 + sc_ref[...]) + sh_ref[...]).astype(BF16)


def _modulate(x, mods, sub, seq, tm=512):
    t, d = x.shape
    tpb = seq // tm
    return pl.pallas_call(
        _modulate_kernel,
        grid=(t // tm,),
        in_specs=[pl.BlockSpec((tm, d), lambda i: (i, 0)),
                  _mod_spec(d, sub, SCALE, tpb), _mod_spec(d, sub, SHIFT, tpb)],
        out_specs=pl.BlockSpec((tm, d), lambda i: (i, 0)),
        out_shape=jax.ShapeDtypeStruct((t, d), BF16),
        compiler_params=_cparams("parallel"),
        name="modulate",
    )(x, mods, mods)


def _proj_kernel(segs, n_extra, h_ref, w_ref, *refs):
    extra = refs[:n_extra]
    outs = refs[n_extra:]
    j = pl.program_id(1)
    for (lo, hi, epi), o_ref in zip(segs, outs):
        def _store(o_ref=o_ref, epi=epi):
            acc = _dot(h_ref[...], w_ref[...])
            o_ref[...] = epi(acc, *[e[...] for e in extra]).astype(o_ref.dtype)
        if len(segs) == 1:
            _store()
        else:
            pl.when(jnp.logical_and(j >= lo, j < hi))(_store)


def _proj(h, w, segs, extra=(), extra_specs=(), tm=1024, tn=512, name="proj"):
    t, k = h.shape
    n = w.shape[1]
    bounds, lo = [], 0
    for n_cols, epi, _ in segs:
        assert n_cols % tn == 0
        bounds.append((lo, lo + n_cols // tn, epi))
        lo += n_cols // tn
    assert lo * tn == n
    out_specs = [
        pl.BlockSpec((tm, tn), functools.partial(
            lambda i, j, lo, hi: (i, jnp.clip(j - lo, 0, hi - lo - 1)), lo=lo_, hi=hi_))
        for lo_, hi_, _ in bounds]
    out_shape = [jax.ShapeDtypeStruct((t, n_cols), dt) for n_cols, _, dt in segs]
    return pl.pallas_call(
        functools.partial(_proj_kernel, bounds, len(extra)),
        grid=(t // tm, n // tn),
        in_specs=[pl.BlockSpec((tm, k), lambda i, j: (i, 0)),
                  pl.BlockSpec((k, tn), lambda i, j: (0, j))] + list(extra_specs),
        out_specs=out_specs,
        out_shape=out_shape,
        compiler_params=_cparams("parallel", "arbitrary"),
        name=name,
    )(h, w, *extra)


def _ln_mod_store(y, x_ref, gate_ref, lnw_ref, lnb_ref, sc_ref, sh_ref, xo_ref, ho_ref):
    z = DN_ALPHA * x_ref[...] + (1.0 + gate_ref[...]) * y
    mu = jnp.mean(z, axis=-1, keepdims=True)
    zc = z - mu
    var = jnp.mean(zc * zc, axis=-1, keepdims=True)
    xn = zc * lax.rsqrt(var + LN_EPS) * lnw_ref[...] + lnb_ref[...]
    xo_ref[...] = xn
    if ho_ref is not None:
        ho_ref[...] = (xn * (1.0 + sc_ref[...]) + sh_ref[...]).astype(BF16)


def _out_ln_kernel(n_a, has_next, *refs):
    a_refs = refs[:n_a]
    w_ref, x_ref, gate_ref, lnw_ref, lnb_ref = refs[n_a:n_a + 5]
    rest = refs[n_a + 5:]
    if has_next:
        sc_ref, sh_ref, xo_ref, ho_ref = rest
    else:
        (xo_ref,), sc_ref, sh_ref, ho_ref = rest, None, None, None
    a = a_refs[0][...]
    for r in a_refs[1:]:
        a = a.astype(F32) + r[...].astype(F32)
    y = _dot(a.astype(BF16), w_ref[...])
    _ln_mod_store(y, x_ref, gate_ref, lnw_ref, lnb_ref, sc_ref, sh_ref, xo_ref, ho_ref)


def _out_ln(a_list, w, x, mods, sub, sub_next, ln_w, ln_b, seq, tm=512):
    t, d = x.shape
    k = w.shape[0]
    tpb = seq // tm
    has_next = sub_next is not None
    in_specs = [pl.BlockSpec((tm, k), lambda i: (i, 0)) for _ in a_list]
    in_specs += [pl.BlockSpec((k, d), lambda i: (0, 0), pipeline_mode=pl.Buffered(1)),
                 pl.BlockSpec((tm, d), lambda i: (i, 0)),
                 _mod_spec(d, sub, GATE, tpb), _row_spec(d), _row_spec(d)]
    args = list(a_list) + [w, x, mods, ln_w.reshape(1, d), ln_b.reshape(1, d)]
    out_specs = [pl.BlockSpec((tm, d), lambda i: (i, 0))]
    out_shape = [jax.ShapeDtypeStruct((t, d), F32)]
    if has_next:
        in_specs += [_mod_spec(d, sub_next, SCALE, tpb), _mod_spec(d, sub_next, SHIFT, tpb)]
        args += [mods, mods]
        out_specs.append(pl.BlockSpec((tm, d), lambda i: (i, 0)))
        out_shape.append(jax.ShapeDtypeStruct((t, d), BF16))
    res = pl.pallas_call(
        functools.partial(_out_ln_kernel, len(a_list), has_next),
        grid=(t // tm,),
        in_specs=in_specs, out_specs=out_specs, out_shape=out_shape,
        compiler_params=_cparams("parallel"),
        name="out_ln",
    )(*args)
    return (res[0], res[1]) if has_next else (res[0], None)


def _mlp_kernel(has_next, h_ref, w1_ref, w2_ref, x_ref, gate_ref, lnw_ref, lnb_ref, *rest):
    if has_next:
        sc_ref, sh_ref, xo_ref, ho_ref, acc_ref = rest
    else:
        (xo_ref, acc_ref), sc_ref, sh_ref, ho_ref = rest, None, None, None
    j = pl.program_id(1)

    @pl.when(j == 0)
    def _():
        acc_ref[...] = jnp.zeros_like(acc_ref)

    hid = _dot(h_ref[...], w1_ref[...])
    hid = jnp.square(jnp.maximum(hid, 0.0)).astype(BF16)
    acc_ref[...] += _dot(hid, w2_ref[...])

    @pl.when(j == pl.num_programs(1) - 1)
    def _():
        _ln_mod_store(acc_ref[...], x_ref, gate_ref, lnw_ref, lnb_ref, sc_ref, sh_ref, xo_ref, ho_ref)


def _mlp(h, w1, w2, x, mods, sub, sub_next, ln_w, ln_b, seq, tm=512, tf=1024):
    t, d = x.shape
    ff = w1.shape[1]
    tpb = seq // tm
    has_next = sub_next is not None
    in_specs = [pl.BlockSpec((tm, d), lambda i, j: (i, 0)),
                pl.BlockSpec((d, tf), lambda i, j: (0, j)),
                pl.BlockSpec((tf, d), lambda i, j: (j, 0)),
                pl.BlockSpec((tm, d), lambda i, j: (i, 0)),
                _mod_spec(d, sub, GATE, tpb), _row_spec(d), _row_spec(d)]
    args = [h, w1, w2, x, mods, ln_w.reshape(1, d), ln_b.reshape(1, d)]
    out_specs = [pl.BlockSpec((tm, d), lambda i, j: (i, 0))]
    out_shape = [jax.ShapeDtypeStruct((t, d), F32)]
    if has_next:
        in_specs += [_mod_spec(d, sub_next, SCALE, tpb), _mod_spec(d, sub_next, SHIFT, tpb)]
        args += [mods, mods]
        out_specs.append(pl.BlockSpec((tm, d), lambda i, j: (i, 0)))
        out_shape.append(jax.ShapeDtypeStruct((t, d), BF16))
    res = pl.pallas_call(
        functools.partial(_mlp_kernel, has_next),
        grid=(t // tm, ff // tf),
        in_specs=in_specs, out_specs=out_specs, out_shape=out_shape,
        scratch_shapes=[pltpu.VMEM((tm, d), F32)],
        compiler_params=_cparams("parallel", "arbitrary"),
        name="mlp",
    )(*args)
    return (res[0], res[1]) if has_next else (res[0], None)


HALO = 8


def _lru_scan_kernel(ts, gy_ref, xb_ref, cw_ref, cb_ref, gw_ref, gb_ref, lam_ref, o_ref,
                     buf, a_scr, u_scr, h_scr):
    i = pl.program_id(1)
    width = xb_ref.shape[-1]
    bd = width // LRU_BLOCKS

    @pl.when(i == 0)
    def _():
        buf[0:HALO, :] = jnp.zeros((HALO, width), F32)
        h_scr[...] = jnp.zeros_like(h_scr)

    @pl.when(i > 0)
    def _():
        buf[0:HALO, :] = buf[ts:ts + HALO, :]

    buf[HALO:HALO + ts, :] = xb_ref[...]
    cw = cw_ref[...]
    taps = [buf[HALO - (CONV_WIDTH - 1) + k:HALO - (CONV_WIDTH - 1) + k + ts, :] * cw[k:k + 1, :]
            for k in range(CONV_WIDTH)]
    conv = taps[0]
    for tap in taps[1:]:
        conv = conv + tap
    xc = cb_ref[...] + conv
    xcb = xc.astype(BF16)
    gates = []
    for jg in range(2):
        cols = [_dot(xcb[:, n * bd:(n + 1) * bd], gw_ref[jg, n]) for n in range(LRU_BLOCKS)]
        gates.append(jnp.concatenate(cols, axis=1) + gb_ref[jg:jg + 1, :])
    r = 0.5 * (1.0 + jnp.tanh(0.5 * gates[0]))
    ig = 0.5 * (1.0 + jnp.tanh(0.5 * gates[1]))
    neg_lam = -lam_ref[...]
    softplus = jnp.maximum(neg_lam, 0.0) + jnp.log1p(jnp.exp(-jnp.abs(neg_lam)))
    log_a = -LRU_C * r * softplus
    a = jnp.exp(log_a)
    a_scr[...] = a
    one_m_a2 = -jnp.tanh(log_a) * (1.0 + a * a)
    root = one_m_a2 * lax.rsqrt(jnp.maximum(one_m_a2, F32_TINY))
    u_scr[...] = root * (ig * xc)

    def step(t, h):
        h = a_scr[pl.ds(t, 1), :] * h + u_scr[pl.ds(t, 1), :]
        u_scr[pl.ds(t, 1), :] = h
        return h

    h_scr[...] = lax.fori_loop(0, ts, step, h_scr[...], unroll=8)
    o_ref[...] = (gy_ref[...] * u_scr[...]).astype(BF16)


def _lru_scan(gy, xb, conv_w, conv_b, gate_w, gate_b, lam, ts=512):
    bsz, seq, width = xb.shape
    bd = width // LRU_BLOCKS
    tile = pl.BlockSpec((None, ts, width), lambda b, i: (b, i, 0))
    return pl.pallas_call(
        functools.partial(_lru_scan_kernel, ts),
        grid=(bsz, seq // ts),
        in_specs=[tile, tile,
                  pl.BlockSpec((CONV_WIDTH, width), lambda b, i: (0, 0)),
                  pl.BlockSpec((1, width), lambda b, i: (0, 0)),
                  pl.BlockSpec((2, LRU_BLOCKS, bd, bd), lambda b, i: (0, 0, 0, 0)),
                  pl.BlockSpec((2, width), lambda b, i: (0, 0)),
                  pl.BlockSpec((1, width), lambda b, i: (0, 0))],
        out_specs=tile,
        out_shape=jax.ShapeDtypeStruct((bsz, seq, width), BF16),
        scratch_shapes=[pltpu.VMEM((ts + HALO, width), F32), pltpu.VMEM((ts, width), F32),
                        pltpu.VMEM((ts, width), F32), pltpu.VMEM((1, width), F32)],
        compiler_params=_cparams("parallel", "arbitrary"),
        name="lru_scan",
    )(gy, xb, conv_w, conv_b.reshape(1, width), gate_w.astype(BF16), gate_b, lam.reshape(1, width))


def _lru_mixer(h, bsz, seq, w_in, conv_w, conv_b, gate_w, gate_b, lam):
    width = w_in.shape[1] // 2
    gy, xb = _proj(h, w_in.astype(BF16),
                   [(width, lambda acc: _gelu_tanh(acc), BF16), (width, lambda acc: acc, F32)],
                   tn=1024, name="lru_proj")
    a = _lru_scan(gy.reshape(bsz, seq, width), xb.reshape(bsz, seq, width),
                  conv_w, conv_b, gate_w, gate_b, lam)
    return [a.reshape(bsz * seq, width)]


def _t5_bucket_np(dist):
    n = np.maximum(np.asarray(dist, np.int64), 0)
    max_exact = T5_BUCKETS // 2
    n_large = T5_BUCKETS - max_exact
    ratio = T5_MAX_DIST // max_exact
    thresholds = []
    for k in range(1, n_large):
        m = max_exact
        while m ** n_large < ratio ** k * max_exact ** n_large:
            m += 1
        thresholds.append(m)
    large = max_exact + sum((n >= th).astype(np.int64) for th in thresholds)
    return np.where(n < max_exact, n, large).astype(np.int32)


def _bias_table_kernel(sub_far, rb_ref, idx_ref, o_ref):
    hd = pl.program_id(0)
    idx = idx_ref[...]
    base = rb_ref[T5_BUCKETS - 1, hd] if sub_far else 0.0
    acc = jnp.full(idx.shape, NEG_MASK, F32)
    for b in range(T5_BUCKETS):
        acc = jnp.where(idx == b, rb_ref[b, hd] - base, acc)
    o_ref[...] = acc


def _bias_table(rel_bias, idx_np, sub_far, name):
    rows, cols = idx_np.shape
    tr = 128
    n_heads = rel_bias.shape[1]
    return pl.pallas_call(
        functools.partial(_bias_table_kernel, sub_far),
        grid=(n_heads, rows // tr),
        in_specs=[pl.BlockSpec(memory_space=pltpu.SMEM),
                  pl.BlockSpec((tr, cols), lambda hd, i: (i, 0))],
        out_specs=pl.BlockSpec((None, tr, cols), lambda hd, i: (hd, i, 0)),
        out_shape=jax.ShapeDtypeStruct((n_heads, rows, cols), F32),
        compiler_params=_cparams("parallel", "parallel"),
        name=name,
    )(rel_bias, jnp.asarray(idx_np))


def _compress_kernel(seq, x_ref, pe_ref, w1_ref, b1_ref, w2_ref, o_ref, xs, nat):
    xs[0:seq, :] = x_ref[...]
    xs[seq:CMP_ROWS, :] = jnp.zeros((CMP_ROWS - seq, LANES), F32)
    acc = jnp.zeros((N_CMP_SLOTS, LANES), F32)
    for l in range(CMP_BLOCK):
        rows = xs[pl.ds(l, N_CMP_SLOTS, stride=CMP_STRIDE), :] + pe_ref[l:l + 1, :]
        acc = acc + _dot(rows.astype(BF16), w1_ref[l])
    hid = _gelu_tanh(acc + b1_ref[...])
    nat[...] = _dot(hid.astype(BF16), w2_ref[...])
    for r in range(4):
        o_ref[r * LANES:(r + 1) * LANES, :] = nat[pl.ds(r, LANES, stride=4), :].astype(BF16)


def _compress(cmp_tok, pe, w1, b1, w2):
    bsz, seq, _ = cmp_tok.shape
    g = NSA_KV_GROUPS
    dh = NSA_HEAD_DIM
    return pl.pallas_call(
        functools.partial(_compress_kernel, seq),
        grid=(bsz, 2, g),
        in_specs=[pl.BlockSpec((None, seq, dh), lambda b, j, gi: (b, 0, j * g + gi)),
                  pl.BlockSpec((None, CMP_BLOCK, dh), lambda b, j, gi: (j, 0, 0)),
                  pl.BlockSpec((None, CMP_BLOCK, dh, dh), lambda b, j, gi: (j, 0, 0, 0)),
                  pl.BlockSpec((None, 1, dh), lambda b, j, gi: (j, 0, 0)),
                  pl.BlockSpec((None, dh, dh), lambda b, j, gi: (j, 0, 0))],
        out_specs=pl.BlockSpec((None, None, None, N_CMP_SLOTS, dh), lambda b, j, gi: (b, j, gi, 0, 0)),
        out_shape=jax.ShapeDtypeStruct((bsz, 2, g, N_CMP_SLOTS, dh), BF16),
        scratch_shapes=[pltpu.VMEM((CMP_ROWS, dh), F32), pltpu.VMEM((N_CMP_SLOTS, dh), F32)],
        compiler_params=_cparams("parallel", "parallel", "parallel"),
        name="nsa_compress",
    )(cmp_tok, pe, w1.astype(BF16), b1.reshape(2, 1, dh), w2.astype(BF16))


def _cmp_table_idx(tq):
    ql = np.arange(tq)[:, None, None]
    r = np.arange(4)[None, :, None]
    m = np.arange(LANES)[None, None, :]
    e = np.where(m <= (tq - CMP_BLOCK) // SEL_BLOCK, -m, LANES - m)
    dist = SEL_BLOCK * e + ql - CMP_STRIDE * r - (CMP_BLOCK - 1)
    return _t5_bucket_np(dist).reshape(tq, 4 * LANES)


CMP_SUB = 2


def _cmp_attn_kernel(tq, q_ref, kc_ref, vc_ref, tb_ref, g_ref, o_ref, val_ref):
    i = pl.program_id(2)
    kc = kc_ref[...]
    vc = vc_ref[...]
    ql = lax.broadcasted_iota(jnp.int32, (tq, N_CMP_SLOTS), 0)
    slot = lax.broadcasted_iota(jnp.int32, (tq, N_CMP_SLOTS), 1)
    rel = ql - SEL_BLOCK * (slot & (LANES - 1)) - CMP_STRIDE * (slot >> 7) - (CMP_BLOCK - 1)
    lane = lax.broadcasted_iota(jnp.int32, (tq, LANES), 1)
    for sub in range(CMP_SUB):
        rows = slice(sub * tq, (sub + 1) * tq)
        t0 = (i * CMP_SUB + sub) * tq
        first_blk = (i * CMP_SUB + sub) * (tq // SEL_BLOCK)
        gate = g_ref[rows, :]
        causal = jnp.where(t0 + rel >= 0, 0.0, NEG_MASK)
        row_ok = (t0 + lax.broadcasted_iota(jnp.int32, (tq, 1), 0)) >= CMP_BLOCK - 1
        p_sum = jnp.zeros((tq, N_CMP_SLOTS), F32)
        for hd in range(NSA_HPG):
            tb = tb_ref[hd]
            tb = jnp.concatenate(
                [pltpu.roll(tb[:, c * LANES:(c + 1) * LANES], first_blk, axis=1) for c in range(4)], axis=1)
            s = _dot_nt(q_ref[rows, hd * LANES:(hd + 1) * LANES], kc) + tb + causal
            e = jnp.exp(s - jnp.max(s, axis=-1, keepdims=True))
            p = e * jnp.where(row_ok, 1.0 / jnp.sum(e, axis=-1, keepdims=True), 0.0)
            p_sum = p_sum + p
            o_ref[rows, hd * LANES:(hd + 1) * LANES] = (
                _dot(p.astype(BF16), vc) * gate[:, hd:hd + 1]).astype(o_ref.dtype)
        ph = [p_sum[:, c * LANES:(c + 1) * LANES] for c in range(4)]
        prev = jnp.where(lane == 0, 0.0, pltpu.roll(ph[3], 1, axis=1))
        score = (((prev + ph[0]) + ph[1]) + ph[2]) + ph[3]
        cur = (t0 + lax.broadcasted_iota(jnp.int32, (tq, LANES), 0)) >> 6
        forced = jnp.logical_or(lane == 0, jnp.logical_or(lane == cur, lane == cur - 1))
        val_ref[rows, :] = jnp.where(forced, SEL_FORCE, jnp.where(lane <= cur, score, -SEL_FORCE))


def _topk_mask_kernel(n_top, val_ref, m_ref):
    val0 = val_ref[...]
    lane = lax.broadcasted_iota(jnp.int32, val0.shape, 1).astype(F32)
    val = val0
    sel = jnp.zeros(val0.shape, F32)
    for _ in range(n_top):
        mx = jnp.max(val, axis=-1, keepdims=True)
        first = jnp.min(jnp.where(val == mx, lane, float(LANES)), axis=-1, keepdims=True)
        pick = lane == first
        sel = jnp.where(pick, 1.0, sel)
        val = jnp.where(pick, -jnp.inf, val)
    m_ref[...] = jnp.where(jnp.logical_and(sel > 0.0, val0 >= 0.0), 0.0, NEG_MASK).astype(BF16)


def _topk_mask(val, n_top, tr=1024):
    rows = val.shape[0]
    return pl.pallas_call(
        functools.partial(_topk_mask_kernel, n_top),
        grid=(rows // tr,),
        in_specs=[pl.BlockSpec((tr, LANES), lambda i: (i, 0))],
        out_specs=pl.BlockSpec((tr, LANES), lambda i: (i, 0)),
        out_shape=jax.ShapeDtypeStruct((rows, LANES), BF16),
        compiler_params=_cparams("parallel"),
        name="nsa_topk_mask",
    )(val)


def _cmp_attn(q, kvc, table, gates, tq=256):
    bsz, seq, _ = q.shape
    g = NSA_KV_GROUPS
    qw = NSA_HPG * NSA_HEAD_DIM
    rows = tq * CMP_SUB
    return pl.pallas_call(
        functools.partial(_cmp_attn_kernel, tq),
        grid=(bsz, g, seq // rows),
        in_specs=[pl.BlockSpec((None, rows, qw), lambda b, gi, i: (b, i, gi)),
                  pl.BlockSpec((None, None, None, N_CMP_SLOTS, NSA_HEAD_DIM), lambda b, gi, i: (b, 0, gi, 0, 0)),
                  pl.BlockSpec((None, None, None, N_CMP_SLOTS, NSA_HEAD_DIM), lambda b, gi, i: (b, 1, gi, 0, 0)),
                  pl.BlockSpec((NSA_HPG, tq, N_CMP_SLOTS), lambda b, gi, i: (gi, 0, 0)),
                  pl.BlockSpec((None, rows, LANES), lambda b, gi, i: (b, i, gi))],
        out_specs=[pl.BlockSpec((None, rows, qw), lambda b, gi, i: (b, i, gi)),
                   pl.BlockSpec((None, None, rows, LANES), lambda b, gi, i: (b, gi, i, 0))],
        out_shape=[jax.ShapeDtypeStruct((bsz, seq, g * qw), BF16),
                   jax.ShapeDtypeStruct((bsz, g, seq, LANES), F32)],
        compiler_params=_cparams("parallel", "parallel", "parallel"),
        name="nsa_cmp_attn",
    )(q, kvc, kvc, table, gates)


SEL_TQ = 256
SEL_TK = 256
SEL_N_DELTA = -(-(T5_MAX_DIST - 1 + SEL_TK) // SEL_TQ)
SEL_N_NEAR = -(-SEL_N_DELTA // (SEL_TK // SEL_TQ))
SEL_STREAMS = 4


def _sel_table_idx():
    d = np.arange(SEL_N_DELTA)[:, None, None]
    ql = np.arange(SEL_TQ)[None, :, None]
    kl = np.arange(SEL_TK)[None, None, :]
    dist = ql - kl + SEL_TQ * d
    idx = np.where(dist >= 0, _t5_bucket_np(dist), MASK_BUCKET)
    return idx.reshape(SEL_N_DELTA * SEL_TQ, SEL_TK).astype(np.int32)


def _sel_attn_kernel(q_ref, m_ref, k_ref, v_ref, oh_ref, nb_ref, g_ref, o_ref, qa, m_scr, acc):
    tq, tk = SEL_TQ, SEL_TK
    ratio = tk // tq
    i = pl.program_id(2)
    kd = i // ratio
    sel_mask = m_ref[...]
    for hd in range(NSA_HPG):
        qa[hd * tq:(hd + 1) * tq, 0:LANES] = q_ref[:, hd * LANES:(hd + 1) * LANES]
        qa[hd * tq:(hd + 1) * tq, LANES:2 * LANES] = sel_mask
    m_scr[...] = jnp.full(m_scr.shape, -jnp.inf, F32)
    acc[...] = jnp.zeros_like(acc)
    n_kt = k_ref.shape[0] // tk
    ones = jnp.ones((tk, LANES), BF16)

    def update(st, tiles):
        scores, values = [], []
        for kt, valid, delta in tiles:
            k0 = pl.multiple_of(jnp.clip(kt, 0, n_kt - 1) * tk, tk)
            ka = jnp.concatenate([k_ref[pl.ds(k0, tk), :], oh_ref[pl.ds(k0, tk), :]], axis=1)
            s = _dot_nt(qa[...], ka)
            if valid is not None:
                s = s + jnp.where(valid, 0.0, NEG_MASK)
            if delta is not None:
                s = s + jnp.concatenate([nb_ref[hd, delta] for hd in range(NSA_HPG)], axis=0)
            scores.append(s)
            values.append(jnp.concatenate([v_ref[pl.ds(k0, tk), :], ones], axis=1))
        s_max = scores[0]
        for s in scores[1:]:
            s_max = jnp.maximum(s_max, s)
        m_prev = m_scr[st]
        m_new = jnp.maximum(m_prev, jnp.max(s_max, axis=-1, keepdims=True))
        alpha = jnp.exp(m_prev - m_new)
        m_rep = jnp.concatenate([m_new] * (tk // LANES), axis=1)
        pv = None
        for s, va in zip(scores, values):
            part = _dot(jnp.exp(s - m_rep).astype(BF16), va)
            pv = part if pv is None else pv + part
        acc[st] = jnp.concatenate([alpha, alpha], axis=1) * acc[st] + pv
        m_scr[st] = m_new

    n_far = jnp.maximum(kd - (SEL_N_NEAR - 1), 0)
    pair = 2 * SEL_STREAMS

    def far_pairs(it, carry):
        for st in range(SEL_STREAMS):
            kt = it * pair + 2 * st
            update(st, [(kt, None, None), (kt + 1, None, None)])
        return carry

    lax.fori_loop(0, n_far // pair, far_pairs, 0)
    base = (n_far // pair) * pair
    rem = n_far - base

    @pl.when(rem >= SEL_STREAMS)
    def _():
        for st in range(SEL_STREAMS):
            update(st, [(base + st, None, None)])

    base = base + jnp.where(rem >= SEL_STREAMS, SEL_STREAMS, 0)
    left = n_far - base
    slots = [(base + f, f < left, None) for f in range(SEL_STREAMS - 1)]
    for back in range(SEL_N_NEAR - 1, -1, -1):
        kt = kd - back
        slots.append((kt, kt >= 0, (i - kd * ratio) + back * ratio))
    for st in range(SEL_STREAMS):
        update(st, slots[st::SEL_STREAMS])

    m_all = m_scr[0]
    for st in range(1, SEL_STREAMS):
        m_all = jnp.maximum(m_all, m_scr[st])
    tot = jnp.zeros((NSA_HPG * tq, 2 * LANES), F32)
    for st in range(SEL_STREAMS):
        w = jnp.exp(m_scr[st] - m_all)
        tot = tot + jnp.concatenate([w, w], axis=1) * acc[st]
    gate = g_ref[...]
    out = tot[:, :LANES] / jnp.maximum(tot[:, LANES:], 1e-30)
    for hd in range(NSA_HPG):
        o_ref[:, hd * LANES:(hd + 1) * LANES] = (
            out[hd * tq:(hd + 1) * tq, :] * gate[:, NSA_HPG + hd:NSA_HPG + hd + 1]).astype(o_ref.dtype)


def _sel_attn(q, sel_mask, kv_tok, onehot, table, gates):
    bsz, seq, _ = q.shape
    g = NSA_KV_GROUPS
    dh = NSA_HEAD_DIM
    qw = NSA_HPG * dh
    tq, tk = SEL_TQ, SEL_TK
    rows = NSA_HPG * tq
    return pl.pallas_call(
        _sel_attn_kernel,
        grid=(bsz, g, seq // tq),
        in_specs=[pl.BlockSpec((None, tq, qw), lambda b, gi, i: (b, i, gi)),
                  pl.BlockSpec((None, None, tq, LANES), lambda b, gi, i: (b, gi, i, 0)),
                  pl.BlockSpec((None, seq, dh), lambda b, gi, i: (b, 0, gi)),
                  pl.BlockSpec((None, seq, dh), lambda b, gi, i: (b, 0, g + gi)),
                  pl.BlockSpec((seq, LANES), lambda b, gi, i: (0, 0)),
                  pl.BlockSpec((NSA_HPG, SEL_N_DELTA, tq, tk), lambda b, gi, i: (gi, 0, 0, 0)),
                  pl.BlockSpec((None, tq, LANES), lambda b, gi, i: (b, i, gi))],
        out_specs=pl.BlockSpec((None, tq, qw), lambda b, gi, i: (b, i, gi)),
        out_shape=jax.ShapeDtypeStruct((bsz, seq, g * qw), BF16),
        scratch_shapes=[pltpu.VMEM((rows, 2 * LANES), BF16), pltpu.VMEM((SEL_STREAMS, rows, LANES), F32),
                        pltpu.VMEM((SEL_STREAMS, rows, 2 * dh), F32)],
        compiler_params=_cparams("parallel", "parallel", "parallel"),
        name="nsa_sel_attn",
    )(q, sel_mask, kv_tok, kv_tok, onehot, table, gates)


WIN_TQ = 128
WIN_SUB = 4
WIN_KEYS = WINDOW + WIN_TQ


def _win_table_idx():
    ql = np.arange(WIN_TQ)[:, None]
    kl = np.arange(WIN_KEYS)[None, :]
    dist = ql + WINDOW - kl
    return np.where((dist >= 0) & (dist < WINDOW), _t5_bucket_np(dist), MASK_BUCKET).astype(np.int32)


def _win_attn_kernel(seq, q_ref, k_ref, v_ref, wb_ref, g_ref, o_ref, kpad, vpad, qs):
    tq = WIN_TQ
    i = pl.program_id(2)

    @pl.when(i == 0)
    def _():
        kpad[0:WINDOW, :] = jnp.zeros((WINDOW, LANES), BF16)
        vpad[0:WINDOW, :] = jnp.zeros((WINDOW, LANES), BF16)
        kpad[WINDOW:WINDOW + seq, :] = k_ref[...]
        vpad[WINDOW:WINDOW + seq, :] = v_ref[...]

    bias = wb_ref[...].reshape(NSA_HPG * tq, WIN_KEYS)
    for sub in range(WIN_SUB):
        rows = slice(sub * tq, (sub + 1) * tq)
        t0 = pl.multiple_of((i * WIN_SUB + sub) * tq, tq)
        for hd in range(NSA_HPG):
            qs[sub, hd * tq:(hd + 1) * tq, :] = q_ref[rows, hd * LANES:(hd + 1) * LANES]
        keys = kpad[pl.ds(t0, WIN_KEYS), :]
        vals = vpad[pl.ds(t0, WIN_KEYS), :]
        s = _dot_nt(qs[sub], keys) + bias
        kl = lax.broadcasted_iota(jnp.int32, s.shape, 1)
        s = jnp.where(kl >= WINDOW - t0, s, NEG_MASK)
        e = jnp.exp(s - jnp.max(s, axis=-1, keepdims=True))
        p = e * (1.0 / jnp.sum(e, axis=-1, keepdims=True))
        out = _dot(p.astype(BF16), vals)
        gate = g_ref[rows, :]
        for hd in range(NSA_HPG):
            o_ref[rows, hd * LANES:(hd + 1) * LANES] = (
                out[hd * tq:(hd + 1) * tq, :] * gate[:, 2 * NSA_HPG + hd:2 * NSA_HPG + hd + 1]).astype(o_ref.dtype)


def _win_attn(q, kv_tok, table, gates):
    bsz, seq, _ = q.shape
    g = NSA_KV_GROUPS
    dh = NSA_HEAD_DIM
    qw = NSA_HPG * dh
    tq = WIN_TQ
    rows = tq * WIN_SUB
    return pl.pallas_call(
        functools.partial(_win_attn_kernel, seq),
        grid=(bsz, g, seq // rows),
        in_specs=[pl.BlockSpec((None, rows, qw), lambda b, gi, i: (b, i, gi)),
                  pl.BlockSpec((None, seq, dh), lambda b, gi, i: (b, 0, 2 * g + gi)),
                  pl.BlockSpec((None, seq, dh), lambda b, gi, i: (b, 0, 3 * g + gi)),
                  pl.BlockSpec((NSA_HPG, tq, WIN_KEYS), lambda b, gi, i: (gi, 0, 0)),
                  pl.BlockSpec((None, rows, LANES), lambda b, gi, i: (b, i, gi))],
        out_specs=pl.BlockSpec((None, rows, qw), lambda b, gi, i: (b, i, gi)),
        out_shape=jax.ShapeDtypeStruct((bsz, seq, g * qw), BF16),
        scratch_shapes=[pltpu.VMEM((seq + WINDOW, dh), BF16), pltpu.VMEM((seq + WINDOW, dh), BF16),
                        pltpu.VMEM((WIN_SUB, NSA_HPG * tq, dh), BF16)],
        compiler_params=_cparams("parallel", "parallel", "arbitrary"),
        name="nsa_win_attn",
    )(q, kv_tok, kv_tok, table, gates)


def _nsa_mixer(h, bsz, seq, rel_bias, w_in, cmp_pe, cmp_w1, cmp_b1, cmp_w2):
    g, hpg, dh = NSA_KV_GROUPS, NSA_HPG, NSA_HEAD_DIM
    qw = NSA_HEADS * dh
    kvw = g * dh
    n_main = qw + 2 * N_BRANCH * kvw
    scale = dh ** -0.5
    q, cmp_tok, kv_tok = _proj(
        h, w_in[:, :n_main].astype(BF16),
        [(qw, lambda acc: acc * scale, BF16), (2 * kvw, lambda acc: acc, F32), (4 * kvw, lambda acc: acc, BF16)],
        tn=1024, name="nsa_proj")
    src = np.zeros((g * LANES,), np.int32)
    valid = np.zeros((g * LANES,), bool)
    for gi in range(g):
        for br in range(N_BRANCH):
            for hd in range(hpg):
                src[gi * LANES + br * hpg + hd] = n_main + br * NSA_HEADS + gi * hpg + hd
                valid[gi * LANES + br * hpg + hd] = True
    w_gate = jnp.where(jnp.asarray(valid)[None, :], w_in[:, src], 0.0).astype(BF16)
    (gates,) = _proj(h, w_gate, [(g * LANES, lambda acc: _sigmoid(acc), F32)], name="nsa_gate_proj")

    q = q.reshape(bsz, seq, qw)
    gates = gates.reshape(bsz, seq, g * LANES)
    kv_tok = kv_tok.reshape(bsz, seq, 4 * kvw)
    kvc = _compress(cmp_tok.reshape(bsz, seq, 2 * kvw), cmp_pe, cmp_w1, cmp_b1, cmp_w2)

    n_top = min(SEL_TOP_N, seq // SEL_BLOCK)
    cmp_tq = 256
    cmp_table = _bias_table(rel_bias, _cmp_table_idx(cmp_tq), False, "nsa_cmp_bias")
    o_c, sel_val = _cmp_attn(q, kvc, cmp_table, gates, cmp_tq)
    sel_mask = _topk_mask(sel_val.reshape(bsz * g * seq, LANES), n_top).reshape(bsz, g, seq, LANES)

    sel_table = _bias_table(rel_bias, _sel_table_idx(), True, "nsa_sel_bias")
    sel_table = sel_table.reshape(NSA_HEADS, SEL_N_DELTA, SEL_TQ, SEL_TK)
    onehot = (jnp.arange(seq, dtype=jnp.int32)[:, None] // SEL_BLOCK
              == jnp.arange(LANES, dtype=jnp.int32)[None, :]).astype(BF16)
    o_s = _sel_attn(q, sel_mask, kv_tok, onehot, sel_table, gates)

    win_table = _bias_table(rel_bias, _win_table_idx(), False, "nsa_win_bias")
    o_w = _win_attn(q, kv_tok, win_table, gates)
    t = bsz * seq
    return [o_c.reshape(t, qw), o_s.reshape(t, qw), o_w.reshape(t, qw)]


def _rotary_epi(scale, acc, cos, sin):
    half = cos.shape[-1]
    outs = []
    for hd in range(acc.shape[1] // (2 * half)):
        x1 = acc[:, 2 * hd * half:(2 * hd + 1) * half]
        x2 = acc[:, (2 * hd + 1) * half:(2 * hd + 2) * half]
        outs += [x1 * cos - x2 * sin, x1 * sin + x2 * cos]
    rot = jnp.concatenate(outs, axis=1)
    return rot * scale if scale != 1.0 else rot


RET_HPS = 2


def _ret_kernel(tc, q_ref, k_ref, v_ref, g_ref, dec_ref, xi_ref, zeta_ref, cd_ref, gw_ref, gb_ref,
                o_ref, state):
    i = pl.program_id(2)
    dk = q_ref.shape[-1] // RET_HPS
    dv = v_ref.shape[-1] // RET_HPS

    @pl.when(i == 0)
    def _():
        state[...] = jnp.zeros_like(state)

    for c in range(tc // RET_CHUNK):
        rows = slice(c * RET_CHUNK, (c + 1) * RET_CHUNK)
        for hh in range(RET_HPS):
            qk_cols = slice(hh * dk, (hh + 1) * dk)
            v_cols = slice(hh * dv, (hh + 1) * dv)
            qc = q_ref[rows, qk_cols]
            kc = k_ref[rows, qk_cols]
            vc = v_ref[rows, v_cols]
            inner = _dot_nt(qc, kc) * dec_ref[hh]
            st = state[hh]
            out = _dot(inner.astype(BF16), vc) + _dot(qc, st.astype(BF16)) * xi_ref[hh]
            kz = (kc.astype(F32) * zeta_ref[hh]).astype(BF16)
            state[hh] = st * cd_ref[hh] + _dot_tn(kz, vc)
            mu = jnp.mean(out, axis=-1, keepdims=True)
            oc = out - mu
            var = jnp.mean(oc * oc, axis=-1, keepdims=True)
            y = (oc * lax.rsqrt(var + GN_EPS)) * gw_ref[:, v_cols] + gb_ref[:, v_cols]
            o_ref[rows, v_cols] = (g_ref[rows, v_cols] * y).astype(BF16)


def _ret_core(q, k, v, g, gn_w, gn_b, tc=512):
    bsz, seq, qk_total = q.shape
    v_total = v.shape[-1]
    nh = RET_HEADS
    dk = qk_total // nh
    dv = v_total // nh
    cs = RET_CHUNK
    hps = RET_HPS
    log_g = jnp.log1p(-jnp.exp2(-5.0 - jnp.arange(nh, dtype=F32)))
    idx = jnp.arange(cs, dtype=F32)
    diff = idx[:, None] - idx[None, :]
    decay = jnp.where(diff >= 0, jnp.exp(jnp.maximum(diff, 0.0) * log_g[:, None, None]), 0.0)
    xi = jnp.exp((idx + 1.0) * log_g[:, None])[:, :, None]
    zeta = jnp.exp((cs - 1.0 - idx) * log_g[:, None])[:, :, None]
    chunk_decay = jnp.exp(cs * log_g).reshape(nh, 1, 1)
    return pl.pallas_call(
        functools.partial(_ret_kernel, tc),
        grid=(bsz, nh // hps, seq // tc),
        in_specs=[pl.BlockSpec((None, tc, hps * dk), lambda b, hp, i: (b, i, hp)),
                  pl.BlockSpec((None, tc, hps * dk), lambda b, hp, i: (b, i, hp)),
                  pl.BlockSpec((None, tc, hps * dv), lambda b, hp, i: (b, i, hp)),
                  pl.BlockSpec((None, tc, hps * dv), lambda b, hp, i: (b, i, hp)),
                  pl.BlockSpec((hps, cs, cs), lambda b, hp, i: (hp, 0, 0)),
                  pl.BlockSpec((hps, cs, 1), lambda b, hp, i: (hp, 0, 0)),
                  pl.BlockSpec((hps, cs, 1), lambda b, hp, i: (hp, 0, 0)),
                  pl.BlockSpec((hps, 1, 1), lambda b, hp, i: (hp, 0, 0)),
                  pl.BlockSpec((1, hps * dv), lambda b, hp, i: (0, hp)),
                  pl.BlockSpec((1, hps * dv), lambda b, hp, i: (0, hp))],
        out_specs=pl.BlockSpec((None, tc, hps * dv), lambda b, hp, i: (b, i, hp)),
        out_shape=jax.ShapeDtypeStruct((bsz, seq, v_total), BF16),
        scratch_shapes=[pltpu.VMEM((hps, dk, dv), F32)],
        compiler_params=_cparams("parallel", "parallel", "arbitrary"),
        name="ret_core",
    )(q, k, v, g, decay, xi, zeta, chunk_decay, gn_w.reshape(1, v_total), gn_b.reshape(1, v_total))


def _ret_mixer(h, bsz, seq, w_in, gn_w, gn_b, tm=1024):
    qk_total = w_in.shape[0]
    v_total = (w_in.shape[1] - 2 * qk_total) // 2
    dk = qk_total // RET_HEADS
    half = dk // 2
    pos = jnp.arange(seq, dtype=F32)
    inv = ROPE_BASE ** (-jnp.arange(half, dtype=F32) / half)
    ang = pos[:, None] * inv[None, :]
    cos, sin = jnp.cos(ang), jnp.sin(ang)
    tpb = seq // tm
    rope_spec = pl.BlockSpec((tm, half), lambda i, j: (i % tpb, 0))
    q, k, v, g = _proj(
        h, w_in.astype(BF16),
        [(qk_total, functools.partial(_rotary_epi, 1.0), BF16),
         (qk_total, functools.partial(_rotary_epi, dk ** -0.5), BF16),
         (v_total, lambda acc, cos, sin: acc, BF16),
         (v_total, lambda acc, cos, sin: _silu(acc), BF16)],
        extra=(cos, sin), extra_specs=(rope_spec, rope_spec), tm=tm, tn=1024, name="ret_proj")
    a = _ret_core(q.reshape(bsz, seq, qk_total), k.reshape(bsz, seq, qk_total),
                  v.reshape(bsz, seq, v_total), g.reshape(bsz, seq, v_total), gn_w, gn_b)
    return [a.reshape(bsz * seq, v_total)]


def kernel(x, c, rel_bias, ada_w, ada_b, ln_w, ln_b, mlp_w1, mlp_w2, lru_w_in, lru_conv_w, lru_conv_b,
           lru_gate_w, lru_gate_b, lru_lambda, lru_w_out, nsa_w_in, nsa_cmp_pe, nsa_cmp_w1, nsa_cmp_b1,
           nsa_cmp_w2, nsa_w_out, ret_w_in, ret_gn_w, ret_gn_b, ret_w_out):
    bsz, seq, d = x.shape
    depth = ada_w.shape[0]
    mods = _ada_mods(c, ada_w, ada_b)
    xf = x.reshape(bsz * seq, d)
    h = _modulate(xf, mods, 0, seq)
    for layer in range(depth):
        mixer, inst = layer % N_MIXERS, layer // N_MIXERS
        sub = 2 * layer
        if mixer == 0:
            a_list = _lru_mixer(h, bsz, seq, lru_w_in[inst], lru_conv_w[inst], lru_conv_b[inst],
                                lru_gate_w[inst], lru_gate_b[inst], lru_lambda[inst])
            w_out = lru_w_out[inst]
        elif mixer == 1:
            a_list = _nsa_mixer(h, bsz, seq, rel_bias, nsa_w_in[inst], nsa_cmp_pe[inst], nsa_cmp_w1[inst],
                                nsa_cmp_b1[inst], nsa_cmp_w2[inst])
            w_out = nsa_w_out[inst]
        else:
            a_list = _ret_mixer(h, bsz, seq, ret_w_in[inst], ret_gn_w[inst], ret_gn_b[inst])
            w_out = ret_w_out[inst]
        xf, h = _out_ln(a_list, w_out.astype(BF16), xf, mods, sub, sub + 1, ln_w[layer, 0], ln_b[layer, 0], seq)
        sub_next = sub + 2 if layer + 1 < depth else None
        xf, h = _mlp(h, mlp_w1[layer].astype(BF16), mlp_w2[layer].astype(BF16), xf, mods, sub + 1, sub_next,
                     ln_w[layer, 1], ln_b[layer, 1], seq)
    return xf.reshape(bsz, seq, d)
```

```python
import functools
import math

import numpy as np
import jax
import jax.numpy as jnp
from jax import lax
from jax.experimental import pallas as pl
from jax.experimental.pallas import tpu as pltpu

F32 = jnp.float32
BF16 = jnp.bfloat16

DN_DEPTH = 4
N_MIXERS = 3
DN_ALPHA = (2 * DN_DEPTH) ** 0.25
LN_EPS = 1e-5
GN_EPS = 1e-5
LRU_BLOCKS = 8
CONV_WIDTH = 4
LRU_C = 8.0
NSA_HEADS = 16
NSA_HEAD_DIM = 128
NSA_KV_GROUPS = 4
NSA_HPG = NSA_HEADS // NSA_KV_GROUPS
N_BRANCH = 3
CMP_BLOCK = 32
CMP_STRIDE = 16
SEL_BLOCK = 64
SEL_TOP_N = 16
SEL_FORCE = 1e6
WINDOW = 512
NEG_BIG = 1e30
T5_BUCKETS = 32
T5_MAX_DIST = 1024
RET_HEADS = 8
RET_CHUNK = 128
ROPE_BASE = 10000.0

LANES = 128
VMEM_LIMIT_BYTES = 56 * 2 ** 20

NEG_MASK = -(2.0 ** 100)
MASK_BUCKET = T5_BUCKETS
F32_TINY = float(np.finfo(np.float32).tiny)
LOG2E = math.log2(math.e)

N_CMP_SLOTS = 4 * LANES
CMP_ROWS = CMP_STRIDE * N_CMP_SLOTS + 4 * CMP_STRIDE


def _cparams(*sem):
    return pltpu.CompilerParams(dimension_semantics=sem, vmem_limit_bytes=VMEM_LIMIT_BYTES)


def _sigmoid(x):
    return 1.0 / (1.0 + jnp.exp(-x))


def _silu(x):
    return x * _sigmoid(x)


def _gelu_tanh(x):
    return 0.5 * x * (1.0 + jnp.tanh(math.sqrt(2.0 / math.pi) * (x + 0.044715 * (x * x * x))))


def _dot(a, b):
    return jnp.dot(a, b, preferred_element_type=F32)


def _dot_nt(a, b):
    return lax.dot_general(a, b, (((1,), (1,)), ((), ())), preferred_element_type=F32)


def _dot_tn(a, b):
    return lax.dot_general(a, b, (((0,), (0,)), ((), ())), preferred_element_type=F32)


def _ada_kernel(c_ref, w_ref, b_ref, o_ref):
    cond = _silu(c_ref[...]).astype(BF16)
    o_ref[...] = _dot(cond, w_ref[...].astype(BF16)) + b_ref[...]


def _ada_mods(c, ada_w, ada_b):
    depth, _, d, d3 = ada_w.shape
    bsz = c.shape[0]
    n = depth * 2
    rows = 8
    c_pad = jnp.zeros((rows, d), F32).at[:bsz].set(c)
    tn = 1024
    out = pl.pallas_call(
        _ada_kernel,
        grid=(n, d3 // tn),
        in_specs=[
            pl.BlockSpec((rows, d), lambda s, j: (0, 0)),
            pl.BlockSpec((None, d, tn), lambda s, j: (s, 0, j)),
            pl.BlockSpec((None, 1, tn), lambda s, j: (s, 0, j)),
        ],
        out_specs=pl.BlockSpec((None, rows, tn), lambda s, j: (s, 0, j)),
        out_shape=jax.ShapeDtypeStruct((n, rows, d3), F32),
        compiler_params=_cparams("parallel", "parallel"),
        name="ada_mods",
    )(c_pad, ada_w.reshape(n, d, d3), ada_b.reshape(n, 1, d3))
    return out[:, :bsz].reshape(n, bsz, 3, d).transpose(0, 2, 1, 3).reshape(n, 3, bsz, 1, d)


SHIFT, SCALE, GATE = 0, 1, 2


def _mod_spec(d, sub, which, tiles_per_batch):
    return pl.BlockSpec((None, None, None, 1, d),
                        lambda i, *_: (sub, which, i // tiles_per_batch, 0, 0))


def _row_spec(d):
    return pl.BlockSpec((1, d), lambda *_: (0, 0))


def _modulate_kernel(x_ref, sc_ref, sh_ref, h_ref):
    h_ref[...] = (x_ref[...] * (1.0 + sc_ref[...]) + sh_ref[...]).astype(BF16)


def _modulate(x, mods, sub, seq, tm=512):
    t, d = x.shape
    tpb = seq // tm
    return pl.pallas_call(
        _modulate_kernel,
        grid=(t // tm,),
        in_specs=[pl.BlockSpec((tm, d), lambda i: (i, 0)),
                  _mod_spec(d, sub, SCALE, tpb), _mod_spec(d, sub, SHIFT, tpb)],
        out_specs=pl.BlockSpec((tm, d), lambda i: (i, 0)),
        out_shape=jax.ShapeDtypeStruct((t, d), BF16),
        compiler_params=_cparams("parallel"),
        name="modulate",
    )(x, mods, mods)


def _proj_kernel(segs, n_extra, h_ref, w_ref, *refs):
    extra = refs[:n_extra]
    outs = refs[n_extra:]
    j = pl.program_id(1)
    for (lo, hi, epi), o_ref in zip(segs, outs):
        def _store(o_ref=o_ref, epi=epi):
            acc = _dot(h_ref[...], w_ref[...])
            o_ref[...] = epi(acc, *[e[...] for e in extra]).astype(o_ref.dtype)
        if len(segs) == 1:
            _store()
        else:
            pl.when(jnp.logical_and(j >= lo, j < hi))(_store)


def _proj(h, w, segs, extra=(), extra_specs=(), tm=1024, tn=512, name="proj"):
    t, k = h.shape
    n = w.shape[1]
    bounds, lo = [], 0
    for n_cols, epi, _ in segs:
        assert n_cols % tn == 0
        bounds.append((lo, lo + n_cols // tn, epi))
        lo += n_cols // tn
    assert lo * tn == n
    out_specs = [
        pl.BlockSpec((tm, tn), functools.partial(
            lambda i, j, lo, hi: (i, jnp.clip(j - lo, 0, hi - lo - 1)), lo=lo_, hi=hi_))
        for lo_, hi_, _ in bounds]
    out_shape = [jax.ShapeDtypeStruct((t, n_cols), dt) for n_cols, _, dt in segs]
    return pl.pallas_call(
        functools.partial(_proj_kernel, bounds, len(extra)),
        grid=(t // tm, n // tn),
        in_specs=[pl.BlockSpec((tm, k), lambda i, j: (i, 0)),
                  pl.BlockSpec((k, tn), lambda i, j: (0, j))] + list(extra_specs),
        out_specs=out_specs,
        out_shape=out_shape,
        compiler_params=_cparams("parallel", "arbitrary"),
        name=name,
    )(h, w, *extra)


def _ln_mod_store(y, x_ref, gate_ref, lnw_ref, lnb_ref, sc_ref, sh_ref, xo_ref, ho_ref, rows=slice(None)):
    z = DN_ALPHA * x_ref[rows, :] + (1.0 + gate_ref[...]) * y
    mu = jnp.mean(z, axis=-1, keepdims=True)
    zc = z - mu
    var = jnp.mean(zc * zc, axis=-1, keepdims=True)
    xn = zc * lax.rsqrt(var + LN_EPS) * lnw_ref[...] + lnb_ref[...]
    xo_ref[rows, :] = xn
    if ho_ref is not None:
        ho_ref[rows, :] = (xn * (1.0 + sc_ref[...]) + sh_ref[...]).astype(BF16)


OUT_LN_CHUNKS = 2


def _out_ln_kernel(n_a, has_next, *refs):
    a_refs = refs[:n_a]
    w_ref, x_ref, gate_ref, lnw_ref, lnb_ref = refs[n_a:n_a + 5]
    rest = refs[n_a + 5:]
    if has_next:
        sc_ref, sh_ref, xo_ref, ho_ref = rest
    else:
        (xo_ref,), sc_ref, sh_ref, ho_ref = rest, None, None, None
    rc = x_ref.shape[0] // OUT_LN_CHUNKS
    for c in range(OUT_LN_CHUNKS):
        rows = slice(c * rc, (c + 1) * rc)
        a = a_refs[0][rows, :]
        for r in a_refs[1:]:
            a = a.astype(F32) + r[rows, :].astype(F32)
        y = _dot(a.astype(BF16), w_ref[...])
        _ln_mod_store(y, x_ref, gate_ref, lnw_ref, lnb_ref, sc_ref, sh_ref, xo_ref, ho_ref, rows)


def _out_ln(a_list, w, x, mods, sub, sub_next, ln_w, ln_b, seq, tm=512):
    t, d = x.shape
    k = w.shape[0]
    tpb = seq // tm
    has_next = sub_next is not None
    in_specs = [pl.BlockSpec((tm, k), lambda i: (i, 0)) for _ in a_list]
    in_specs += [pl.BlockSpec((k, d), lambda i: (0, 0), pipeline_mode=pl.Buffered(1)),
                 pl.BlockSpec((tm, d), lambda i: (i, 0)),
                 _mod_spec(d, sub, GATE, tpb), _row_spec(d), _row_spec(d)]
    args = list(a_list) + [w, x, mods, ln_w.reshape(1, d), ln_b.reshape(1, d)]
    out_specs = [pl.BlockSpec((tm, d), lambda i: (i, 0))]
    out_shape = [jax.ShapeDtypeStruct((t, d), F32)]
    if has_next:
        in_specs += [_mod_spec(d, sub_next, SCALE, tpb), _mod_spec(d, sub_next, SHIFT, tpb)]
        args += [mods, mods]
        out_specs.append(pl.BlockSpec((tm, d), lambda i: (i, 0)))
        out_shape.append(jax.ShapeDtypeStruct((t, d), BF16))
    res = pl.pallas_call(
        functools.partial(_out_ln_kernel, len(a_list), has_next),
        grid=(t // tm,),
        in_specs=in_specs, out_specs=out_specs, out_shape=out_shape,
        compiler_params=_cparams("parallel"),
        name="out_ln",
    )(*args)
    return (res[0], res[1]) if has_next else (res[0], None)


MLP_EPILOGUE_CHUNKS = 2


def _mlp_kernel(has_next, h_ref, w1_ref, w2_ref, x_ref, gate_ref, lnw_ref, lnb_ref, *rest):
    if has_next:
        sc_ref, sh_ref, xo_ref, ho_ref, acc_ref = rest
    else:
        (xo_ref, acc_ref), sc_ref, sh_ref, ho_ref = rest, None, None, None
    j = pl.program_id(1)
    last = pl.num_programs(1) - 1
    tm = h_ref.shape[0]

    def partial_out(rows):
        hid = _dot(h_ref[rows, :], w1_ref[...])
        hid = jnp.square(jnp.maximum(hid, 0.0)).astype(BF16)
        return _dot(hid, w2_ref[...])

    @pl.when(j == 0)
    def _():
        acc_ref[...] = partial_out(slice(None))

    @pl.when(jnp.logical_and(j > 0, j < last))
    def _():
        acc_ref[...] += partial_out(slice(None))

    @pl.when(j == last)
    def _():
        rc = tm // MLP_EPILOGUE_CHUNKS
        for c in range(MLP_EPILOGUE_CHUNKS):
            rows = slice(c * rc, (c + 1) * rc)
            y = acc_ref[rows, :] + partial_out(rows)
            _ln_mod_store(y, x_ref, gate_ref, lnw_ref, lnb_ref, sc_ref, sh_ref, xo_ref, ho_ref, rows)


def _mlp(h, w1, w2, x, mods, sub, sub_next, ln_w, ln_b, seq, tm=512, tf=1024):
    t, d = x.shape
    ff = w1.shape[1]
    assert ff // tf >= 2
    tpb = seq // tm
    has_next = sub_next is not None
    in_specs = [pl.BlockSpec((tm, d), lambda i, j: (i, 0)),
                pl.BlockSpec((d, tf), lambda i, j: (0, j)),
                pl.BlockSpec((tf, d), lambda i, j: (j, 0)),
                pl.BlockSpec((tm, d), lambda i, j: (i, 0)),
                _mod_spec(d, sub, GATE, tpb), _row_spec(d), _row_spec(d)]
    args = [h, w1, w2, x, mods, ln_w.reshape(1, d), ln_b.reshape(1, d)]
    out_specs = [pl.BlockSpec((tm, d), lambda i, j: (i, 0))]
    out_shape = [jax.ShapeDtypeStruct((t, d), F32)]
    if has_next:
        in_specs += [_mod_spec(d, sub_next, SCALE, tpb), _mod_spec(d, sub_next, SHIFT, tpb)]
        args += [mods, mods]
        out_specs.append(pl.BlockSpec((tm, d), lambda i, j: (i, 0)))
        out_shape.append(jax.ShapeDtypeStruct((t, d), BF16))
    res = pl.pallas_call(
        functools.partial(_mlp_kernel, has_next),
        grid=(t // tm, ff // tf),
        in_specs=in_specs, out_specs=out_specs, out_shape=out_shape,
        scratch_shapes=[pltpu.VMEM((tm, d), F32)],
        compiler_params=_cparams("parallel", "arbitrary"),
        name="mlp",
    )(*args)
    return (res[0], res[1]) if has_next else (res[0], None)


HALO = 8


def _lru_scan_kernel(ts, gy_ref, xb_ref, cw_ref, cb_ref, gw_ref, gb_ref, lam_ref, o_ref,
                     buf, a_scr, u_scr, h_scr):
    i = pl.program_id(1)
    width = xb_ref.shape[-1]
    bd = width // LRU_BLOCKS

    @pl.when(i == 0)
    def _():
        buf[0:HALO, :] = jnp.zeros((HALO, width), F32)
        h_scr[...] = jnp.zeros_like(h_scr)

    @pl.when(i > 0)
    def _():
        buf[0:HALO, :] = buf[ts:ts + HALO, :]

    buf[HALO:HALO + ts, :] = xb_ref[...]
    cw = cw_ref[...]
    taps = [buf[HALO - (CONV_WIDTH - 1) + k:HALO - (CONV_WIDTH - 1) + k + ts, :] * cw[k:k + 1, :]
            for k in range(CONV_WIDTH)]
    conv = taps[0]
    for tap in taps[1:]:
        conv = conv + tap
    xc = cb_ref[...] + conv
    xcb = xc.astype(BF16)
    gates = []
    for jg in range(2):
        cols = [_dot(xcb[:, n * bd:(n + 1) * bd], gw_ref[jg, n]) for n in range(LRU_BLOCKS)]
        gates.append(jnp.concatenate(cols, axis=1) + gb_ref[jg:jg + 1, :])
    r = 0.5 * (1.0 + jnp.tanh(0.5 * gates[0]))
    ig = 0.5 * (1.0 + jnp.tanh(0.5 * gates[1]))
    neg_lam = -lam_ref[...]
    softplus = jnp.maximum(neg_lam, 0.0) + jnp.log1p(jnp.exp(-jnp.abs(neg_lam)))
    log_a = -LRU_C * r * softplus
    a = jnp.exp(log_a)
    a_scr[...] = a
    one_m_a2 = -jnp.tanh(log_a) * (1.0 + a * a)
    root = one_m_a2 * lax.rsqrt(jnp.maximum(one_m_a2, F32_TINY))
    u_scr[...] = root * (ig * xc)

    def step(t, h):
        h = a_scr[pl.ds(t, 1), :] * h + u_scr[pl.ds(t, 1), :]
        u_scr[pl.ds(t, 1), :] = h
        return h

    h_scr[...] = lax.fori_loop(0, ts, step, h_scr[...], unroll=8)
    o_ref[...] = (gy_ref[...] * u_scr[...]).astype(BF16)


def _lru_scan(gy, xb, conv_w, conv_b, gate_w, gate_b, lam, ts=512):
    bsz, seq, width = xb.shape
    bd = width // LRU_BLOCKS
    tile = pl.BlockSpec((None, ts, width), lambda b, i: (b, i, 0))
    return pl.pallas_call(
        functools.partial(_lru_scan_kernel, ts),
        grid=(bsz, seq // ts),
        in_specs=[tile, tile,
                  pl.BlockSpec((CONV_WIDTH, width), lambda b, i: (0, 0)),
                  pl.BlockSpec((1, width), lambda b, i: (0, 0)),
                  pl.BlockSpec((2, LRU_BLOCKS, bd, bd), lambda b, i: (0, 0, 0, 0)),
                  pl.BlockSpec((2, width), lambda b, i: (0, 0)),
                  pl.BlockSpec((1, width), lambda b, i: (0, 0))],
        out_specs=tile,
        out_shape=jax.ShapeDtypeStruct((bsz, seq, width), BF16),
        scratch_shapes=[pltpu.VMEM((ts + HALO, width), F32), pltpu.VMEM((ts, width), F32),
                        pltpu.VMEM((ts, width), F32), pltpu.VMEM((1, width), F32)],
        compiler_params=_cparams("parallel", "arbitrary"),
        name="lru_scan",
    )(gy, xb, conv_w, conv_b.reshape(1, width), gate_w.astype(BF16), gate_b, lam.reshape(1, width))


def _lru_mixer(h, bsz, seq, w_in, conv_w, conv_b, gate_w, gate_b, lam):
    width = w_in.shape[1] // 2
    gy, xb = _proj(h, w_in.astype(BF16),
                   [(width, lambda acc: _gelu_tanh(acc), BF16), (width, lambda acc: acc, F32)],
                   tn=1024, name="lru_proj")
    a = _lru_scan(gy.reshape(bsz, seq, width), xb.reshape(bsz, seq, width),
                  conv_w, conv_b, gate_w, gate_b, lam)
    return [a.reshape(bsz * seq, width)]


def _t5_bucket_np(dist):
    n = np.maximum(np.asarray(dist, np.int64), 0)
    max_exact = T5_BUCKETS // 2
    n_large = T5_BUCKETS - max_exact
    ratio = T5_MAX_DIST // max_exact
    thresholds = []
    for k in range(1, n_large):
        m = max_exact
        while m ** n_large < ratio ** k * max_exact ** n_large:
            m += 1
        thresholds.append(m)
    large = max_exact + sum((n >= th).astype(np.int64) for th in thresholds)
    return np.where(n < max_exact, n, large).astype(np.int32)


def _bias_table_kernel(sub_far, rb_ref, idx_ref, o_ref):
    hd = pl.program_id(0)
    idx = idx_ref[...]
    base = rb_ref[T5_BUCKETS - 1, hd] if sub_far else 0.0
    acc = jnp.full(idx.shape, NEG_MASK, F32)
    for b in range(T5_BUCKETS):
        acc = jnp.where(idx == b, (rb_ref[b, hd] - base) * LOG2E, acc)
    o_ref[...] = acc


def _bias_table(rel_bias, idx_np, sub_far, name):
    rows, cols = idx_np.shape
    tr = 128
    n_heads = rel_bias.shape[1]
    return pl.pallas_call(
        functools.partial(_bias_table_kernel, sub_far),
        grid=(n_heads, rows // tr),
        in_specs=[pl.BlockSpec(memory_space=pltpu.SMEM),
                  pl.BlockSpec((tr, cols), lambda hd, i: (i, 0))],
        out_specs=pl.BlockSpec((None, tr, cols), lambda hd, i: (hd, i, 0)),
        out_shape=jax.ShapeDtypeStruct((n_heads, rows, cols), F32),
        compiler_params=_cparams("parallel", "parallel"),
        name=name,
    )(rel_bias, jnp.asarray(idx_np))


def _compress_kernel(seq, x_ref, pe_ref, w1_ref, b1_ref, w2_ref, o_ref, xs, nat):
    xs[0:seq, :] = x_ref[...]
    xs[seq:CMP_ROWS, :] = jnp.zeros((CMP_ROWS - seq, LANES), F32)
    acc = jnp.zeros((N_CMP_SLOTS, LANES), F32)
    for l in range(CMP_BLOCK):
        rows = xs[pl.ds(l, N_CMP_SLOTS, stride=CMP_STRIDE), :] + pe_ref[l:l + 1, :]
        acc = acc + _dot(rows.astype(BF16), w1_ref[l])
    hid = _gelu_tanh(acc + b1_ref[...])
    nat[...] = _dot(hid.astype(BF16), w2_ref[...])
    for r in range(4):
        o_ref[r * LANES:(r + 1) * LANES, :] = nat[pl.ds(r, LANES, stride=4), :].astype(BF16)


def _compress(cmp_tok, pe, w1, b1, w2):
    bsz, seq, _ = cmp_tok.shape
    g = NSA_KV_GROUPS
    dh = NSA_HEAD_DIM
    return pl.pallas_call(
        functools.partial(_compress_kernel, seq),
        grid=(bsz, 2, g),
        in_specs=[pl.BlockSpec((None, seq, dh), lambda b, j, gi: (b, 0, j * g + gi)),
                  pl.BlockSpec((None, CMP_BLOCK, dh), lambda b, j, gi: (j, 0, 0)),
                  pl.BlockSpec((None, CMP_BLOCK, dh, dh), lambda b, j, gi: (j, 0, 0, 0)),
                  pl.BlockSpec((None, 1, dh), lambda b, j, gi: (j, 0, 0)),
                  pl.BlockSpec((None, dh, dh), lambda b, j, gi: (j, 0, 0))],
        out_specs=pl.BlockSpec((None, None, None, N_CMP_SLOTS, dh), lambda b, j, gi: (b, j, gi, 0, 0)),
        out_shape=jax.ShapeDtypeStruct((bsz, 2, g, N_CMP_SLOTS, dh), BF16),
        scratch_shapes=[pltpu.VMEM((CMP_ROWS, dh), F32), pltpu.VMEM((N_CMP_SLOTS, dh), F32)],
        compiler_params=_cparams("parallel", "parallel", "parallel"),
        name="nsa_compress",
    )(cmp_tok, pe, w1.astype(BF16), b1.reshape(2, 1, dh), w2.astype(BF16))


def _cmp_table_idx(tq):
    ql = np.arange(tq)[:, None, None]
    r = np.arange(4)[None, :, None]
    m = np.arange(LANES)[None, None, :]
    e = np.where(m <= (tq - CMP_BLOCK) // SEL_BLOCK, -m, LANES - m)
    dist = SEL_BLOCK * e + ql - CMP_STRIDE * r - (CMP_BLOCK - 1)
    return _t5_bucket_np(dist).reshape(tq, 4 * LANES)


CMP_SUB = 2


def _cmp_attn_kernel(tq, q_ref, kc_ref, vc_ref, tb_ref, g_ref, o_ref, val_ref):
    i = pl.program_id(2)
    kc = kc_ref[...]
    vc = vc_ref[...]
    ql = lax.broadcasted_iota(jnp.int32, (tq, N_CMP_SLOTS), 0)
    slot = lax.broadcasted_iota(jnp.int32, (tq, N_CMP_SLOTS), 1)
    rel = ql - SEL_BLOCK * (slot & (LANES - 1)) - CMP_STRIDE * (slot >> 7) - (CMP_BLOCK - 1)
    lane = lax.broadcasted_iota(jnp.int32, (tq, LANES), 1)
    for sub in range(CMP_SUB):
        rows = slice(sub * tq, (sub + 1) * tq)
        t0 = (i * CMP_SUB + sub) * tq
        first_blk = (i * CMP_SUB + sub) * (tq // SEL_BLOCK)
        gate = g_ref[rows, :]
        causal = jnp.where(t0 + rel >= 0, 0.0, NEG_MASK)
        row_ok = (t0 + lax.broadcasted_iota(jnp.int32, (tq, 1), 0)) >= CMP_BLOCK - 1
        p_sum = jnp.zeros((tq, N_CMP_SLOTS), F32)
        for hd in range(NSA_HPG):
            tb = tb_ref[hd]
            tb = jnp.concatenate(
                [pltpu.roll(tb[:, c * LANES:(c + 1) * LANES], first_blk, axis=1) for c in range(4)], axis=1)
            s = _dot_nt(q_ref[rows, hd * LANES:(hd + 1) * LANES], kc) + tb + causal
            e = jnp.exp2(s - jnp.max(s, axis=-1, keepdims=True))
            p = e * jnp.where(row_ok, 1.0 / jnp.sum(e, axis=-1, keepdims=True), 0.0)
            p_sum = p_sum + p
            o_ref[rows, hd * LANES:(hd + 1) * LANES] = (
                _dot(p.astype(BF16), vc) * gate[:, hd:hd + 1]).astype(o_ref.dtype)
        ph = [p_sum[:, c * LANES:(c + 1) * LANES] for c in range(4)]
        prev = jnp.where(lane == 0, 0.0, pltpu.roll(ph[3], 1, axis=1))
        score = (((prev + ph[0]) + ph[1]) + ph[2]) + ph[3]
        cur = (t0 + lax.broadcasted_iota(jnp.int32, (tq, LANES), 0)) >> 6
        forced = jnp.logical_or(lane == 0, jnp.logical_or(lane == cur, lane == cur - 1))
        val_ref[rows, :] = jnp.where(forced, SEL_FORCE, jnp.where(lane <= cur, score, -SEL_FORCE))


N_FORCED = 3


def _topk_mask_kernel(n_top, val_ref, m_ref):
    val0 = val_ref[...]
    lane = lax.broadcasted_iota(jnp.int32, val0.shape, 1).astype(F32)
    forced = val0 >= 0.5 * SEL_FORCE
    val = jnp.where(forced, -jnp.inf, val0)
    sel = jnp.where(forced, 1.0, 0.0)
    for _ in range(n_top - N_FORCED):
        mx = jnp.max(val, axis=-1, keepdims=True)
        first = jnp.min(jnp.where(val == mx, lane, float(LANES)), axis=-1, keepdims=True)
        pick = lane == first
        sel = jnp.where(pick, 1.0, sel)
        val = jnp.where(pick, -jnp.inf, val)
    m_ref[...] = jnp.where(jnp.logical_and(sel > 0.0, val0 >= 0.0), 0.0, NEG_MASK).astype(BF16)


def _topk_mask(val, n_top, tr=1024):
    rows = val.shape[0]
    return pl.pallas_call(
        functools.partial(_topk_mask_kernel, n_top),
        grid=(rows // tr,),
        in_specs=[pl.BlockSpec((tr, LANES), lambda i: (i, 0))],
        out_specs=pl.BlockSpec((tr, LANES), lambda i: (i, 0)),
        out_shape=jax.ShapeDtypeStruct((rows, LANES), BF16),
        compiler_params=_cparams("parallel"),
        name="nsa_topk_mask",
    )(val)


def _cmp_attn(q, kvc, table, gates, tq=256):
    bsz, seq, _ = q.shape
    g = NSA_KV_GROUPS
    qw = NSA_HPG * NSA_HEAD_DIM
    rows = tq * CMP_SUB
    return pl.pallas_call(
        functools.partial(_cmp_attn_kernel, tq),
        grid=(bsz, g, seq // rows),
        in_specs=[pl.BlockSpec((None, rows, qw), lambda b, gi, i: (b, i, gi)),
                  pl.BlockSpec((None, None, None, N_CMP_SLOTS, NSA_HEAD_DIM), lambda b, gi, i: (b, 0, gi, 0, 0)),
                  pl.BlockSpec((None, None, None, N_CMP_SLOTS, NSA_HEAD_DIM), lambda b, gi, i: (b, 1, gi, 0, 0)),
                  pl.BlockSpec((NSA_HPG, tq, N_CMP_SLOTS), lambda b, gi, i: (gi, 0, 0)),
                  pl.BlockSpec((None, rows, LANES), lambda b, gi, i: (b, i, gi))],
        out_specs=[pl.BlockSpec((None, rows, qw), lambda b, gi, i: (b, i, gi)),
                   pl.BlockSpec((None, None, rows, LANES), lambda b, gi, i: (b, gi, i, 0))],
        out_shape=[jax.ShapeDtypeStruct((bsz, seq, g * qw), BF16),
                   jax.ShapeDtypeStruct((bsz, g, seq, LANES), F32)],
        compiler_params=_cparams("parallel", "parallel", "parallel"),
        name="nsa_cmp_attn",
    )(q, kvc, kvc, table, gates)


SEL_TQ = 256
SEL_TK = 256
SEL_N_DELTA = -(-(T5_MAX_DIST - 1 + SEL_TK) // SEL_TQ)
SEL_N_NEAR = -(-SEL_N_DELTA // (SEL_TK // SEL_TQ))
SEL_STREAMS = 4


def _sel_table_idx():
    d = np.arange(SEL_N_DELTA)[:, None, None]
    ql = np.arange(SEL_TQ)[None, :, None]
    kl = np.arange(SEL_TK)[None, None, :]
    dist = ql - kl + SEL_TQ * d
    idx = np.where(dist >= 0, _t5_bucket_np(dist), MASK_BUCKET)
    return idx.reshape(SEL_N_DELTA * SEL_TQ, SEL_TK).astype(np.int32)


def _sel_attn_kernel(q_ref, m_ref, k_ref, v_ref, oh_ref, nb_ref, g_ref, o_ref, qa, m_scr, acc):
    tq, tk = SEL_TQ, SEL_TK
    ratio = tk // tq
    i = pl.program_id(2)
    kd = i // ratio
    sel_mask = m_ref[...]
    for hd in range(NSA_HPG):
        qa[hd * tq:(hd + 1) * tq, 0:LANES] = q_ref[:, hd * LANES:(hd + 1) * LANES]
        qa[hd * tq:(hd + 1) * tq, LANES:2 * LANES] = sel_mask
    m_scr[...] = jnp.full(m_scr.shape, -jnp.inf, F32)
    acc[...] = jnp.zeros_like(acc)
    n_kt = k_ref.shape[0] // tk
    ones = jnp.ones((tk, LANES), BF16)

    def update(st, tiles):
        scores, values = [], []
        for kt, valid, delta in tiles:
            k0 = pl.multiple_of(jnp.clip(kt, 0, n_kt - 1) * tk, tk)
            ka = jnp.concatenate([k_ref[pl.ds(k0, tk), :], oh_ref[pl.ds(k0, tk), :]], axis=1)
            s = _dot_nt(qa[...], ka)
            if valid is not None:
                s = s + jnp.where(valid, 0.0, NEG_MASK)
            if delta is not None:
                s = s + jnp.concatenate([nb_ref[hd, delta] for hd in range(NSA_HPG)], axis=0)
            scores.append(s)
            values.append(jnp.concatenate([v_ref[pl.ds(k0, tk), :], ones], axis=1))
        s_max = scores[0]
        for s in scores[1:]:
            s_max = jnp.maximum(s_max, s)
        m_prev = m_scr[st]
        m_new = jnp.maximum(m_prev, jnp.max(s_max, axis=-1, keepdims=True))
        alpha = jnp.exp2(m_prev - m_new)
        m_rep = jnp.concatenate([m_new] * (tk // LANES), axis=1)
        pv = None
        for s, va in zip(scores, values):
            part = _dot(jnp.exp2(s - m_rep).astype(BF16), va)
            pv = part if pv is None else pv + part
        acc[st] = jnp.concatenate([alpha, alpha], axis=1) * acc[st] + pv
        m_scr[st] = m_new

    n_far = jnp.maximum(kd - (SEL_N_NEAR - 1), 0)
    pair = 2 * SEL_STREAMS

    def far_pairs(it, carry):
        for st in range(SEL_STREAMS):
            kt = it * pair + 2 * st
            update(st, [(kt, None, None), (kt + 1, None, None)])
        return carry

    lax.fori_loop(0, n_far // pair, far_pairs, 0)
    base = (n_far // pair) * pair
    rem = n_far - base

    @pl.when(rem >= SEL_STREAMS)
    def _():
        for st in range(SEL_STREAMS):
            update(st, [(base + st, None, None)])

    base = base + jnp.where(rem >= SEL_STREAMS, SEL_STREAMS, 0)
    left = n_far - base
    slots = [(base + f, f < left, None) for f in range(SEL_STREAMS - 1)]
    for back in range(SEL_N_NEAR - 1, -1, -1):
        kt = kd - back
        slots.append((kt, kt >= 0, (i - kd * ratio) + back * ratio))
    for st in range(SEL_STREAMS):
        update(st, slots[st::SEL_STREAMS])

    m_all = m_scr[0]
    for st in range(1, SEL_STREAMS):
        m_all = jnp.maximum(m_all, m_scr[st])
    tot = jnp.zeros((NSA_HPG * tq, 2 * LANES), F32)
    for st in range(SEL_STREAMS):
        w = jnp.exp2(m_scr[st] - m_all)
        tot = tot + jnp.concatenate([w, w], axis=1) * acc[st]
    gate = g_ref[...]
    out = tot[:, :LANES] / jnp.maximum(tot[:, LANES:], 1e-30)
    for hd in range(NSA_HPG):
        o_ref[:, hd * LANES:(hd + 1) * LANES] = (
            out[hd * tq:(hd + 1) * tq, :] * gate[:, NSA_HPG + hd:NSA_HPG + hd + 1]).astype(o_ref.dtype)


def _sel_attn(q, sel_mask, kv_tok, onehot, table, gates):
    bsz, seq, _ = q.shape
    g = NSA_KV_GROUPS
    dh = NSA_HEAD_DIM
    qw = NSA_HPG * dh
    tq, tk = SEL_TQ, SEL_TK
    rows = NSA_HPG * tq
    return pl.pallas_call(
        _sel_attn_kernel,
        grid=(bsz, g, seq // tq),
        in_specs=[pl.BlockSpec((None, tq, qw), lambda b, gi, i: (b, i, gi)),
                  pl.BlockSpec((None, None, tq, LANES), lambda b, gi, i: (b, gi, i, 0)),
                  pl.BlockSpec((None, seq, dh), lambda b, gi, i: (b, 0, gi)),
                  pl.BlockSpec((None, seq, dh), lambda b, gi, i: (b, 0, g + gi)),
                  pl.BlockSpec((seq, LANES), lambda b, gi, i: (0, 0)),
                  pl.BlockSpec((NSA_HPG, SEL_N_DELTA, tq, tk), lambda b, gi, i: (gi, 0, 0, 0)),
                  pl.BlockSpec((None, tq, LANES), lambda b, gi, i: (b, i, gi))],
        out_specs=pl.BlockSpec((None, tq, qw), lambda b, gi, i: (b, i, gi)),
        out_shape=jax.ShapeDtypeStruct((bsz, seq, g * qw), BF16),
        scratch_shapes=[pltpu.VMEM((rows, 2 * LANES), BF16), pltpu.VMEM((SEL_STREAMS, rows, LANES), F32),
                        pltpu.VMEM((SEL_STREAMS, rows, 2 * dh), F32)],
        compiler_params=_cparams("parallel", "parallel", "parallel"),
        name="nsa_sel_attn",
    )(q, sel_mask, kv_tok, kv_tok, onehot, table, gates)


WIN_TQ = 128
WIN_SUB = 4
WIN_KEYS = WINDOW + WIN_TQ


def _win_table_idx():
    ql = np.arange(WIN_TQ)[:, None]
    kl = np.arange(WIN_KEYS)[None, :]
    dist = ql + WINDOW - kl
    return np.where((dist >= 0) & (dist < WINDOW), _t5_bucket_np(dist), MASK_BUCKET).astype(np.int32)


def _win_attn_kernel(seq, q_ref, k_ref, v_ref, wb_ref, g_ref, o_ref, kpad, vpad, qs):
    tq = WIN_TQ
    i = pl.program_id(2)

    @pl.when(i == 0)
    def _():
        kpad[0:WINDOW, :] = jnp.zeros((WINDOW, LANES), BF16)
        vpad[0:WINDOW, :] = jnp.zeros((WINDOW, LANES), BF16)
        kpad[WINDOW:WINDOW + seq, :] = k_ref[...]
        vpad[WINDOW:WINDOW + seq, :] = v_ref[...]

    bias = wb_ref[...].reshape(NSA_HPG * tq, WIN_KEYS)

    def step(header_visible):
        for sub in range(WIN_SUB):
            rows = slice(sub * tq, (sub + 1) * tq)
            t0 = pl.multiple_of((i * WIN_SUB + sub) * tq, tq)
            for hd in range(NSA_HPG):
                qs[sub, hd * tq:(hd + 1) * tq, :] = q_ref[rows, hd * LANES:(hd + 1) * LANES]
            keys = kpad[pl.ds(t0, WIN_KEYS), :]
            vals = vpad[pl.ds(t0, WIN_KEYS), :]
            s = _dot_nt(qs[sub], keys) + bias
            if header_visible:
                kl = lax.broadcasted_iota(jnp.int32, s.shape, 1)
                s = jnp.where(kl >= WINDOW - t0, s, NEG_MASK)
            e = jnp.exp2(s - jnp.max(s, axis=-1, keepdims=True))
            out = _dot(e.astype(BF16), vals) * (1.0 / jnp.sum(e, axis=-1, keepdims=True))
            gate = g_ref[rows, :]
            for hd in range(NSA_HPG):
                o_ref[rows, hd * LANES:(hd + 1) * LANES] = (
                    out[hd * tq:(hd + 1) * tq, :]
                    * gate[:, 2 * NSA_HPG + hd:2 * NSA_HPG + hd + 1]).astype(o_ref.dtype)

    n_header_steps = -(-WINDOW // (WIN_SUB * tq))
    pl.when(i < n_header_steps)(functools.partial(step, True))
    pl.when(i >= n_header_steps)(functools.partial(step, False))


def _win_attn(q, kv_tok, table, gates):
    bsz, seq, _ = q.shape
    g = NSA_KV_GROUPS
    dh = NSA_HEAD_DIM
    qw = NSA_HPG * dh
    tq = WIN_TQ
    rows = tq * WIN_SUB
    return pl.pallas_call(
        functools.partial(_win_attn_kernel, seq),
        grid=(bsz, g, seq // rows),
        in_specs=[pl.BlockSpec((None, rows, qw), lambda b, gi, i: (b, i, gi)),
                  pl.BlockSpec((None, seq, dh), lambda b, gi, i: (b, 0, 2 * g + gi)),
                  pl.BlockSpec((None, seq, dh), lambda b, gi, i: (b, 0, 3 * g + gi)),
                  pl.BlockSpec((NSA_HPG, tq, WIN_KEYS), lambda b, gi, i: (gi, 0, 0)),
                  pl.BlockSpec((None, rows, LANES), lambda b, gi, i: (b, i, gi))],
        out_specs=pl.BlockSpec((None, rows, qw), lambda b, gi, i: (b, i, gi)),
        out_shape=jax.ShapeDtypeStruct((bsz, seq, g * qw), BF16),
        scratch_shapes=[pltpu.VMEM((seq + WINDOW, dh), BF16), pltpu.VMEM((seq + WINDOW, dh), BF16),
                        pltpu.VMEM((WIN_SUB, NSA_HPG * tq, dh), BF16)],
        compiler_params=_cparams("parallel", "parallel", "arbitrary"),
        name="nsa_win_attn",
    )(q, kv_tok, kv_tok, table, gates)


def _nsa_mixer(h, bsz, seq, rel_bias, w_in, cmp_pe, cmp_w1, cmp_b1, cmp_w2):
    g, hpg, dh = NSA_KV_GROUPS, NSA_HPG, NSA_HEAD_DIM
    qw = NSA_HEADS * dh
    kvw = g * dh
    n_main = qw + 2 * N_BRANCH * kvw
    scale = dh ** -0.5 * LOG2E
    q, cmp_tok, kv_tok = _proj(
        h, w_in[:, :n_main].astype(BF16),
        [(qw, lambda acc: acc * scale, BF16), (2 * kvw, lambda acc: acc, F32), (4 * kvw, lambda acc: acc, BF16)],
        tn=1024, name="nsa_proj")
    src = np.zeros((g * LANES,), np.int32)
    valid = np.zeros((g * LANES,), bool)
    for gi in range(g):
        for br in range(N_BRANCH):
            for hd in range(hpg):
                src[gi * LANES + br * hpg + hd] = n_main + br * NSA_HEADS + gi * hpg + hd
                valid[gi * LANES + br * hpg + hd] = True
    w_gate = jnp.where(jnp.asarray(valid)[None, :], w_in[:, src], 0.0).astype(BF16)
    (gates,) = _proj(h, w_gate, [(g * LANES, lambda acc: _sigmoid(acc), F32)], name="nsa_gate_proj")

    q = q.reshape(bsz, seq, qw)
    gates = gates.reshape(bsz, seq, g * LANES)
    kv_tok = kv_tok.reshape(bsz, seq, 4 * kvw)
    kvc = _compress(cmp_tok.reshape(bsz, seq, 2 * kvw), cmp_pe, cmp_w1, cmp_b1, cmp_w2)

    n_top = min(SEL_TOP_N, seq // SEL_BLOCK)
    cmp_tq = 256
    cmp_table = _bias_table(rel_bias, _cmp_table_idx(cmp_tq), False, "nsa_cmp_bias")
    o_c, sel_val = _cmp_attn(q, kvc, cmp_table, gates, cmp_tq)
    sel_mask = _topk_mask(sel_val.reshape(bsz * g * seq, LANES), n_top).reshape(bsz, g, seq, LANES)

    sel_table = _bias_table(rel_bias, _sel_table_idx(), True, "nsa_sel_bias")
    sel_table = sel_table.reshape(NSA_HEADS, SEL_N_DELTA, SEL_TQ, SEL_TK)
    onehot = (jnp.arange(seq, dtype=jnp.int32)[:, None] // SEL_BLOCK
              == jnp.arange(LANES, dtype=jnp.int32)[None, :]).astype(BF16)
    o_s = _sel_attn(q, sel_mask, kv_tok, onehot, sel_table, gates)

    win_table = _bias_table(rel_bias, _win_table_idx(), False, "nsa_win_bias")
    o_w = _win_attn(q, kv_tok, win_table, gates)
    t = bsz * seq
    return [o_c.reshape(t, qw), o_s.reshape(t, qw), o_w.reshape(t, qw)]


def _rotary_epi(scale, acc, cos, sin):
    half = cos.shape[-1]
    outs = []
    for hd in range(acc.shape[1] // (2 * half)):
        x1 = acc[:, 2 * hd * half:(2 * hd + 1) * half]
        x2 = acc[:, (2 * hd + 1) * half:(2 * hd + 2) * half]
        outs += [x1 * cos - x2 * sin, x1 * sin + x2 * cos]
    rot = jnp.concatenate(outs, axis=1)
    return rot * scale if scale != 1.0 else rot


RET_HPS = 2


def _ret_kernel(tc, q_ref, k_ref, v_ref, g_ref, dec_ref, xi_ref, zeta_ref, cd_ref, gw_ref, gb_ref,
                o_ref, state):
    i = pl.program_id(2)
    dk = q_ref.shape[-1] // RET_HPS
    dv = v_ref.shape[-1] // RET_HPS

    @pl.when(i == 0)
    def _():
        state[...] = jnp.zeros_like(state)

    for c in range(tc // RET_CHUNK):
        rows = slice(c * RET_CHUNK, (c + 1) * RET_CHUNK)
        for hh in range(RET_HPS):
            qk_cols = slice(hh * dk, (hh + 1) * dk)
            v_cols = slice(hh * dv, (hh + 1) * dv)
            qc = q_ref[rows, qk_cols]
            kc = k_ref[rows, qk_cols]
            vc = v_ref[rows, v_cols]
            inner = _dot_nt(qc, kc) * dec_ref[hh]
            st = state[hh]
            out = _dot(inner.astype(BF16), vc) + _dot(qc, st.astype(BF16)) * xi_ref[hh]
            kz = (kc.astype(F32) * zeta_ref[hh]).astype(BF16)
            state[hh] = st * cd_ref[hh] + _dot_tn(kz, vc)
            mu = jnp.mean(out, axis=-1, keepdims=True)
            oc = out - mu
            var = jnp.mean(oc * oc, axis=-1, keepdims=True)
            y = (oc * lax.rsqrt(var + GN_EPS)) * gw_ref[:, v_cols] + gb_ref[:, v_cols]
            o_ref[rows, v_cols] = (g_ref[rows, v_cols] * y).astype(BF16)


def _ret_core(q, k, v, g, gn_w, gn_b, tc=512):
    bsz, seq, qk_total = q.shape
    v_total = v.shape[-1]
    nh = RET_HEADS
    dk = qk_total // nh
    dv = v_total // nh
    cs = RET_CHUNK
    hps = RET_HPS
    log_g = jnp.log1p(-jnp.exp2(-5.0 - jnp.arange(nh, dtype=F32)))
    idx = jnp.arange(cs, dtype=F32)
    diff = idx[:, None] - idx[None, :]
    decay = jnp.where(diff >= 0, jnp.exp(jnp.maximum(diff, 0.0) * log_g[:, None, None]), 0.0)
    xi = jnp.exp((idx + 1.0) * log_g[:, None])[:, :, None]
    zeta = jnp.exp((cs - 1.0 - idx) * log_g[:, None])[:, :, None]
    chunk_decay = jnp.exp(cs * log_g).reshape(nh, 1, 1)
    return pl.pallas_call(
        functools.partial(_ret_kernel, tc),
        grid=(bsz, nh // hps, seq // tc),
        in_specs=[pl.BlockSpec((None, tc, hps * dk), lambda b, hp, i: (b, i, hp)),
                  pl.BlockSpec((None, tc, hps * dk), lambda b, hp, i: (b, i, hp)),
                  pl.BlockSpec((None, tc, hps * dv), lambda b, hp, i: (b, i, hp)),
                  pl.BlockSpec((None, tc, hps * dv), lambda b, hp, i: (b, i, hp)),
                  pl.BlockSpec((hps, cs, cs), lambda b, hp, i: (hp, 0, 0)),
                  pl.BlockSpec((hps, cs, 1), lambda b, hp, i: (hp, 0, 0)),
                  pl.BlockSpec((hps, cs, 1), lambda b, hp, i: (hp, 0, 0)),
                  pl.BlockSpec((hps, 1, 1), lambda b, hp, i: (hp, 0, 0)),
                  pl.BlockSpec((1, hps * dv), lambda b, hp, i: (0, hp)),
                  pl.BlockSpec((1, hps * dv), lambda b, hp, i: (0, hp))],
        out_specs=pl.BlockSpec((None, tc, hps * dv), lambda b, hp, i: (b, i, hp)),
        out_shape=jax.ShapeDtypeStruct((bsz, seq, v_total), BF16),
        scratch_shapes=[pltpu.VMEM((hps, dk, dv), F32)],
        compiler_params=_cparams("parallel", "parallel", "arbitrary"),
        name="ret_core",
    )(q, k, v, g, decay, xi, zeta, chunk_decay, gn_w.reshape(1, v_total), gn_b.reshape(1, v_total))


def _ret_mixer(h, bsz, seq, w_in, gn_w, gn_b, tm=1024):
    qk_total = w_in.shape[0]
    v_total = (w_in.shape[1] - 2 * qk_total) // 2
    dk = qk_total // RET_HEADS
    half = dk // 2
    pos = jnp.arange(seq, dtype=F32)
    inv = ROPE_BASE ** (-jnp.arange(half, dtype=F32) / half)
    ang = pos[:, None] * inv[None, :]
    cos, sin = jnp.cos(ang), jnp.sin(ang)
    tpb = seq // tm
    rope_spec = pl.BlockSpec((tm, half), lambda i, j: (i % tpb, 0))
    q, k, v, g = _proj(
        h, w_in.astype(BF16),
        [(qk_total, functools.partial(_rotary_epi, 1.0), BF16),
         (qk_total, functools.partial(_rotary_epi, dk ** -0.5), BF16),
         (v_total, lambda acc, cos, sin: acc, BF16),
         (v_total, lambda acc, cos, sin: _silu(acc), BF16)],
        extra=(cos, sin), extra_specs=(rope_spec, rope_spec), tm=tm, tn=1024, name="ret_proj")
    a = _ret_core(q.reshape(bsz, seq, qk_total), k.reshape(bsz, seq, qk_total),
                  v.reshape(bsz, seq, v_total), g.reshape(bsz, seq, v_total), gn_w, gn_b)
    return [a.reshape(bsz * seq, v_total)]


def kernel(x, c, rel_bias, ada_w, ada_b, ln_w, ln_b, mlp_w1, mlp_w2, lru_w_in, lru_conv_w, lru_conv_b,
           lru_gate_w, lru_gate_b, lru_lambda, lru_w_out, nsa_w_in, nsa_cmp_pe, nsa_cmp_w1, nsa_cmp_b1,
           nsa_cmp_w2, nsa_w_out, ret_w_in, ret_gn_w, ret_gn_b, ret_w_out):
    bsz, seq, d = x.shape
    depth = ada_w.shape[0]
    mods = _ada_mods(c, ada_w, ada_b)
    xf = x.reshape(bsz * seq, d)
    h = _modulate(xf, mods, 0, seq)
    for layer in range(depth):
        mixer, inst = layer % N_MIXERS, layer // N_MIXERS
        sub = 2 * layer
        if mixer == 0:
            a_list = _lru_mixer(h, bsz, seq, lru_w_in[inst], lru_conv_w[inst], lru_conv_b[inst],
                                lru_gate_w[inst], lru_gate_b[inst], lru_lambda[inst])
            w_out = lru_w_out[inst]
        elif mixer == 1:
            a_list = _nsa_mixer(h, bsz, seq, rel_bias, nsa_w_in[inst], nsa_cmp_pe[inst], nsa_cmp_w1[inst],
                                nsa_cmp_b1[inst], nsa_cmp_w2[inst])
            w_out = nsa_w_out[inst]
        else:
            a_list = _ret_mixer(h, bsz, seq, ret_w_in[inst], ret_gn_w[inst], ret_gn_b[inst])
            w_out = ret_w_out[inst]
        xf, h = _out_ln(a_list, w_out.astype(BF16), xf, mods, sub, sub + 1, ln_w[layer, 0], ln_b[layer, 0], seq)
        sub_next = sub + 2 if layer + 1 < depth else None
        xf, h = _mlp(h, mlp_w1[layer].astype(BF16), mlp_w2[layer].astype(BF16), xf, mods, sub + 1, sub_next,
                     ln_w[layer, 1], ln_b[layer, 1], seq)
    return xf.reshape(bsz, seq, d)
```

```python
import functools
import math

import numpy as np
import jax
import jax.numpy as jnp
from jax import lax
from jax.experimental import pallas as pl
from jax.experimental.pallas import tpu as pltpu

F32 = jnp.float32
BF16 = jnp.bfloat16

DN_DEPTH = 4
N_MIXERS = 3
DN_ALPHA = (2 * DN_DEPTH) ** 0.25
LN_EPS = 1e-5
GN_EPS = 1e-5
LRU_BLOCKS = 8
CONV_WIDTH = 4
LRU_C = 8.0
NSA_HEADS = 16
NSA_HEAD_DIM = 128
NSA_KV_GROUPS = 4
NSA_HPG = NSA_HEADS // NSA_KV_GROUPS
N_BRANCH = 3
CMP_BLOCK = 32
CMP_STRIDE = 16
SEL_BLOCK = 64
SEL_TOP_N = 16
SEL_FORCE = 1e6
WINDOW = 512
NEG_BIG = 1e30
T5_BUCKETS = 32
T5_MAX_DIST = 1024
RET_HEADS = 8
RET_CHUNK = 128
ROPE_BASE = 10000.0

LANES = 128
VMEM_LIMIT_BYTES = 56 * 2 ** 20

NEG_MASK = -(2.0 ** 100)
MASK_BUCKET = T5_BUCKETS
F32_TINY = float(np.finfo(np.float32).tiny)
LOG2E = math.log2(math.e)

N_CMP_SLOTS = 4 * LANES
CMP_ROWS = CMP_STRIDE * N_CMP_SLOTS + 4 * CMP_STRIDE


def _cparams(*sem):
    return pltpu.CompilerParams(dimension_semantics=sem, vmem_limit_bytes=VMEM_LIMIT_BYTES)


def _sigmoid(x):
    return 1.0 / (1.0 + jnp.exp(-x))


def _silu(x):
    return x * _sigmoid(x)


def _gelu_tanh(x):
    return 0.5 * x * (1.0 + jnp.tanh(math.sqrt(2.0 / math.pi) * (x + 0.044715 * (x * x * x))))


def _dot(a, b):
    return jnp.dot(a, b, preferred_element_type=F32)


def _dot_nt(a, b):
    return lax.dot_general(a, b, (((1,), (1,)), ((), ())), preferred_element_type=F32)


def _dot_tn(a, b):
    return lax.dot_general(a, b, (((0,), (0,)), ((), ())), preferred_element_type=F32)


def _ada_kernel(c_ref, w_ref, b_ref, o_ref):
    cond = _silu(c_ref[...]).astype(BF16)
    o_ref[...] = _dot(cond, w_ref[...].astype(BF16)) + b_ref[...]


def _ada_mods(c, ada_w, ada_b):
    depth, _, d, d3 = ada_w.shape
    bsz = c.shape[0]
    n = depth * 2
    rows = 8
    c_pad = jnp.zeros((rows, d), F32).at[:bsz].set(c)
    tn = 1024
    out = pl.pallas_call(
        _ada_kernel,
        grid=(n, d3 // tn),
        in_specs=[
            pl.BlockSpec((rows, d), lambda s, j: (0, 0)),
            pl.BlockSpec((None, d, tn), lambda s, j: (s, 0, j)),
            pl.BlockSpec((None, 1, tn), lambda s, j: (s, 0, j)),
        ],
        out_specs=pl.BlockSpec((None, rows, tn), lambda s, j: (s, 0, j)),
        out_shape=jax.ShapeDtypeStruct((n, rows, d3), F32),
        compiler_params=_cparams("parallel", "parallel"),
        name="ada_mods",
    )(c_pad, ada_w.reshape(n, d, d3), ada_b.reshape(n, 1, d3))
    return out[:, :bsz].reshape(n, bsz, 3, d).transpose(0, 2, 1, 3).reshape(n, 3, bsz, 1, d)


SHIFT, SCALE, GATE = 0, 1, 2


def _mod_spec(d, sub, which, tiles_per_batch):
    return pl.BlockSpec((None, None, None, 1, d),
                        lambda i, *_: (sub, which, i // tiles_per_batch, 0, 0))


def _row_spec(d):
    return pl.BlockSpec((1, d), lambda *_: (0, 0))


def _modulate_kernel(x_ref, sc_ref, sh_ref, h_ref):
    h_ref[...] = (x_ref[...] * (1.0 + sc_ref[...]) + sh_ref[...]).astype(BF16)


def _modulate(x, mods, sub, seq, tm=512):
    t, d = x.shape
    tpb = seq // tm
    return pl.pallas_call(
        _modulate_kernel,
        grid=(t // tm,),
        in_specs=[pl.BlockSpec((tm, d), lambda i: (i, 0)),
                  _mod_spec(d, sub, SCALE, tpb), _mod_spec(d, sub, SHIFT, tpb)],
        out_specs=pl.BlockSpec((tm, d), lambda i: (i, 0)),
        out_shape=jax.ShapeDtypeStruct((t, d), BF16),
        compiler_params=_cparams("parallel"),
        name="modulate",
    )(x, mods, mods)


def _column_tiles(w, tn):
    k, n = w.shape
    return w.reshape(k, n // tn, tn).transpose(1, 0, 2)


def _proj_kernel(segs, n_extra, h_ref, w_ref, *refs):
    extra = refs[:n_extra]
    outs = refs[n_extra:]
    j = pl.program_id(1)
    for (lo, hi, epi), o_ref in zip(segs, outs):
        def _store(o_ref=o_ref, epi=epi):
            acc = _dot(h_ref[...], w_ref[...])
            o_ref[...] = epi(acc, *[e[...] for e in extra]).astype(o_ref.dtype)
        if len(segs) == 1:
            _store()
        else:
            pl.when(jnp.logical_and(j >= lo, j < hi))(_store)


def _proj(h, w, segs, extra=(), extra_specs=(), tm=1024, tn=512, name="proj"):
    t, k = h.shape
    n = w.shape[1]
    bounds, lo = [], 0
    for n_cols, epi, _ in segs:
        assert n_cols % tn == 0
        bounds.append((lo, lo + n_cols // tn, epi))
        lo += n_cols // tn
    assert lo * tn == n
    out_specs = [
        pl.BlockSpec((tm, tn), functools.partial(
            lambda i, j, lo, hi: (i, jnp.clip(j - lo, 0, hi - lo - 1)), lo=lo_, hi=hi_))
        for lo_, hi_, _ in bounds]
    out_shape = [jax.ShapeDtypeStruct((t, n_cols), dt) for n_cols, _, dt in segs]
    return pl.pallas_call(
        functools.partial(_proj_kernel, bounds, len(extra)),
        grid=(t // tm, n // tn),
        in_specs=[pl.BlockSpec((tm, k), lambda i, j: (i, 0)),
                  pl.BlockSpec((None, k, tn), lambda i, j: (j, 0, 0))] + list(extra_specs),
        out_specs=out_specs,
        out_shape=out_shape,
        compiler_params=_cparams("parallel", "arbitrary"),
        name=name,
    )(h, _column_tiles(w, tn), *extra)


def _ln_mod_store(y, x_ref, gate_ref, lnw_ref, lnb_ref, sc_ref, sh_ref, xo_ref, ho_ref, rows=slice(None)):
    z = DN_ALPHA * x_ref[rows, :] + (1.0 + gate_ref[...]) * y
    mu = jnp.mean(z, axis=-1, keepdims=True)
    zc = z - mu
    var = jnp.mean(zc * zc, axis=-1, keepdims=True)
    xn = zc * lax.rsqrt(var + LN_EPS) * lnw_ref[...] + lnb_ref[...]
    xo_ref[rows, :] = xn
    if ho_ref is not None:
        ho_ref[rows, :] = (xn * (1.0 + sc_ref[...]) + sh_ref[...]).astype(BF16)


OUT_LN_CHUNKS = 2


def _out_ln_kernel(n_a, has_next, *refs):
    a_refs = refs[:n_a]
    w_ref, x_ref, gate_ref, lnw_ref, lnb_ref = refs[n_a:n_a + 5]
    rest = refs[n_a + 5:]
    if has_next:
        sc_ref, sh_ref, xo_ref, ho_ref = rest
    else:
        (xo_ref,), sc_ref, sh_ref, ho_ref = rest, None, None, None
    rc = x_ref.shape[0] // OUT_LN_CHUNKS
    for c in range(OUT_LN_CHUNKS):
        rows = slice(c * rc, (c + 1) * rc)
        a = a_refs[0][rows, :]
        for r in a_refs[1:]:
            a = a.astype(F32) + r[rows, :].astype(F32)
        y = _dot(a.astype(BF16), w_ref[...])
        _ln_mod_store(y, x_ref, gate_ref, lnw_ref, lnb_ref, sc_ref, sh_ref, xo_ref, ho_ref, rows)


def _out_ln(a_list, w, x, mods, sub, sub_next, ln_w, ln_b, seq, tm=512):
    t, d = x.shape
    k = w.shape[0]
    tpb = seq // tm
    has_next = sub_next is not None
    in_specs = [pl.BlockSpec((tm, k), lambda i: (i, 0)) for _ in a_list]
    in_specs += [pl.BlockSpec((k, d), lambda i: (0, 0), pipeline_mode=pl.Buffered(1)),
                 pl.BlockSpec((tm, d), lambda i: (i, 0)),
                 _mod_spec(d, sub, GATE, tpb), _row_spec(d), _row_spec(d)]
    args = list(a_list) + [w, x, mods, ln_w.reshape(1, d), ln_b.reshape(1, d)]
    out_specs = [pl.BlockSpec((tm, d), lambda i: (i, 0))]
    out_shape = [jax.ShapeDtypeStruct((t, d), F32)]
    if has_next:
        in_specs += [_mod_spec(d, sub_next, SCALE, tpb), _mod_spec(d, sub_next, SHIFT, tpb)]
        args += [mods, mods]
        out_specs.append(pl.BlockSpec((tm, d), lambda i: (i, 0)))
        out_shape.append(jax.ShapeDtypeStruct((t, d), BF16))
    res = pl.pallas_call(
        functools.partial(_out_ln_kernel, len(a_list), has_next),
        grid=(t // tm,),
        in_specs=in_specs, out_specs=out_specs, out_shape=out_shape,
        compiler_params=_cparams("parallel"),
        name="out_ln",
    )(*args)
    return (res[0], res[1]) if has_next else (res[0], None)


MLP_EPILOGUE_CHUNKS = 2


def _mlp_kernel(has_next, h_ref, w1_ref, w2_ref, x_ref, gate_ref, lnw_ref, lnb_ref, *rest):
    if has_next:
        sc_ref, sh_ref, xo_ref, ho_ref, acc_ref = rest
    else:
        (xo_ref, acc_ref), sc_ref, sh_ref, ho_ref = rest, None, None, None
    j = pl.program_id(1)
    last = pl.num_programs(1) - 1
    tm = h_ref.shape[0]

    def partial_out(rows):
        hid = _dot(h_ref[rows, :], w1_ref[...])
        hid = jnp.square(jnp.maximum(hid, 0.0)).astype(BF16)
        return _dot(hid, w2_ref[...])

    @pl.when(j == 0)
    def _():
        acc_ref[...] = partial_out(slice(None))

    @pl.when(jnp.logical_and(j > 0, j < last))
    def _():
        acc_ref[...] += partial_out(slice(None))

    @pl.when(j == last)
    def _():
        rc = tm // MLP_EPILOGUE_CHUNKS
        for c in range(MLP_EPILOGUE_CHUNKS):
            rows = slice(c * rc, (c + 1) * rc)
            y = acc_ref[rows, :] + partial_out(rows)
            _ln_mod_store(y, x_ref, gate_ref, lnw_ref, lnb_ref, sc_ref, sh_ref, xo_ref, ho_ref, rows)


def _mlp(h, w1, w2, x, mods, sub, sub_next, ln_w, ln_b, seq, tm=512, tf=1024):
    t, d = x.shape
    ff = w1.shape[1]
    assert ff // tf >= 2
    tpb = seq // tm
    has_next = sub_next is not None
    in_specs = [pl.BlockSpec((tm, d), lambda i, j: (i, 0)),
                pl.BlockSpec((None, d, tf), lambda i, j: (j, 0, 0)),
                pl.BlockSpec((tf, d), lambda i, j: (j, 0)),
                pl.BlockSpec((tm, d), lambda i, j: (i, 0)),
                _mod_spec(d, sub, GATE, tpb), _row_spec(d), _row_spec(d)]
    args = [h, _column_tiles(w1, tf), w2, x, mods, ln_w.reshape(1, d), ln_b.reshape(1, d)]
    out_specs = [pl.BlockSpec((tm, d), lambda i, j: (i, 0))]
    out_shape = [jax.ShapeDtypeStruct((t, d), F32)]
    if has_next:
        in_specs += [_mod_spec(d, sub_next, SCALE, tpb), _mod_spec(d, sub_next, SHIFT, tpb)]
        args += [mods, mods]
        out_specs.append(pl.BlockSpec((tm, d), lambda i, j: (i, 0)))
        out_shape.append(jax.ShapeDtypeStruct((t, d), BF16))
    res = pl.pallas_call(
        functools.partial(_mlp_kernel, has_next),
        grid=(t // tm, ff // tf),
        in_specs=in_specs, out_specs=out_specs, out_shape=out_shape,
        scratch_shapes=[pltpu.VMEM((tm, d), F32)],
        compiler_params=_cparams("parallel", "arbitrary"),
        name="mlp",
    )(*args)
    return (res[0], res[1]) if has_next else (res[0], None)


HALO = 8


def _lru_scan_kernel(ts, gy_ref, xb_ref, cw_ref, cb_ref, gw_ref, gb_ref, lam_ref, o_ref,
                     buf, a_scr, u_scr, h_scr):
    i = pl.program_id(1)
    width = xb_ref.shape[-1]
    bd = width // LRU_BLOCKS

    @pl.when(i == 0)
    def _():
        buf[0:HALO, :] = jnp.zeros((HALO, width), F32)
        h_scr[...] = jnp.zeros_like(h_scr)

    @pl.when(i > 0)
    def _():
        buf[0:HALO, :] = buf[ts:ts + HALO, :]

    buf[HALO:HALO + ts, :] = xb_ref[...]
    cw = cw_ref[...]
    taps = [buf[HALO - (CONV_WIDTH - 1) + k:HALO - (CONV_WIDTH - 1) + k + ts, :] * cw[k:k + 1, :]
            for k in range(CONV_WIDTH)]
    conv = taps[0]
    for tap in taps[1:]:
        conv = conv + tap
    xc = cb_ref[...] + conv
    xcb = xc.astype(BF16)
    gates = []
    for jg in range(2):
        cols = [_dot(xcb[:, n * bd:(n + 1) * bd], gw_ref[jg, n]) for n in range(LRU_BLOCKS)]
        gates.append(jnp.concatenate(cols, axis=1) + gb_ref[jg:jg + 1, :])
    r = 0.5 * (1.0 + jnp.tanh(0.5 * gates[0]))
    ig = 0.5 * (1.0 + jnp.tanh(0.5 * gates[1]))
    neg_lam = -lam_ref[...]
    softplus = jnp.maximum(neg_lam, 0.0) + jnp.log1p(jnp.exp(-jnp.abs(neg_lam)))
    log_a = -LRU_C * r * softplus
    a = jnp.exp(log_a)
    a_scr[...] = a
    one_m_a2 = -jnp.tanh(log_a) * (1.0 + a * a)
    root = one_m_a2 * lax.rsqrt(jnp.maximum(one_m_a2, F32_TINY))
    u_scr[...] = root * (ig * xc)

    def step(t, h):
        h = a_scr[pl.ds(t, 1), :] * h + u_scr[pl.ds(t, 1), :]
        u_scr[pl.ds(t, 1), :] = h
        return h

    h_scr[...] = lax.fori_loop(0, ts, step, h_scr[...], unroll=8)
    o_ref[...] = (gy_ref[...] * u_scr[...]).astype(BF16)


def _lru_scan(gy, xb, conv_w, conv_b, gate_w, gate_b, lam, ts=512):
    bsz, seq, width = xb.shape
    bd = width // LRU_BLOCKS
    tile = pl.BlockSpec((None, ts, width), lambda b, i: (b, i, 0))
    return pl.pallas_call(
        functools.partial(_lru_scan_kernel, ts),
        grid=(bsz, seq // ts),
        in_specs=[tile, tile,
                  pl.BlockSpec((CONV_WIDTH, width), lambda b, i: (0, 0)),
                  pl.BlockSpec((1, width), lambda b, i: (0, 0)),
                  pl.BlockSpec((2, LRU_BLOCKS, bd, bd), lambda b, i: (0, 0, 0, 0)),
                  pl.BlockSpec((2, width), lambda b, i: (0, 0)),
                  pl.BlockSpec((1, width), lambda b, i: (0, 0))],
        out_specs=tile,
        out_shape=jax.ShapeDtypeStruct((bsz, seq, width), BF16),
        scratch_shapes=[pltpu.VMEM((ts + HALO, width), F32), pltpu.VMEM((ts, width), F32),
                        pltpu.VMEM((ts, width), F32), pltpu.VMEM((1, width), F32)],
        compiler_params=_cparams("parallel", "arbitrary"),
        name="lru_scan",
    )(gy, xb, conv_w, conv_b.reshape(1, width), gate_w.astype(BF16), gate_b, lam.reshape(1, width))


def _lru_mixer(h, bsz, seq, w_in, conv_w, conv_b, gate_w, gate_b, lam):
    width = w_in.shape[1] // 2
    gy, xb = _proj(h, w_in.astype(BF16),
                   [(width, lambda acc: _gelu_tanh(acc), BF16), (width, lambda acc: acc, F32)],
                   tn=1024, name="lru_proj")
    a = _lru_scan(gy.reshape(bsz, seq, width), xb.reshape(bsz, seq, width),
                  conv_w, conv_b, gate_w, gate_b, lam)
    return [a.reshape(bsz * seq, width)]


def _t5_bucket_np(dist):
    n = np.maximum(np.asarray(dist, np.int64), 0)
    max_exact = T5_BUCKETS // 2
    n_large = T5_BUCKETS - max_exact
    ratio = T5_MAX_DIST // max_exact
    thresholds = []
    for k in range(1, n_large):
        m = max_exact
        while m ** n_large < ratio ** k * max_exact ** n_large:
            m += 1
        thresholds.append(m)
    large = max_exact + sum((n >= th).astype(np.int64) for th in thresholds)
    return np.where(n < max_exact, n, large).astype(np.int32)


def _bias_table_kernel(sub_far, rb_ref, idx_ref, o_ref):
    hd = pl.program_id(0)
    idx = idx_ref[...]
    base = rb_ref[T5_BUCKETS - 1, hd] if sub_far else 0.0
    acc = jnp.full(idx.shape, NEG_MASK, F32)
    for b in range(T5_BUCKETS):
        acc = jnp.where(idx == b, (rb_ref[b, hd] - base) * LOG2E, acc)
    o_ref[...] = acc


def _bias_table(rel_bias, idx_np, sub_far, name):
    rows, cols = idx_np.shape
    tr = 128
    n_heads = rel_bias.shape[1]
    return pl.pallas_call(
        functools.partial(_bias_table_kernel, sub_far),
        grid=(n_heads, rows // tr),
        in_specs=[pl.BlockSpec(memory_space=pltpu.SMEM),
                  pl.BlockSpec((tr, cols), lambda hd, i: (i, 0))],
        out_specs=pl.BlockSpec((None, tr, cols), lambda hd, i: (hd, i, 0)),
        out_shape=jax.ShapeDtypeStruct((n_heads, rows, cols), F32),
        compiler_params=_cparams("parallel", "parallel"),
        name=name,
    )(rel_bias, jnp.asarray(idx_np))


def _compress_kernel(seq, x_ref, pe_ref, w1_ref, b1_ref, w2_ref, o_ref, xs):
    xs[0:seq, :] = x_ref[...]
    xs[seq:CMP_ROWS, :] = jnp.zeros((CMP_ROWS - seq, LANES), F32)
    acc = jnp.zeros((N_CMP_SLOTS, LANES), F32)
    for l in range(CMP_BLOCK):
        rows = xs[pl.ds(l, N_CMP_SLOTS, stride=CMP_STRIDE), :] + pe_ref[l:l + 1, :]
        acc = acc + _dot(rows.astype(BF16), w1_ref[l])
    hid = _gelu_tanh(acc + b1_ref[...])
    o_ref[...] = _dot(hid.astype(BF16), w2_ref[...]).astype(BF16)


def _compress(cmp_tok, pe, w1, b1, w2):
    bsz, seq, _ = cmp_tok.shape
    g = NSA_KV_GROUPS
    dh = NSA_HEAD_DIM
    return pl.pallas_call(
        functools.partial(_compress_kernel, seq),
        grid=(bsz, 2, g),
        in_specs=[pl.BlockSpec((None, seq, dh), lambda b, j, gi: (b, 0, j * g + gi)),
                  pl.BlockSpec((None, CMP_BLOCK, dh), lambda b, j, gi: (j, 0, 0)),
                  pl.BlockSpec((None, CMP_BLOCK, dh, dh), lambda b, j, gi: (j, 0, 0, 0)),
                  pl.BlockSpec((None, 1, dh), lambda b, j, gi: (j, 0, 0)),
                  pl.BlockSpec((None, dh, dh), lambda b, j, gi: (j, 0, 0))],
        out_specs=pl.BlockSpec((None, None, None, N_CMP_SLOTS, dh), lambda b, j, gi: (b, j, gi, 0, 0)),
        out_shape=jax.ShapeDtypeStruct((bsz, 2, g, N_CMP_SLOTS, dh), BF16),
        scratch_shapes=[pltpu.VMEM((CMP_ROWS, dh), F32)],
        compiler_params=_cparams("parallel", "parallel", "parallel"),
        name="nsa_compress",
    )(cmp_tok, pe, w1.astype(BF16), b1.reshape(2, 1, dh), w2.astype(BF16))


CMP_COLS = N_CMP_SLOTS // LANES
CMP_COL_TOKENS = CMP_STRIDE * LANES


def _cmp_table_idx(tq):
    ql = np.arange(tq)[:, None]
    m = np.arange(N_CMP_SLOTS)[None, :]
    e = np.where(m <= (tq - CMP_BLOCK) // CMP_STRIDE, -m, N_CMP_SLOTS - m)
    dist = CMP_STRIDE * e + ql - (CMP_BLOCK - 1)
    return _t5_bucket_np(dist)


def _sel_sum_matrix():
    i = np.arange(N_CMP_SLOTS)[:, None]
    j = np.arange(LANES)[None, :]
    ratio = SEL_BLOCK // CMP_STRIDE
    return ((i >= ratio * j - 1) & (i <= ratio * j + ratio - 1)).astype(np.float32)


CMP_SUB = 2


def _cmp_attn_kernel(tq, q_ref, kc_ref, vc_ref, tb_ref, sm_ref, g_ref, o_ref, val_ref):
    i = pl.program_id(2)
    lane = lax.broadcasted_iota(jnp.int32, (tq, LANES), 1)
    row = lax.broadcasted_iota(jnp.int32, (tq, LANES), 0)

    def body(nc):
        width = nc * LANES
        kc = kc_ref[0:width, :]
        vc = vc_ref[0:width, :]
        ql = lax.broadcasted_iota(jnp.int32, (tq, width), 0)
        blk = lax.broadcasted_iota(jnp.int32, (tq, width), 1)
        rel = ql - CMP_STRIDE * blk - (CMP_BLOCK - 1)
        for sub in range(CMP_SUB):
            rows = slice(sub * tq, (sub + 1) * tq)
            tile = i * CMP_SUB + sub
            t0 = tile * tq
            gate = g_ref[rows, :]
            causal = jnp.where(t0 + rel >= 0, 0.0, NEG_MASK)
            row_ok = (t0 + lax.broadcasted_iota(jnp.int32, (tq, 1), 0)) >= CMP_BLOCK - 1
            shift = (tile * (tq // CMP_STRIDE)) % LANES
            keep = lane >= shift
            p_sum = jnp.zeros((tq, width), F32)
            for hd in range(NSA_HPG):
                rot = {}
                for c in range(nc):
                    for k in ((c - (nc - 1)) % CMP_COLS, (c - nc) % CMP_COLS):
                        if k not in rot:
                            rot[k] = pltpu.roll(tb_ref[hd, :, k * LANES:(k + 1) * LANES], shift, axis=1)
                tb = jnp.concatenate(
                    [jnp.where(keep, rot[(c - (nc - 1)) % CMP_COLS], rot[(c - nc) % CMP_COLS])
                     for c in range(nc)], axis=1)
                s = _dot_nt(q_ref[rows, hd * LANES:(hd + 1) * LANES], kc) + tb + causal
                e = jnp.exp2(s - jnp.max(s, axis=-1, keepdims=True))
                p = e * jnp.where(row_ok, 1.0 / jnp.sum(e, axis=-1, keepdims=True), 0.0)
                p_sum = p_sum + p
                o_ref[rows, hd * LANES:(hd + 1) * LANES] = (
                    _dot(p.astype(BF16), vc) * gate[:, hd:hd + 1]).astype(o_ref.dtype)
            sm = sm_ref[0:width, :]
            hi = p_sum.astype(BF16)
            r1 = p_sum - hi.astype(F32)
            mid = r1.astype(BF16)
            lo = (r1 - mid.astype(F32)).astype(BF16)
            score = (_dot(hi, sm) + _dot(mid, sm)) + _dot(lo, sm)
            cur = (t0 + row) >> 6
            forced = jnp.logical_or(lane == 0, jnp.logical_or(lane == cur, lane == cur - 1))
            val_ref[rows, :] = jnp.where(forced, SEL_FORCE, jnp.where(lane <= cur, score, -SEL_FORCE))

    last_q = (i + 1) * (CMP_SUB * tq) - 1
    n_cols = jnp.minimum((last_q - (CMP_BLOCK - 1)) // CMP_COL_TOKENS + 1, CMP_COLS)
    for nc in range(1, CMP_COLS + 1):
        pl.when(n_cols == nc)(functools.partial(body, nc))


N_FORCED = 3


def _topk_mask_kernel(n_top, val_ref, m_ref):
    val0 = val_ref[...]
    lane = lax.broadcasted_iota(jnp.int32, val0.shape, 1).astype(F32)
    forced = val0 >= 0.5 * SEL_FORCE
    val = jnp.where(forced, -jnp.inf, val0)
    sel = jnp.where(forced, 1.0, 0.0)
    for _ in range(n_top - N_FORCED):
        mx = jnp.max(val, axis=-1, keepdims=True)
        first = jnp.min(jnp.where(val == mx, lane, float(LANES)), axis=-1, keepdims=True)
        pick = lane == first
        sel = jnp.where(pick, 1.0, sel)
        val = jnp.where(pick, -jnp.inf, val)
    m_ref[...] = jnp.where(jnp.logical_and(sel > 0.0, val0 >= 0.0), 0.0, NEG_MASK).astype(BF16)


def _topk_mask(val, n_top, tr=1024):
    rows = val.shape[0]
    return pl.pallas_call(
        functools.partial(_topk_mask_kernel, n_top),
        grid=(rows // tr,),
        in_specs=[pl.BlockSpec((tr, LANES), lambda i: (i, 0))],
        out_specs=pl.BlockSpec((tr, LANES), lambda i: (i, 0)),
        out_shape=jax.ShapeDtypeStruct((rows, LANES), BF16),
        compiler_params=_cparams("parallel"),
        name="nsa_topk_mask",
    )(val)


def _cmp_attn(q, kvc, table, gates, tq=256):
    bsz, seq, _ = q.shape
    g = NSA_KV_GROUPS
    qw = NSA_HPG * NSA_HEAD_DIM
    rows = tq * CMP_SUB
    assert CMP_COL_TOKENS % rows == 0 and tq % CMP_STRIDE == 0
    return pl.pallas_call(
        functools.partial(_cmp_attn_kernel, tq),
        grid=(bsz, g, seq // rows),
        in_specs=[pl.BlockSpec((None, rows, qw), lambda b, gi, i: (b, i, gi)),
                  pl.BlockSpec((None, None, None, N_CMP_SLOTS, NSA_HEAD_DIM), lambda b, gi, i: (b, 0, gi, 0, 0)),
                  pl.BlockSpec((None, None, None, N_CMP_SLOTS, NSA_HEAD_DIM), lambda b, gi, i: (b, 1, gi, 0, 0)),
                  pl.BlockSpec((NSA_HPG, tq, N_CMP_SLOTS), lambda b, gi, i: (gi, 0, 0)),
                  pl.BlockSpec((N_CMP_SLOTS, LANES), lambda b, gi, i: (0, 0)),
                  pl.BlockSpec((None, rows, LANES), lambda b, gi, i: (b, i, gi))],
        out_specs=[pl.BlockSpec((None, rows, qw), lambda b, gi, i: (b, i, gi)),
                   pl.BlockSpec((None, None, rows, LANES), lambda b, gi, i: (b, gi, i, 0))],
        out_shape=[jax.ShapeDtypeStruct((bsz, seq, g * qw), BF16),
                   jax.ShapeDtypeStruct((bsz, g, seq, LANES), F32)],
        compiler_params=_cparams("parallel", "parallel", "parallel"),
        name="nsa_cmp_attn",
    )(q, kvc, kvc, table, jnp.asarray(_sel_sum_matrix(), BF16), gates)


SEL_TQ = 256
SEL_TK = 256
SEL_N_DELTA = -(-(T5_MAX_DIST - 1 + SEL_TK) // SEL_TQ)
SEL_N_NEAR = -(-SEL_N_DELTA // (SEL_TK // SEL_TQ))
SEL_STREAMS = 4


def _sel_table_idx():
    d = np.arange(SEL_N_DELTA)[:, None, None]
    ql = np.arange(SEL_TQ)[None, :, None]
    kl = np.arange(SEL_TK)[None, None, :]
    dist = ql - kl + SEL_TQ * d
    idx = np.where(dist >= 0, _t5_bucket_np(dist), MASK_BUCKET)
    return idx.reshape(SEL_N_DELTA * SEL_TQ, SEL_TK).astype(np.int32)


def _sel_attn_kernel(q_ref, m_ref, k_ref, v_ref, oh_ref, nb_ref, g_ref, o_ref, qa, m_scr, acc):
    tq, tk = SEL_TQ, SEL_TK
    ratio = tk // tq
    i = pl.program_id(2)
    kd = i // ratio
    sel_mask = m_ref[...]
    for hd in range(NSA_HPG):
        qa[hd * tq:(hd + 1) * tq, 0:LANES] = q_ref[:, hd * LANES:(hd + 1) * LANES]
        qa[hd * tq:(hd + 1) * tq, LANES:2 * LANES] = sel_mask
    m_scr[...] = jnp.full(m_scr.shape, -jnp.inf, F32)
    acc[...] = jnp.zeros_like(acc)
    n_kt = k_ref.shape[0] // tk
    ones = jnp.ones((tk, LANES), BF16)

    def update(st, tiles):
        scores, values = [], []
        for kt, valid, delta in tiles:
            k0 = pl.multiple_of(jnp.clip(kt, 0, n_kt - 1) * tk, tk)
            ka = jnp.concatenate([k_ref[pl.ds(k0, tk), :], oh_ref[pl.ds(k0, tk), :]], axis=1)
            s = _dot_nt(qa[...], ka)
            if valid is not None:
                s = s + jnp.where(valid, 0.0, NEG_MASK)
            if delta is not None:
                s = s + jnp.concatenate([nb_ref[hd, delta] for hd in range(NSA_HPG)], axis=0)
            scores.append(s)
            values.append(jnp.concatenate([v_ref[pl.ds(k0, tk), :], ones], axis=1))
        s_max = scores[0]
        for s in scores[1:]:
            s_max = jnp.maximum(s_max, s)
        m_prev = m_scr[st]
        m_new = jnp.maximum(m_prev, jnp.max(s_max, axis=-1, keepdims=True))
        alpha = jnp.exp2(m_prev - m_new)
        m_rep = jnp.concatenate([m_new] * (tk // LANES), axis=1)
        pv = None
        for s, va in zip(scores, values):
            part = _dot(jnp.exp2(s - m_rep).astype(BF16), va)
            pv = part if pv is None else pv + part
        acc[st] = jnp.concatenate([alpha, alpha], axis=1) * acc[st] + pv
        m_scr[st] = m_new

    n_far = jnp.maximum(kd - (SEL_N_NEAR - 1), 0)
    pair = 2 * SEL_STREAMS

    def far_pairs(it, carry):
        for st in range(SEL_STREAMS):
            kt = it * pair + 2 * st
            update(st, [(kt, None, None), (kt + 1, None, None)])
        return carry

    lax.fori_loop(0, n_far // pair, far_pairs, 0)
    base = (n_far // pair) * pair
    rem = n_far - base

    @pl.when(rem >= SEL_STREAMS)
    def _():
        for st in range(SEL_STREAMS):
            update(st, [(base + st, None, None)])

    base = base + jnp.where(rem >= SEL_STREAMS, SEL_STREAMS, 0)
    left = n_far - base
    slots = [(base + f, f < left, None) for f in range(SEL_STREAMS - 1)]
    for back in range(SEL_N_NEAR - 1, -1, -1):
        kt = kd - back
        slots.append((kt, kt >= 0, (i - kd * ratio) + back * ratio))
    for st in range(SEL_STREAMS):
        update(st, slots[st::SEL_STREAMS])

    m_all = m_scr[0]
    for st in range(1, SEL_STREAMS):
        m_all = jnp.maximum(m_all, m_scr[st])
    tot = jnp.zeros((NSA_HPG * tq, 2 * LANES), F32)
    for st in range(SEL_STREAMS):
        w = jnp.exp2(m_scr[st] - m_all)
        tot = tot + jnp.concatenate([w, w], axis=1) * acc[st]
    gate = g_ref[...]
    out = tot[:, :LANES] / jnp.maximum(tot[:, LANES:], 1e-30)
    for hd in range(NSA_HPG):
        o_ref[:, hd * LANES:(hd + 1) * LANES] = (
            out[hd * tq:(hd + 1) * tq, :] * gate[:, NSA_HPG + hd:NSA_HPG + hd + 1]).astype(o_ref.dtype)


def _sel_attn(q, sel_mask, kv_tok, onehot, table, gates):
    bsz, seq, _ = q.shape
    g = NSA_KV_GROUPS
    dh = NSA_HEAD_DIM
    qw = NSA_HPG * dh
    tq, tk = SEL_TQ, SEL_TK
    rows = NSA_HPG * tq
    return pl.pallas_call(
        _sel_attn_kernel,
        grid=(bsz, g, seq // tq),
        in_specs=[pl.BlockSpec((None, tq, qw), lambda b, gi, i: (b, i, gi)),
                  pl.BlockSpec((None, None, tq, LANES), lambda b, gi, i: (b, gi, i, 0)),
                  pl.BlockSpec((None, seq, dh), lambda b, gi, i: (b, 0, gi)),
                  pl.BlockSpec((None, seq, dh), lambda b, gi, i: (b, 0, g + gi)),
                  pl.BlockSpec((seq, LANES), lambda b, gi, i: (0, 0)),
                  pl.BlockSpec((NSA_HPG, SEL_N_DELTA, tq, tk), lambda b, gi, i: (gi, 0, 0, 0)),
                  pl.BlockSpec((None, tq, LANES), lambda b, gi, i: (b, i, gi))],
        out_specs=pl.BlockSpec((None, tq, qw), lambda b, gi, i: (b, i, gi)),
        out_shape=jax.ShapeDtypeStruct((bsz, seq, g * qw), BF16),
        scratch_shapes=[pltpu.VMEM((rows, 2 * LANES), BF16), pltpu.VMEM((SEL_STREAMS, rows, LANES), F32),
                        pltpu.VMEM((SEL_STREAMS, rows, 2 * dh), F32)],
        compiler_params=_cparams("parallel", "parallel", "parallel"),
        name="nsa_sel_attn",
    )(q, sel_mask, kv_tok, kv_tok, onehot, table, gates)


WIN_TQ = 128
WIN_SUB = 4
WIN_KEYS = WINDOW + WIN_TQ


def _win_table_idx():
    ql = np.arange(WIN_TQ)[:, None]
    kl = np.arange(WIN_KEYS)[None, :]
    dist = ql + WINDOW - kl
    return np.where((dist >= 0) & (dist < WINDOW), _t5_bucket_np(dist), MASK_BUCKET).astype(np.int32)


def _win_attn_kernel(seq, q_ref, k_ref, v_ref, wb_ref, g_ref, o_ref, kpad, vpad, qs):
    tq = WIN_TQ
    i = pl.program_id(2)

    @pl.when(i == 0)
    def _():
        kpad[0:WINDOW, :] = jnp.zeros((WINDOW, LANES), BF16)
        vpad[0:WINDOW, :] = jnp.zeros((WINDOW, LANES), BF16)
        kpad[WINDOW:WINDOW + seq, :] = k_ref[...]
        vpad[WINDOW:WINDOW + seq, :] = v_ref[...]

    bias = wb_ref[...].reshape(NSA_HPG * tq, WIN_KEYS)

    def step(header_visible):
        for sub in range(WIN_SUB):
            rows = slice(sub * tq, (sub + 1) * tq)
            t0 = pl.multiple_of((i * WIN_SUB + sub) * tq, tq)
            for hd in range(NSA_HPG):
                qs[sub, hd * tq:(hd + 1) * tq, :] = q_ref[rows, hd * LANES:(hd + 1) * LANES]
            keys = kpad[pl.ds(t0, WIN_KEYS), :]
            vals = vpad[pl.ds(t0, WIN_KEYS), :]
            s = _dot_nt(qs[sub], keys) + bias
            if header_visible:
                kl = lax.broadcasted_iota(jnp.int32, s.shape, 1)
                s = jnp.where(kl >= WINDOW - t0, s, NEG_MASK)
            e = jnp.exp2(s - jnp.max(s, axis=-1, keepdims=True))
            out = _dot(e.astype(BF16), vals) * (1.0 / jnp.sum(e, axis=-1, keepdims=True))
            gate = g_ref[rows, :]
            for hd in range(NSA_HPG):
                o_ref[rows, hd * LANES:(hd + 1) * LANES] = (
                    out[hd * tq:(hd + 1) * tq, :]
                    * gate[:, 2 * NSA_HPG + hd:2 * NSA_HPG + hd + 1]).astype(o_ref.dtype)

    n_header_steps = -(-WINDOW // (WIN_SUB * tq))
    pl.when(i < n_header_steps)(functools.partial(step, True))
    pl.when(i >= n_header_steps)(functools.partial(step, False))


def _win_attn(q, kv_tok, table, gates):
    bsz, seq, _ = q.shape
    g = NSA_KV_GROUPS
    dh = NSA_HEAD_DIM
    qw = NSA_HPG * dh
    tq = WIN_TQ
    rows = tq * WIN_SUB
    return pl.pallas_call(
        functools.partial(_win_attn_kernel, seq),
        grid=(bsz, g, seq // rows),
        in_specs=[pl.BlockSpec((None, rows, qw), lambda b, gi, i: (b, i, gi)),
                  pl.BlockSpec((None, seq, dh), lambda b, gi, i: (b, 0, 2 * g + gi)),
                  pl.BlockSpec((None, seq, dh), lambda b, gi, i: (b, 0, 3 * g + gi)),
                  pl.BlockSpec((NSA_HPG, tq, WIN_KEYS), lambda b, gi, i: (gi, 0, 0)),
                  pl.BlockSpec((None, rows, LANES), lambda b, gi, i: (b, i, gi))],
        out_specs=pl.BlockSpec((None, rows, qw), lambda b, gi, i: (b, i, gi)),
        out_shape=jax.ShapeDtypeStruct((bsz, seq, g * qw), BF16),
        scratch_shapes=[pltpu.VMEM((seq + WINDOW, dh), BF16), pltpu.VMEM((seq + WINDOW, dh), BF16),
                        pltpu.VMEM((WIN_SUB, NSA_HPG * tq, dh), BF16)],
        compiler_params=_cparams("parallel", "parallel", "arbitrary"),
        name="nsa_win_attn",
    )(q, kv_tok, kv_tok, table, gates)


def _nsa_mixer(h, bsz, seq, rel_bias, w_in, cmp_pe, cmp_w1, cmp_b1, cmp_w2):
    g, hpg, dh = NSA_KV_GROUPS, NSA_HPG, NSA_HEAD_DIM
    qw = NSA_HEADS * dh
    kvw = g * dh
    n_main = qw + 2 * N_BRANCH * kvw
    scale = dh ** -0.5 * LOG2E
    q, cmp_tok, kv_tok = _proj(
        h, w_in[:, :n_main].astype(BF16),
        [(qw, lambda acc: acc * scale, BF16), (2 * kvw, lambda acc: acc, F32), (4 * kvw, lambda acc: acc, BF16)],
        tn=1024, name="nsa_proj")
    src = np.zeros((g * LANES,), np.int32)
    valid = np.zeros((g * LANES,), bool)
    for gi in range(g):
        for br in range(N_BRANCH):
            for hd in range(hpg):
                src[gi * LANES + br * hpg + hd] = n_main + br * NSA_HEADS + gi * hpg + hd
                valid[gi * LANES + br * hpg + hd] = True
    w_gate = jnp.where(jnp.asarray(valid)[None, :], w_in[:, src], 0.0).astype(BF16)
    (gates,) = _proj(h, w_gate, [(g * LANES, lambda acc: _sigmoid(acc), F32)], name="nsa_gate_proj")

    q = q.reshape(bsz, seq, qw)
    gates = gates.reshape(bsz, seq, g * LANES)
    kv_tok = kv_tok.reshape(bsz, seq, 4 * kvw)
    kvc = _compress(cmp_tok.reshape(bsz, seq, 2 * kvw), cmp_pe, cmp_w1, cmp_b1, cmp_w2)

    n_top = min(SEL_TOP_N, seq // SEL_BLOCK)
    cmp_tq = 256
    cmp_table = _bias_table(rel_bias, _cmp_table_idx(cmp_tq), False, "nsa_cmp_bias")
    o_c, sel_val = _cmp_attn(q, kvc, cmp_table, gates, cmp_tq)
    sel_mask = _topk_mask(sel_val.reshape(bsz * g * seq, LANES), n_top).reshape(bsz, g, seq, LANES)

    sel_table = _bias_table(rel_bias, _sel_table_idx(), True, "nsa_sel_bias")
    sel_table = sel_table.reshape(NSA_HEADS, SEL_N_DELTA, SEL_TQ, SEL_TK)
    onehot = (jnp.arange(seq, dtype=jnp.int32)[:, None] // SEL_BLOCK
              == jnp.arange(LANES, dtype=jnp.int32)[None, :]).astype(BF16)
    o_s = _sel_attn(q, sel_mask, kv_tok, onehot, sel_table, gates)

    win_table = _bias_table(rel_bias, _win_table_idx(), False, "nsa_win_bias")
    o_w = _win_attn(q, kv_tok, win_table, gates)
    t = bsz * seq
    return [o_c.reshape(t, qw), o_s.reshape(t, qw), o_w.reshape(t, qw)]


def _rotary_epi(scale, acc, cos, sin):
    half = cos.shape[-1]
    outs = []
    for hd in range(acc.shape[1] // (2 * half)):
        x1 = acc[:, 2 * hd * half:(2 * hd + 1) * half]
        x2 = acc[:, (2 * hd + 1) * half:(2 * hd + 2) * half]
        outs += [x1 * cos - x2 * sin, x1 * sin + x2 * cos]
    rot = jnp.concatenate(outs, axis=1)
    return rot * scale if scale != 1.0 else rot


RET_HPS = 2


def _ret_kernel(tc, q_ref, k_ref, v_ref, g_ref, dec_ref, xi_ref, zeta_ref, cd_ref, gw_ref, gb_ref,
                o_ref, state):
    i = pl.program_id(2)
    dk = q_ref.shape[-1] // RET_HPS
    dv = v_ref.shape[-1] // RET_HPS

    @pl.when(i == 0)
    def _():
        state[...] = jnp.zeros_like(state)

    for c in range(tc // RET_CHUNK):
        rows = slice(c * RET_CHUNK, (c + 1) * RET_CHUNK)
        for hh in range(RET_HPS):
            qk_cols = slice(hh * dk, (hh + 1) * dk)
            v_cols = slice(hh * dv, (hh + 1) * dv)
            qc = q_ref[rows, qk_cols]
            kc = k_ref[rows, qk_cols]
            vc = v_ref[rows, v_cols]
            inner = _dot_nt(qc, kc) * dec_ref[hh]
            st = state[hh]
            out = _dot(inner.astype(BF16), vc) + _dot(qc, st.astype(BF16)) * xi_ref[hh]
            kz = (kc.astype(F32) * zeta_ref[hh]).astype(BF16)
            state[hh] = st * cd_ref[hh] + _dot_tn(kz, vc)
            mu = jnp.mean(out, axis=-1, keepdims=True)
            oc = out - mu
            var = jnp.mean(oc * oc, axis=-1, keepdims=True)
            y = (oc * lax.rsqrt(var + GN_EPS)) * gw_ref[:, v_cols] + gb_ref[:, v_cols]
            o_ref[rows, v_cols] = (g_ref[rows, v_cols] * y).astype(BF16)


def _ret_core(q, k, v, g, gn_w, gn_b, tc=512):
    bsz, seq, qk_total = q.shape
    v_total = v.shape[-1]
    nh = RET_HEADS
    dk = qk_total // nh
    dv = v_total // nh
    cs = RET_CHUNK
    hps = RET_HPS
    log_g = jnp.log1p(-jnp.exp2(-5.0 - jnp.arange(nh, dtype=F32)))
    idx = jnp.arange(cs, dtype=F32)
    diff = idx[:, None] - idx[None, :]
    decay = jnp.where(diff >= 0, jnp.exp(jnp.maximum(diff, 0.0) * log_g[:, None, None]), 0.0)
    xi = jnp.exp((idx + 1.0) * log_g[:, None])[:, :, None]
    zeta = jnp.exp((cs - 1.0 - idx) * log_g[:, None])[:, :, None]
    chunk_decay = jnp.exp(cs * log_g).reshape(nh, 1, 1)
    return pl.pallas_call(
        functools.partial(_ret_kernel, tc),
        grid=(bsz, nh // hps, seq // tc),
        in_specs=[pl.BlockSpec((None, tc, hps * dk), lambda b, hp, i: (b, i, hp)),
                  pl.BlockSpec((None, tc, hps * dk), lambda b, hp, i: (b, i, hp)),
                  pl.BlockSpec((None, tc, hps * dv), lambda b, hp, i: (b, i, hp)),
                  pl.BlockSpec((None, tc, hps * dv), lambda b, hp, i: (b, i, hp)),
                  pl.BlockSpec((hps, cs, cs), lambda b, hp, i: (hp, 0, 0)),
                  pl.BlockSpec((hps, cs, 1), lambda b, hp, i: (hp, 0, 0)),
                  pl.BlockSpec((hps, cs, 1), lambda b, hp, i: (hp, 0, 0)),
                  pl.BlockSpec((hps, 1, 1), lambda b, hp, i: (hp, 0, 0)),
                  pl.BlockSpec((1, hps * dv), lambda b, hp, i: (0, hp)),
                  pl.BlockSpec((1, hps * dv), lambda b, hp, i: (0, hp))],
        out_specs=pl.BlockSpec((None, tc, hps * dv), lambda b, hp, i: (b, i, hp)),
        out_shape=jax.ShapeDtypeStruct((bsz, seq, v_total), BF16),
        scratch_shapes=[pltpu.VMEM((hps, dk, dv), F32)],
        compiler_params=_cparams("parallel", "parallel", "arbitrary"),
        name="ret_core",
    )(q, k, v, g, decay, xi, zeta, chunk_decay, gn_w.reshape(1, v_total), gn_b.reshape(1, v_total))


def _ret_mixer(h, bsz, seq, w_in, gn_w, gn_b, tm=1024):
    qk_total = w_in.shape[0]
    v_total = (w_in.shape[1] - 2 * qk_total) // 2
    dk = qk_total // RET_HEADS
    half = dk // 2
    pos = jnp.arange(seq, dtype=F32)
    inv = ROPE_BASE ** (-jnp.arange(half, dtype=F32) / half)
    ang = pos[:, None] * inv[None, :]
    cos, sin = jnp.cos(ang), jnp.sin(ang)
    tpb = seq // tm
    rope_spec = pl.BlockSpec((tm, half), lambda i, j: (i % tpb, 0))
    q, k, v, g = _proj(
        h, w_in.astype(BF16),
        [(qk_total, functools.partial(_rotary_epi, 1.0), BF16),
         (qk_total, functools.partial(_rotary_epi, dk ** -0.5), BF16),
         (v_total, lambda acc, cos, sin: acc, BF16),
         (v_total, lambda acc, cos, sin: _silu(acc), BF16)],
        extra=(cos, sin), extra_specs=(rope_spec, rope_spec), tm=tm, tn=1024, name="ret_proj")
    a = _ret_core(q.reshape(bsz, seq, qk_total), k.reshape(bsz, seq, qk_total),
                  v.reshape(bsz, seq, v_total), g.reshape(bsz, seq, v_total), gn_w, gn_b)
    return [a.reshape(bsz * seq, v_total)]


def kernel(x, c, rel_bias, ada_w, ada_b, ln_w, ln_b, mlp_w1, mlp_w2, lru_w_in, lru_conv_w, lru_conv_b,
           lru_gate_w, lru_gate_b, lru_lambda, lru_w_out, nsa_w_in, nsa_cmp_pe, nsa_cmp_w1, nsa_cmp_b1,
           nsa_cmp_w2, nsa_w_out, ret_w_in, ret_gn_w, ret_gn_b, ret_w_out):
    bsz, seq, d = x.shape
    depth = ada_w.shape[0]
    mods = _ada_mods(c, ada_w, ada_b)
    xf = x.reshape(bsz * seq, d)
    h = _modulate(xf, mods, 0, seq)
    for layer in range(depth):
        mixer, inst = layer % N_MIXERS, layer // N_MIXERS
        sub = 2 * layer
        if mixer == 0:
            a_list = _lru_mixer(h, bsz, seq, lru_w_in[inst], lru_conv_w[inst], lru_conv_b[inst],
                                lru_gate_w[inst], lru_gate_b[inst], lru_lambda[inst])
            w_out = lru_w_out[inst]
        elif mixer == 1:
            a_list = _nsa_mixer(h, bsz, seq, rel_bias, nsa_w_in[inst], nsa_cmp_pe[inst], nsa_cmp_w1[inst],
                                nsa_cmp_b1[inst], nsa_cmp_w2[inst])
            w_out = nsa_w_out[inst]
        else:
            a_list = _ret_mixer(h, bsz, seq, ret_w_in[inst], ret_gn_w[inst], ret_gn_b[inst])
            w_out = ret_w_out[inst]
        xf, h = _out_ln(a_list, w_out.astype(BF16), xf, mods, sub, sub + 1, ln_w[layer, 0], ln_b[layer, 0], seq)
        sub_next = sub + 2 if layer + 1 < depth else None
        xf, h = _mlp(h, mlp_w1[layer].astype(BF16), mlp_w2[layer].astype(BF16), xf, mods, sub + 1, sub_next,
                     ln_w[layer, 1], ln_b[layer, 1], seq)
    return xf.reshape(bsz, seq, d)
```

```python
import functools
import math

import numpy as np
import jax
import jax.numpy as jnp
from jax import lax
from jax.experimental import pallas as pl
from jax.experimental.pallas import tpu as pltpu

F32 = jnp.float32
BF16 = jnp.bfloat16

DN_DEPTH = 4
N_MIXERS = 3
DN_ALPHA = (2 * DN_DEPTH) ** 0.25
LN_EPS = 1e-5
GN_EPS = 1e-5
LRU_BLOCKS = 8
CONV_WIDTH = 4
LRU_C = 8.0
NSA_HEADS = 16
NSA_HEAD_DIM = 128
NSA_KV_GROUPS = 4
NSA_HPG = NSA_HEADS // NSA_KV_GROUPS
N_BRANCH = 3
CMP_BLOCK = 32
CMP_STRIDE = 16
SEL_BLOCK = 64
SEL_TOP_N = 16
SEL_FORCE = 1e6
WINDOW = 512
NEG_BIG = 1e30
T5_BUCKETS = 32
T5_MAX_DIST = 1024
RET_HEADS = 8
RET_CHUNK = 128
ROPE_BASE = 10000.0

LANES = 128
VMEM_LIMIT_BYTES = 56 * 2 ** 20

NEG_MASK = -(2.0 ** 100)
MASK_BUCKET = T5_BUCKETS
F32_TINY = float(np.finfo(np.float32).tiny)
LOG2E = math.log2(math.e)

N_CMP_SLOTS = 4 * LANES
CMP_ROWS = CMP_STRIDE * N_CMP_SLOTS + 4 * CMP_STRIDE


def _cparams(*sem):
    return pltpu.CompilerParams(dimension_semantics=sem, vmem_limit_bytes=VMEM_LIMIT_BYTES)


def _sigmoid(x):
    return 1.0 / (1.0 + jnp.exp(-x))


def _silu(x):
    return x * _sigmoid(x)


def _gelu_tanh(x):
    return 0.5 * x * (1.0 + jnp.tanh(math.sqrt(2.0 / math.pi) * (x + 0.044715 * (x * x * x))))


def _dot(a, b):
    return jnp.dot(a, b, preferred_element_type=F32)


def _dot_nt(a, b):
    return lax.dot_general(a, b, (((1,), (1,)), ((), ())), preferred_element_type=F32)


def _dot_tn(a, b):
    return lax.dot_general(a, b, (((0,), (0,)), ((), ())), preferred_element_type=F32)


def _ada_kernel(c_ref, w_ref, b_ref, o_ref):
    cond = _silu(c_ref[...]).astype(BF16)
    o_ref[...] = _dot(cond, w_ref[...].astype(BF16)) + b_ref[...]


def _ada_mods(c, ada_w, ada_b):
    depth, _, d, d3 = ada_w.shape
    bsz = c.shape[0]
    n = depth * 2
    rows = 8
    c_pad = jnp.zeros((rows, d), F32).at[:bsz].set(c)
    tn = 1024
    out = pl.pallas_call(
        _ada_kernel,
        grid=(n, d3 // tn),
        in_specs=[
            pl.BlockSpec((rows, d), lambda s, j: (0, 0)),
            pl.BlockSpec((None, d, tn), lambda s, j: (s, 0, j)),
            pl.BlockSpec((None, 1, tn), lambda s, j: (s, 0, j)),
        ],
        out_specs=pl.BlockSpec((None, rows, tn), lambda s, j: (s, 0, j)),
        out_shape=jax.ShapeDtypeStruct((n, rows, d3), F32),
        compiler_params=_cparams("parallel", "parallel"),
        name="ada_mods",
    )(c_pad, ada_w.reshape(n, d, d3), ada_b.reshape(n, 1, d3))
    return out[:, :bsz].reshape(n, bsz, 3, d).transpose(0, 2, 1, 3).reshape(n, 3, bsz, 1, d)


SHIFT, SCALE, GATE = 0, 1, 2


def _mod_spec(d, sub, which, tiles_per_batch):
    return pl.BlockSpec((None, None, None, 1, d),
                        lambda i, *_: (sub, which, i // tiles_per_batch, 0, 0))


def _row_spec(d):
    return pl.BlockSpec((1, d), lambda *_: (0, 0))


def _modulate_kernel(x_ref, sc_ref, sh_ref, h_ref):
    h_ref[...] = (x_ref[...] * (1.0 + sc_ref[...]) + sh_ref[...]).astype(BF16)


def _modulate(x, mods, sub, seq, tm=512):
    t, d = x.shape
    tpb = seq // tm
    return pl.pallas_call(
        _modulate_kernel,
        grid=(t // tm,),
        in_specs=[pl.BlockSpec((tm, d), lambda i: (i, 0)),
                  _mod_spec(d, sub, SCALE, tpb), _mod_spec(d, sub, SHIFT, tpb)],
        out_specs=pl.BlockSpec((tm, d), lambda i: (i, 0)),
        out_shape=jax.ShapeDtypeStruct((t, d), BF16),
        compiler_params=_cparams("parallel"),
        name="modulate",
    )(x, mods, mods)


def _proj_kernel(segs, n_extra, h_ref, w_ref, *refs):
    extra = refs[:n_extra]
    outs = refs[n_extra:]
    j = pl.program_id(1)
    for (lo, hi, epi), o_ref in zip(segs, outs):
        def _store(o_ref=o_ref, epi=epi):
            acc = _dot(h_ref[...], w_ref[...])
            o_ref[...] = epi(acc, *[e[...] for e in extra]).astype(o_ref.dtype)
        if len(segs) == 1:
            _store()
        else:
            pl.when(jnp.logical_and(j >= lo, j < hi))(_store)


def _proj(h, w, segs, extra=(), extra_specs=(), tm=1024, tn=512, name="proj"):
    t, k = h.shape
    n = w.shape[1]
    bounds, lo = [], 0
    for n_cols, epi, _ in segs:
        assert n_cols % tn == 0
        bounds.append((lo, lo + n_cols // tn, epi))
        lo += n_cols // tn
    assert lo * tn == n
    out_specs = [
        pl.BlockSpec((tm, tn), functools.partial(
            lambda i, j, lo, hi: (i, jnp.clip(j - lo, 0, hi - lo - 1)), lo=lo_, hi=hi_))
        for lo_, hi_, _ in bounds]
    out_shape = [jax.ShapeDtypeStruct((t, n_cols), dt) for n_cols, _, dt in segs]
    return pl.pallas_call(
        functools.partial(_proj_kernel, bounds, len(extra)),
        grid=(t // tm, n // tn),
        in_specs=[pl.BlockSpec((tm, k), lambda i, j: (i, 0)),
                  pl.BlockSpec((k, tn), lambda i, j: (0, j))] + list(extra_specs),
        out_specs=out_specs,
        out_shape=out_shape,
        compiler_params=_cparams("parallel", "arbitrary"),
        name=name,
    )(h, w, *extra)


def _ln_mod_store(y, x_ref, gate_ref, lnw_ref, lnb_ref, sc_ref, sh_ref, xo_ref, ho_ref, rows=slice(None)):
    z = DN_ALPHA * x_ref[rows, :] + (1.0 + gate_ref[...]) * y
    mu = jnp.mean(z, axis=-1, keepdims=True)
    zc = z - mu
    var = jnp.mean(zc * zc, axis=-1, keepdims=True)
    xn = zc * lax.rsqrt(var + LN_EPS) * lnw_ref[...] + lnb_ref[...]
    xo_ref[rows, :] = xn
    if ho_ref is not None:
        ho_ref[rows, :] = (xn * (1.0 + sc_ref[...]) + sh_ref[...]).astype(BF16)


OUT_LN_CHUNKS = 2


def _out_ln_kernel(n_a, has_next, *refs):
    a_refs = refs[:n_a]
    w_ref, x_ref, gate_ref, lnw_ref, lnb_ref = refs[n_a:n_a + 5]
    rest = refs[n_a + 5:]
    if has_next:
        sc_ref, sh_ref, xo_ref, ho_ref = rest
    else:
        (xo_ref,), sc_ref, sh_ref, ho_ref = rest, None, None, None
    rc = x_ref.shape[0] // OUT_LN_CHUNKS
    for c in range(OUT_LN_CHUNKS):
        rows = slice(c * rc, (c + 1) * rc)
        a = a_refs[0][rows, :]
        for r in a_refs[1:]:
            a = a.astype(F32) + r[rows, :].astype(F32)
        y = _dot(a.astype(BF16), w_ref[...])
        _ln_mod_store(y, x_ref, gate_ref, lnw_ref, lnb_ref, sc_ref, sh_ref, xo_ref, ho_ref, rows)


def _out_ln(a_list, w, x, mods, sub, sub_next, ln_w, ln_b, seq, tm=512):
    t, d = x.shape
    k = w.shape[0]
    tpb = seq // tm
    has_next = sub_next is not None
    in_specs = [pl.BlockSpec((tm, k), lambda i: (i, 0)) for _ in a_list]
    in_specs += [pl.BlockSpec((k, d), lambda i: (0, 0), pipeline_mode=pl.Buffered(1)),
                 pl.BlockSpec((tm, d), lambda i: (i, 0)),
                 _mod_spec(d, sub, GATE, tpb), _row_spec(d), _row_spec(d)]
    args = list(a_list) + [w, x, mods, ln_w.reshape(1, d), ln_b.reshape(1, d)]
    out_specs = [pl.BlockSpec((tm, d), lambda i: (i, 0))]
    out_shape = [jax.ShapeDtypeStruct((t, d), F32)]
    if has_next:
        in_specs += [_mod_spec(d, sub_next, SCALE, tpb), _mod_spec(d, sub_next, SHIFT, tpb)]
        args += [mods, mods]
        out_specs.append(pl.BlockSpec((tm, d), lambda i: (i, 0)))
        out_shape.append(jax.ShapeDtypeStruct((t, d), BF16))
    res = pl.pallas_call(
        functools.partial(_out_ln_kernel, len(a_list), has_next),
        grid=(t // tm,),
        in_specs=in_specs, out_specs=out_specs, out_shape=out_shape,
        compiler_params=_cparams("parallel"),
        name="out_ln",
    )(*args)
    return (res[0], res[1]) if has_next else (res[0], None)


MLP_EPILOGUE_CHUNKS = 2


def _mlp_kernel(has_next, h_ref, w1_ref, w2_ref, x_ref, gate_ref, lnw_ref, lnb_ref, *rest):
    if has_next:
        sc_ref, sh_ref, xo_ref, ho_ref, acc_ref = rest
    else:
        (xo_ref, acc_ref), sc_ref, sh_ref, ho_ref = rest, None, None, None
    j = pl.program_id(1)
    last = pl.num_programs(1) - 1
    tm = h_ref.shape[0]

    def partial_out(rows):
        hid = _dot(h_ref[rows, :], w1_ref[...])
        hid = jnp.square(jnp.maximum(hid, 0.0)).astype(BF16)
        return _dot(hid, w2_ref[...])

    @pl.when(j == 0)
    def _():
        acc_ref[...] = partial_out(slice(None))

    @pl.when(jnp.logical_and(j > 0, j < last))
    def _():
        acc_ref[...] += partial_out(slice(None))

    @pl.when(j == last)
    def _():
        rc = tm // MLP_EPILOGUE_CHUNKS
        for c in range(MLP_EPILOGUE_CHUNKS):
            rows = slice(c * rc, (c + 1) * rc)
            y = acc_ref[rows, :] + partial_out(rows)
            _ln_mod_store(y, x_ref, gate_ref, lnw_ref, lnb_ref, sc_ref, sh_ref, xo_ref, ho_ref, rows)


def _mlp(h, w1, w2, x, mods, sub, sub_next, ln_w, ln_b, seq, tm=512, tf=1024):
    t, d = x.shape
    ff = w1.shape[1]
    assert ff // tf >= 2
    tpb = seq // tm
    has_next = sub_next is not None
    in_specs = [pl.BlockSpec((tm, d), lambda i, j: (i, 0)),
                pl.BlockSpec((d, tf), lambda i, j: (0, j)),
                pl.BlockSpec((tf, d), lambda i, j: (j, 0)),
                pl.BlockSpec((tm, d), lambda i, j: (i, 0)),
                _mod_spec(d, sub, GATE, tpb), _row_spec(d), _row_spec(d)]
    args = [h, w1, w2, x, mods, ln_w.reshape(1, d), ln_b.reshape(1, d)]
    out_specs = [pl.BlockSpec((tm, d), lambda i, j: (i, 0))]
    out_shape = [jax.ShapeDtypeStruct((t, d), F32)]
    if has_next:
        in_specs += [_mod_spec(d, sub_next, SCALE, tpb), _mod_spec(d, sub_next, SHIFT, tpb)]
        args += [mods, mods]
        out_specs.append(pl.BlockSpec((tm, d), lambda i, j: (i, 0)))
        out_shape.append(jax.ShapeDtypeStruct((t, d), BF16))
    res = pl.pallas_call(
        functools.partial(_mlp_kernel, has_next),
        grid=(t // tm, ff // tf),
        in_specs=in_specs, out_specs=out_specs, out_shape=out_shape,
        scratch_shapes=[pltpu.VMEM((tm, d), F32)],
        compiler_params=_cparams("parallel", "arbitrary"),
        name="mlp",
    )(*args)
    return (res[0], res[1]) if has_next else (res[0], None)


HALO = 8


def _lru_scan_kernel(ts, gy_ref, xb_ref, cw_ref, cb_ref, gw_ref, gb_ref, lam_ref, o_ref,
                     buf, a_scr, u_scr, h_scr):
    i = pl.program_id(1)
    width = xb_ref.shape[-1]
    bd = width // LRU_BLOCKS

    @pl.when(i == 0)
    def _():
        buf[0:HALO, :] = jnp.zeros((HALO, width), F32)
        h_scr[...] = jnp.zeros_like(h_scr)

    @pl.when(i > 0)
    def _():
        buf[0:HALO, :] = buf[ts:ts + HALO, :]

    buf[HALO:HALO + ts, :] = xb_ref[...]
    cw = cw_ref[...]
    taps = [buf[HALO - (CONV_WIDTH - 1) + k:HALO - (CONV_WIDTH - 1) + k + ts, :] * cw[k:k + 1, :]
            for k in range(CONV_WIDTH)]
    conv = taps[0]
    for tap in taps[1:]:
        conv = conv + tap
    xc = cb_ref[...] + conv
    xcb = xc.astype(BF16)
    gates = []
    for jg in range(2):
        cols = [_dot(xcb[:, n * bd:(n + 1) * bd], gw_ref[jg, n]) for n in range(LRU_BLOCKS)]
        gates.append(jnp.concatenate(cols, axis=1) + gb_ref[jg:jg + 1, :])
    r = 0.5 * (1.0 + jnp.tanh(0.5 * gates[0]))
    ig = 0.5 * (1.0 + jnp.tanh(0.5 * gates[1]))
    neg_lam = -lam_ref[...]
    softplus = jnp.maximum(neg_lam, 0.0) + jnp.log1p(jnp.exp(-jnp.abs(neg_lam)))
    log_a = -LRU_C * r * softplus
    a = jnp.exp(log_a)
    a_scr[...] = a
    one_m_a2 = -jnp.tanh(log_a) * (1.0 + a * a)
    root = one_m_a2 * lax.rsqrt(jnp.maximum(one_m_a2, F32_TINY))
    u_scr[...] = root * (ig * xc)

    def step(t, h):
        h = a_scr[pl.ds(t, 1), :] * h + u_scr[pl.ds(t, 1), :]
        u_scr[pl.ds(t, 1), :] = h
        return h

    h_scr[...] = lax.fori_loop(0, ts, step, h_scr[...], unroll=8)
    o_ref[...] = (gy_ref[...] * u_scr[...]).astype(BF16)


def _lru_scan(gy, xb, conv_w, conv_b, gate_w, gate_b, lam, ts=512):
    bsz, seq, width = xb.shape
    bd = width // LRU_BLOCKS
    tile = pl.BlockSpec((None, ts, width), lambda b, i: (b, i, 0))
    return pl.pallas_call(
        functools.partial(_lru_scan_kernel, ts),
        grid=(bsz, seq // ts),
        in_specs=[tile, tile,
                  pl.BlockSpec((CONV_WIDTH, width), lambda b, i: (0, 0)),
                  pl.BlockSpec((1, width), lambda b, i: (0, 0)),
                  pl.BlockSpec((2, LRU_BLOCKS, bd, bd), lambda b, i: (0, 0, 0, 0)),
                  pl.BlockSpec((2, width), lambda b, i: (0, 0)),
                  pl.BlockSpec((1, width), lambda b, i: (0, 0))],
        out_specs=tile,
        out_shape=jax.ShapeDtypeStruct((bsz, seq, width), BF16),
        scratch_shapes=[pltpu.VMEM((ts + HALO, width), F32), pltpu.VMEM((ts, width), F32),
                        pltpu.VMEM((ts, width), F32), pltpu.VMEM((1, width), F32)],
        compiler_params=_cparams("parallel", "arbitrary"),
        name="lru_scan",
    )(gy, xb, conv_w, conv_b.reshape(1, width), gate_w.astype(BF16), gate_b, lam.reshape(1, width))


def _lru_mixer(h, bsz, seq, w_in, conv_w, conv_b, gate_w, gate_b, lam):
    width = w_in.shape[1] // 2
    gy, xb = _proj(h, w_in.astype(BF16),
                   [(width, lambda acc: _gelu_tanh(acc), BF16), (width, lambda acc: acc, F32)],
                   tn=1024, name="lru_proj")
    a = _lru_scan(gy.reshape(bsz, seq, width), xb.reshape(bsz, seq, width),
                  conv_w, conv_b, gate_w, gate_b, lam)
    return [a.reshape(bsz * seq, width)]


def _t5_bucket_np(dist):
    n = np.maximum(np.asarray(dist, np.int64), 0)
    max_exact = T5_BUCKETS // 2
    n_large = T5_BUCKETS - max_exact
    ratio = T5_MAX_DIST // max_exact
    thresholds = []
    for k in range(1, n_large):
        m = max_exact
        while m ** n_large < ratio ** k * max_exact ** n_large:
            m += 1
        thresholds.append(m)
    large = max_exact + sum((n >= th).astype(np.int64) for th in thresholds)
    return np.where(n < max_exact, n, large).astype(np.int32)


def _bias_table_kernel(sub_far, rb_ref, idx_ref, o_ref):
    hd = pl.program_id(0)
    idx = idx_ref[...]
    base = rb_ref[T5_BUCKETS - 1, hd] if sub_far else 0.0
    acc = jnp.full(idx.shape, NEG_MASK, F32)
    for b in range(T5_BUCKETS):
        acc = jnp.where(idx == b, (rb_ref[b, hd] - base) * LOG2E, acc)
    o_ref[...] = acc


def _bias_table(rel_bias, idx_np, sub_far, name):
    rows, cols = idx_np.shape
    tr = 128
    n_heads = rel_bias.shape[1]
    return pl.pallas_call(
        functools.partial(_bias_table_kernel, sub_far),
        grid=(n_heads, rows // tr),
        in_specs=[pl.BlockSpec(memory_space=pltpu.SMEM),
                  pl.BlockSpec((tr, cols), lambda hd, i: (i, 0))],
        out_specs=pl.BlockSpec((None, tr, cols), lambda hd, i: (hd, i, 0)),
        out_shape=jax.ShapeDtypeStruct((n_heads, rows, cols), F32),
        compiler_params=_cparams("parallel", "parallel"),
        name=name,
    )(rel_bias, jnp.asarray(idx_np))


def _compress_kernel(seq, x_ref, pe_ref, w1_ref, b1_ref, w2_ref, o_ref, xs):
    xs[0:seq, :] = x_ref[...]
    xs[seq:CMP_ROWS, :] = jnp.zeros((CMP_ROWS - seq, LANES), F32)
    acc = jnp.zeros((N_CMP_SLOTS, LANES), F32)
    for l in range(CMP_BLOCK):
        rows = xs[pl.ds(l, N_CMP_SLOTS, stride=CMP_STRIDE), :] + pe_ref[l:l + 1, :]
        acc = acc + _dot(rows.astype(BF16), w1_ref[l])
    hid = _gelu_tanh(acc + b1_ref[...])
    o_ref[...] = _dot(hid.astype(BF16), w2_ref[...]).astype(BF16)


def _compress(cmp_tok, pe, w1, b1, w2):
    bsz, seq, _ = cmp_tok.shape
    g = NSA_KV_GROUPS
    dh = NSA_HEAD_DIM
    return pl.pallas_call(
        functools.partial(_compress_kernel, seq),
        grid=(bsz, 2, g),
        in_specs=[pl.BlockSpec((None, seq, dh), lambda b, j, gi: (b, 0, j * g + gi)),
                  pl.BlockSpec((None, CMP_BLOCK, dh), lambda b, j, gi: (j, 0, 0)),
                  pl.BlockSpec((None, CMP_BLOCK, dh, dh), lambda b, j, gi: (j, 0, 0, 0)),
                  pl.BlockSpec((None, 1, dh), lambda b, j, gi: (j, 0, 0)),
                  pl.BlockSpec((None, dh, dh), lambda b, j, gi: (j, 0, 0))],
        out_specs=pl.BlockSpec((None, None, None, N_CMP_SLOTS, dh), lambda b, j, gi: (b, j, gi, 0, 0)),
        out_shape=jax.ShapeDtypeStruct((bsz, 2, g, N_CMP_SLOTS, dh), BF16),
        scratch_shapes=[pltpu.VMEM((CMP_ROWS, dh), F32)],
        compiler_params=_cparams("parallel", "parallel", "parallel"),
        name="nsa_compress",
    )(cmp_tok, pe, w1.astype(BF16), b1.reshape(2, 1, dh), w2.astype(BF16))


CMP_COLS = N_CMP_SLOTS // LANES
CMP_COL_TOKENS = CMP_STRIDE * LANES


def _cmp_table_idx(tq):
    ql = np.arange(tq)[:, None]
    m = np.arange(N_CMP_SLOTS)[None, :]
    e = np.where(m <= (tq - CMP_BLOCK) // CMP_STRIDE, -m, N_CMP_SLOTS - m)
    dist = CMP_STRIDE * e + ql - (CMP_BLOCK - 1)
    return _t5_bucket_np(dist)


def _sel_sum_matrix():
    i = np.arange(N_CMP_SLOTS)[:, None]
    j = np.arange(LANES)[None, :]
    ratio = SEL_BLOCK // CMP_STRIDE
    return ((i >= ratio * j - 1) & (i <= ratio * j + ratio - 1)).astype(np.float32)


CMP_SUB = 2


def _cmp_attn_kernel(tq, q_ref, kc_ref, vc_ref, tb_ref, sm_ref, g_ref, o_ref, val_ref):
    i = pl.program_id(2)
    lane = lax.broadcasted_iota(jnp.int32, (tq, LANES), 1)
    row = lax.broadcasted_iota(jnp.int32, (tq, LANES), 0)

    def body(nc):
        width = nc * LANES
        kc = kc_ref[0:width, :]
        vc = vc_ref[0:width, :]
        ql = lax.broadcasted_iota(jnp.int32, (tq, width), 0)
        blk = lax.broadcasted_iota(jnp.int32, (tq, width), 1)
        rel = ql - CMP_STRIDE * blk - (CMP_BLOCK - 1)
        for sub in range(CMP_SUB):
            rows = slice(sub * tq, (sub + 1) * tq)
            tile = i * CMP_SUB + sub
            t0 = tile * tq
            gate = g_ref[rows, :]
            causal = jnp.where(t0 + rel >= 0, 0.0, NEG_MASK)
            row_ok = (t0 + lax.broadcasted_iota(jnp.int32, (tq, 1), 0)) >= CMP_BLOCK - 1
            shift = (tile * (tq // CMP_STRIDE)) % LANES
            keep = lane >= shift
            p_sum = jnp.zeros((tq, width), F32)
            for hd in range(NSA_HPG):
                rot = {}
                for c in range(nc):
                    for k in ((c - (nc - 1)) % CMP_COLS, (c - nc) % CMP_COLS):
                        if k not in rot:
                            rot[k] = pltpu.roll(tb_ref[hd, :, k * LANES:(k + 1) * LANES], shift, axis=1)
                tb = jnp.concatenate(
                    [jnp.where(keep, rot[(c - (nc - 1)) % CMP_COLS], rot[(c - nc) % CMP_COLS])
                     for c in range(nc)], axis=1)
                s = _dot_nt(q_ref[rows, hd * LANES:(hd + 1) * LANES], kc) + tb + causal
                e = jnp.exp2(s - jnp.max(s, axis=-1, keepdims=True))
                p = e * jnp.where(row_ok, 1.0 / jnp.sum(e, axis=-1, keepdims=True), 0.0)
                p_sum = p_sum + p
                o_ref[rows, hd * LANES:(hd + 1) * LANES] = (
                    _dot(p.astype(BF16), vc) * gate[:, hd:hd + 1]).astype(o_ref.dtype)
            sm = sm_ref[0:width, :]
            hi = p_sum.astype(BF16)
            r1 = p_sum - hi.astype(F32)
            mid = r1.astype(BF16)
            lo = (r1 - mid.astype(F32)).astype(BF16)
            score = (_dot(hi, sm) + _dot(mid, sm)) + _dot(lo, sm)
            cur = (t0 + row) >> 6
            forced = jnp.logical_or(lane == 0, jnp.logical_or(lane == cur, lane == cur - 1))
            val_ref[rows, :] = jnp.where(forced, SEL_FORCE, jnp.where(lane <= cur, score, -SEL_FORCE))

    last_q = (i + 1) * (CMP_SUB * tq) - 1
    n_cols = jnp.minimum((last_q - (CMP_BLOCK - 1)) // CMP_COL_TOKENS + 1, CMP_COLS)
    for nc in range(1, CMP_COLS + 1):
        pl.when(n_cols == nc)(functools.partial(body, nc))


N_FORCED = 3


def _topk_mask_kernel(n_top, val_ref, m_ref):
    val0 = val_ref[...]
    lane = lax.broadcasted_iota(jnp.int32, val0.shape, 1).astype(F32)
    forced = val0 >= 0.5 * SEL_FORCE
    val = jnp.where(forced, -jnp.inf, val0)
    sel = jnp.where(forced, 1.0, 0.0)
    for _ in range(n_top - N_FORCED):
        mx = jnp.max(val, axis=-1, keepdims=True)
        first = jnp.min(jnp.where(val == mx, lane, float(LANES)), axis=-1, keepdims=True)
        pick = lane == first
        sel = jnp.where(pick, 1.0, sel)
        val = jnp.where(pick, -jnp.inf, val)
    m_ref[...] = jnp.where(jnp.logical_and(sel > 0.0, val0 >= 0.0), 0.0, NEG_MASK).astype(BF16)


def _topk_mask(val, n_top, tr=1024):
    rows = val.shape[0]
    return pl.pallas_call(
        functools.partial(_topk_mask_kernel, n_top),
        grid=(rows // tr,),
        in_specs=[pl.BlockSpec((tr, LANES), lambda i: (i, 0))],
        out_specs=pl.BlockSpec((tr, LANES), lambda i: (i, 0)),
        out_shape=jax.ShapeDtypeStruct((rows, LANES), BF16),
        compiler_params=_cparams("parallel"),
        name="nsa_topk_mask",
    )(val)


def _cmp_attn(q, kvc, table, gates, tq=256):
    bsz, seq, _ = q.shape
    g = NSA_KV_GROUPS
    qw = NSA_HPG * NSA_HEAD_DIM
    rows = tq * CMP_SUB
    assert CMP_COL_TOKENS % rows == 0 and tq % CMP_STRIDE == 0
    return pl.pallas_call(
        functools.partial(_cmp_attn_kernel, tq),
        grid=(bsz, g, seq // rows),
        in_specs=[pl.BlockSpec((None, rows, qw), lambda b, gi, i: (b, i, gi)),
                  pl.BlockSpec((None, None, None, N_CMP_SLOTS, NSA_HEAD_DIM), lambda b, gi, i: (b, 0, gi, 0, 0)),
                  pl.BlockSpec((None, None, None, N_CMP_SLOTS, NSA_HEAD_DIM), lambda b, gi, i: (b, 1, gi, 0, 0)),
                  pl.BlockSpec((NSA_HPG, tq, N_CMP_SLOTS), lambda b, gi, i: (gi, 0, 0)),
                  pl.BlockSpec((N_CMP_SLOTS, LANES), lambda b, gi, i: (0, 0)),
                  pl.BlockSpec((None, rows, LANES), lambda b, gi, i: (b, i, gi))],
        out_specs=[pl.BlockSpec((None, rows, qw), lambda b, gi, i: (b, i, gi)),
                   pl.BlockSpec((None, None, rows, LANES), lambda b, gi, i: (b, gi, i, 0))],
        out_shape=[jax.ShapeDtypeStruct((bsz, seq, g * qw), BF16),
                   jax.ShapeDtypeStruct((bsz, g, seq, LANES), F32)],
        compiler_params=_cparams("parallel", "parallel", "parallel"),
        name="nsa_cmp_attn",
    )(q, kvc, kvc, table, jnp.asarray(_sel_sum_matrix(), BF16), gates)


SEL_TQ = 256
SEL_TK = 256
SEL_N_DELTA = -(-(T5_MAX_DIST - 1 + SEL_TK) // SEL_TQ)
SEL_N_NEAR = -(-SEL_N_DELTA // (SEL_TK // SEL_TQ))
SEL_STREAMS = 4


def _sel_table_idx():
    d = np.arange(SEL_N_DELTA)[:, None, None]
    ql = np.arange(SEL_TQ)[None, :, None]
    kl = np.arange(SEL_TK)[None, None, :]
    dist = ql - kl + SEL_TQ * d
    idx = np.where(dist >= 0, _t5_bucket_np(dist), MASK_BUCKET)
    return idx.reshape(SEL_N_DELTA * SEL_TQ, SEL_TK).astype(np.int32)


def _sel_attn_kernel(q_ref, m_ref, k_ref, v_ref, oh_ref, nb_ref, g_ref, o_ref, qa, m_scr, acc):
    tq, tk = SEL_TQ, SEL_TK
    ratio = tk // tq
    i = pl.program_id(2)
    kd = i // ratio
    sel_mask = m_ref[...]
    for hd in range(NSA_HPG):
        qa[hd * tq:(hd + 1) * tq, 0:LANES] = q_ref[:, hd * LANES:(hd + 1) * LANES]
        qa[hd * tq:(hd + 1) * tq, LANES:2 * LANES] = sel_mask
    m_scr[...] = jnp.full(m_scr.shape, -jnp.inf, F32)
    acc[...] = jnp.zeros_like(acc)
    n_kt = k_ref.shape[0] // tk
    ones = jnp.ones((tk, LANES), BF16)

    def update(st, tiles):
        scores, values = [], []
        for kt, valid, delta in tiles:
            k0 = pl.multiple_of(jnp.clip(kt, 0, n_kt - 1) * tk, tk)
            ka = jnp.concatenate([k_ref[pl.ds(k0, tk), :], oh_ref[pl.ds(k0, tk), :]], axis=1)
            s = _dot_nt(qa[...], ka)
            if valid is not None:
                s = s + jnp.where(valid, 0.0, NEG_MASK)
            if delta is not None:
                s = s + jnp.concatenate([nb_ref[hd, delta] for hd in range(NSA_HPG)], axis=0)
            scores.append(s)
            values.append(jnp.concatenate([v_ref[pl.ds(k0, tk), :], ones], axis=1))
        s_max = scores[0]
        for s in scores[1:]:
            s_max = jnp.maximum(s_max, s)
        m_prev = m_scr[st]
        m_new = jnp.maximum(m_prev, jnp.max(s_max, axis=-1, keepdims=True))
        alpha = jnp.exp2(m_prev - m_new)
        m_rep = jnp.concatenate([m_new] * (tk // LANES), axis=1)
        pv = None
        for s, va in zip(scores, values):
            part = _dot(jnp.exp2(s - m_rep).astype(BF16), va)
            pv = part if pv is None else pv + part
        acc[st] = jnp.concatenate([alpha, alpha], axis=1) * acc[st] + pv
        m_scr[st] = m_new

    n_far = jnp.maximum(kd - (SEL_N_NEAR - 1), 0)
    pair = 2 * SEL_STREAMS

    def far_pairs(it, carry):
        for st in range(SEL_STREAMS):
            kt = it * pair + 2 * st
            update(st, [(kt, None, None), (kt + 1, None, None)])
        return carry

    lax.fori_loop(0, n_far // pair, far_pairs, 0)
    base = (n_far // pair) * pair
    rem = n_far - base

    @pl.when(rem >= SEL_STREAMS)
    def _():
        for st in range(SEL_STREAMS):
            update(st, [(base + st, None, None)])

    base = base + jnp.where(rem >= SEL_STREAMS, SEL_STREAMS, 0)
    left = n_far - base
    slots = [(base + f, f < left, None) for f in range(SEL_STREAMS - 1)]
    for back in range(SEL_N_NEAR - 1, -1, -1):
        kt = kd - back
        slots.append((kt, kt >= 0, (i - kd * ratio) + back * ratio))
    for st in range(SEL_STREAMS):
        update(st, slots[st::SEL_STREAMS])

    m_all = m_scr[0]
    for st in range(1, SEL_STREAMS):
        m_all = jnp.maximum(m_all, m_scr[st])
    tot = jnp.zeros((NSA_HPG * tq, 2 * LANES), F32)
    for st in range(SEL_STREAMS):
        w = jnp.exp2(m_scr[st] - m_all)
        tot = tot + jnp.concatenate([w, w], axis=1) * acc[st]
    gate = g_ref[...]
    out = tot[:, :LANES] / jnp.maximum(tot[:, LANES:], 1e-30)
    for hd in range(NSA_HPG):
        o_ref[:, hd * LANES:(hd + 1) * LANES] = (
            out[hd * tq:(hd + 1) * tq, :] * gate[:, NSA_HPG + hd:NSA_HPG + hd + 1]).astype(o_ref.dtype)


def _sel_attn(q, sel_mask, kv_tok, onehot, table, gates):
    bsz, seq, _ = q.shape
    g = NSA_KV_GROUPS
    dh = NSA_HEAD_DIM
    qw = NSA_HPG * dh
    tq, tk = SEL_TQ, SEL_TK
    rows = NSA_HPG * tq
    return pl.pallas_call(
        _sel_attn_kernel,
        grid=(bsz, g, seq // tq),
        in_specs=[pl.BlockSpec((None, tq, qw), lambda b, gi, i: (b, i, gi)),
                  pl.BlockSpec((None, None, tq, LANES), lambda b, gi, i: (b, gi, i, 0)),
                  pl.BlockSpec((None, seq, dh), lambda b, gi, i: (b, 0, gi)),
                  pl.BlockSpec((None, seq, dh), lambda b, gi, i: (b, 0, g + gi)),
                  pl.BlockSpec((seq, LANES), lambda b, gi, i: (0, 0)),
                  pl.BlockSpec((NSA_HPG, SEL_N_DELTA, tq, tk), lambda b, gi, i: (gi, 0, 0, 0)),
                  pl.BlockSpec((None, tq, LANES), lambda b, gi, i: (b, i, gi))],
        out_specs=pl.BlockSpec((None, tq, qw), lambda b, gi, i: (b, i, gi)),
        out_shape=jax.ShapeDtypeStruct((bsz, seq, g * qw), BF16),
        scratch_shapes=[pltpu.VMEM((rows, 2 * LANES), BF16), pltpu.VMEM((SEL_STREAMS, rows, LANES), F32),
                        pltpu.VMEM((SEL_STREAMS, rows, 2 * dh), F32)],
        compiler_params=_cparams("parallel", "parallel", "parallel"),
        name="nsa_sel_attn",
    )(q, sel_mask, kv_tok, kv_tok, onehot, table, gates)


WIN_TQ = 128
WIN_SUB = 4
WIN_KEYS = WINDOW + WIN_TQ


def _win_table_idx():
    ql = np.arange(WIN_TQ)[:, None]
    kl = np.arange(WIN_KEYS)[None, :]
    dist = ql + WINDOW - kl
    return np.where((dist >= 0) & (dist < WINDOW), _t5_bucket_np(dist), MASK_BUCKET).astype(np.int32)


def _win_attn_kernel(seq, q_ref, k_ref, v_ref, wb_ref, g_ref, o_ref, kpad, vpad, qs):
    tq = WIN_TQ
    i = pl.program_id(2)

    @pl.when(i == 0)
    def _():
        kpad[0:WINDOW, :] = jnp.zeros((WINDOW, LANES), BF16)
        vpad[0:WINDOW, :] = jnp.zeros((WINDOW, LANES), BF16)
        kpad[WINDOW:WINDOW + seq, :] = k_ref[...]
        vpad[WINDOW:WINDOW + seq, :] = v_ref[...]

    bias = wb_ref[...].reshape(NSA_HPG * tq, WIN_KEYS)

    def step(header_visible):
        for sub in range(WIN_SUB):
            rows = slice(sub * tq, (sub + 1) * tq)
            t0 = pl.multiple_of((i * WIN_SUB + sub) * tq, tq)
            for hd in range(NSA_HPG):
                qs[sub, hd * tq:(hd + 1) * tq, :] = q_ref[rows, hd * LANES:(hd + 1) * LANES]
            keys = kpad[pl.ds(t0, WIN_KEYS), :]
            vals = vpad[pl.ds(t0, WIN_KEYS), :]
            s = _dot_nt(qs[sub], keys) + bias
            if header_visible:
                kl = lax.broadcasted_iota(jnp.int32, s.shape, 1)
                s = jnp.where(kl >= WINDOW - t0, s, NEG_MASK)
            e = jnp.exp2(s - jnp.max(s, axis=-1, keepdims=True))
            out = _dot(e.astype(BF16), vals) * (1.0 / jnp.sum(e, axis=-1, keepdims=True))
            gate = g_ref[rows, :]
            for hd in range(NSA_HPG):
                o_ref[rows, hd * LANES:(hd + 1) * LANES] = (
                    out[hd * tq:(hd + 1) * tq, :]
                    * gate[:, 2 * NSA_HPG + hd:2 * NSA_HPG + hd + 1]).astype(o_ref.dtype)

    n_header_steps = -(-WINDOW // (WIN_SUB * tq))
    pl.when(i < n_header_steps)(functools.partial(step, True))
    pl.when(i >= n_header_steps)(functools.partial(step, False))


def _win_attn(q, kv_tok, table, gates):
    bsz, seq, _ = q.shape
    g = NSA_KV_GROUPS
    dh = NSA_HEAD_DIM
    qw = NSA_HPG * dh
    tq = WIN_TQ
    rows = tq * WIN_SUB
    return pl.pallas_call(
        functools.partial(_win_attn_kernel, seq),
        grid=(bsz, g, seq // rows),
        in_specs=[pl.BlockSpec((None, rows, qw), lambda b, gi, i: (b, i, gi)),
                  pl.BlockSpec((None, seq, dh), lambda b, gi, i: (b, 0, 2 * g + gi)),
                  pl.BlockSpec((None, seq, dh), lambda b, gi, i: (b, 0, 3 * g + gi)),
                  pl.BlockSpec((NSA_HPG, tq, WIN_KEYS), lambda b, gi, i: (gi, 0, 0)),
                  pl.BlockSpec((None, rows, LANES), lambda b, gi, i: (b, i, gi))],
        out_specs=pl.BlockSpec((None, rows, qw), lambda b, gi, i: (b, i, gi)),
        out_shape=jax.ShapeDtypeStruct((bsz, seq, g * qw), BF16),
        scratch_shapes=[pltpu.VMEM((seq + WINDOW, dh), BF16), pltpu.VMEM((seq + WINDOW, dh), BF16),
                        pltpu.VMEM((WIN_SUB, NSA_HPG * tq, dh), BF16)],
        compiler_params=_cparams("parallel", "parallel", "arbitrary"),
        name="nsa_win_attn",
    )(q, kv_tok, kv_tok, table, gates)


def _nsa_mixer(h, bsz, seq, rel_bias, w_in, cmp_pe, cmp_w1, cmp_b1, cmp_w2):
    g, hpg, dh = NSA_KV_GROUPS, NSA_HPG, NSA_HEAD_DIM
    qw = NSA_HEADS * dh
    kvw = g * dh
    n_main = qw + 2 * N_BRANCH * kvw
    scale = dh ** -0.5 * LOG2E
    q, cmp_tok, kv_tok = _proj(
        h, w_in[:, :n_main].astype(BF16),
        [(qw, lambda acc: acc * scale, BF16), (2 * kvw, lambda acc: acc, F32), (4 * kvw, lambda acc: acc, BF16)],
        tn=1024, name="nsa_proj")
    src = np.zeros((g * LANES,), np.int32)
    valid = np.zeros((g * LANES,), bool)
    for gi in range(g):
        for br in range(N_BRANCH):
            for hd in range(hpg):
                src[gi * LANES + br * hpg + hd] = n_main + br * NSA_HEADS + gi * hpg + hd
                valid[gi * LANES + br * hpg + hd] = True
    w_gate = jnp.where(jnp.asarray(valid)[None, :], w_in[:, src], 0.0).astype(BF16)
    (gates,) = _proj(h, w_gate, [(g * LANES, lambda acc: _sigmoid(acc), F32)], name="nsa_gate_proj")

    q = q.reshape(bsz, seq, qw)
    gates = gates.reshape(bsz, seq, g * LANES)
    kv_tok = kv_tok.reshape(bsz, seq, 4 * kvw)
    kvc = _compress(cmp_tok.reshape(bsz, seq, 2 * kvw), cmp_pe, cmp_w1, cmp_b1, cmp_w2)

    n_top = min(SEL_TOP_N, seq // SEL_BLOCK)
    cmp_tq = 256
    cmp_table = _bias_table(rel_bias, _cmp_table_idx(cmp_tq), False, "nsa_cmp_bias")
    o_c, sel_val = _cmp_attn(q, kvc, cmp_table, gates, cmp_tq)
    sel_mask = _topk_mask(sel_val.reshape(bsz * g * seq, LANES), n_top).reshape(bsz, g, seq, LANES)

    sel_table = _bias_table(rel_bias, _sel_table_idx(), True, "nsa_sel_bias")
    sel_table = sel_table.reshape(NSA_HEADS, SEL_N_DELTA, SEL_TQ, SEL_TK)
    onehot = (jnp.arange(seq, dtype=jnp.int32)[:, None] // SEL_BLOCK
              == jnp.arange(LANES, dtype=jnp.int32)[None, :]).astype(BF16)
    o_s = _sel_attn(q, sel_mask, kv_tok, onehot, sel_table, gates)

    win_table = _bias_table(rel_bias, _win_table_idx(), False, "nsa_win_bias")
    o_w = _win_attn(q, kv_tok, win_table, gates)
    t = bsz * seq
    return [o_c.reshape(t, qw), o_s.reshape(t, qw), o_w.reshape(t, qw)]


def _rotary_epi(scale, acc, cos, sin):
    half = cos.shape[-1]
    outs = []
    for hd in range(acc.shape[1] // (2 * half)):
        x1 = acc[:, 2 * hd * half:(2 * hd + 1) * half]
        x2 = acc[:, (2 * hd + 1) * half:(2 * hd + 2) * half]
        outs += [x1 * cos - x2 * sin, x1 * sin + x2 * cos]
    rot = jnp.concatenate(outs, axis=1)
    return rot * scale if scale != 1.0 else rot


RET_HPS = 2


def _ret_kernel(tc, q_ref, k_ref, v_ref, g_ref, dec_ref, xi_ref, zeta_ref, cd_ref, gw_ref, gb_ref,
                o_ref, state):
    i = pl.program_id(2)
    dk = q_ref.shape[-1] // RET_HPS
    dv = v_ref.shape[-1] // RET_HPS

    @pl.when(i == 0)
    def _():
        state[...] = jnp.zeros_like(state)

    for c in range(tc // RET_CHUNK):
        rows = slice(c * RET_CHUNK, (c + 1) * RET_CHUNK)
        for hh in range(RET_HPS):
            qk_cols = slice(hh * dk, (hh + 1) * dk)
            v_cols = slice(hh * dv, (hh + 1) * dv)
            qc = q_ref[rows, qk_cols]
            kc = k_ref[rows, qk_cols]
            vc = v_ref[rows, v_cols]
            inner = _dot_nt(qc, kc) * dec_ref[hh]
            st = state[hh]
            out = _dot(inner.astype(BF16), vc) + _dot(qc, st.astype(BF16)) * xi_ref[hh]
            kz = (kc.astype(F32) * zeta_ref[hh]).astype(BF16)
            state[hh] = st * cd_ref[hh] + _dot_tn(kz, vc)
            mu = jnp.mean(out, axis=-1, keepdims=True)
            oc = out - mu
            var = jnp.mean(oc * oc, axis=-1, keepdims=True)
            y = (oc * lax.rsqrt(var + GN_EPS)) * gw_ref[:, v_cols] + gb_ref[:, v_cols]
            o_ref[rows, v_cols] = (g_ref[rows, v_cols] * y).astype(BF16)


def _ret_core(q, k, v, g, gn_w, gn_b, tc=512):
    bsz, seq, qk_total = q.shape
    v_total = v.shape[-1]
    nh = RET_HEADS
    dk = qk_total // nh
    dv = v_total // nh
    cs = RET_CHUNK
    hps = RET_HPS
    log_g = jnp.log1p(-jnp.exp2(-5.0 - jnp.arange(nh, dtype=F32)))
    idx = jnp.arange(cs, dtype=F32)
    diff = idx[:, None] - idx[None, :]
    decay = jnp.where(diff >= 0, jnp.exp(jnp.maximum(diff, 0.0) * log_g[:, None, None]), 0.0)
    xi = jnp.exp((idx + 1.0) * log_g[:, None])[:, :, None]
    zeta = jnp.exp((cs - 1.0 - idx) * log_g[:, None])[:, :, None]
    chunk_decay = jnp.exp(cs * log_g).reshape(nh, 1, 1)
    return pl.pallas_call(
        functools.partial(_ret_kernel, tc),
        grid=(bsz, nh // hps, seq // tc),
        in_specs=[pl.BlockSpec((None, tc, hps * dk), lambda b, hp, i: (b, i, hp)),
                  pl.BlockSpec((None, tc, hps * dk), lambda b, hp, i: (b, i, hp)),
                  pl.BlockSpec((None, tc, hps * dv), lambda b, hp, i: (b, i, hp)),
                  pl.BlockSpec((None, tc, hps * dv), lambda b, hp, i: (b, i, hp)),
                  pl.BlockSpec((hps, cs, cs), lambda b, hp, i: (hp, 0, 0)),
                  pl.BlockSpec((hps, cs, 1), lambda b, hp, i: (hp, 0, 0)),
                  pl.BlockSpec((hps, cs, 1), lambda b, hp, i: (hp, 0, 0)),
                  pl.BlockSpec((hps, 1, 1), lambda b, hp, i: (hp, 0, 0)),
                  pl.BlockSpec((1, hps * dv), lambda b, hp, i: (0, hp)),
                  pl.BlockSpec((1, hps * dv), lambda b, hp, i: (0, hp))],
        out_specs=pl.BlockSpec((None, tc, hps * dv), lambda b, hp, i: (b, i, hp)),
        out_shape=jax.ShapeDtypeStruct((bsz, seq, v_total), BF16),
        scratch_shapes=[pltpu.VMEM((hps, dk, dv), F32)],
        compiler_params=_cparams("parallel", "parallel", "arbitrary"),
        name="ret_core",
    )(q, k, v, g, decay, xi, zeta, chunk_decay, gn_w.reshape(1, v_total), gn_b.reshape(1, v_total))


def _ret_mixer(h, bsz, seq, w_in, gn_w, gn_b, tm=1024):
    qk_total = w_in.shape[0]
    v_total = (w_in.shape[1] - 2 * qk_total) // 2
    dk = qk_total // RET_HEADS
    half = dk // 2
    pos = jnp.arange(seq, dtype=F32)
    inv = ROPE_BASE ** (-jnp.arange(half, dtype=F32) / half)
    ang = pos[:, None] * inv[None, :]
    cos, sin = jnp.cos(ang), jnp.sin(ang)
    tpb = seq // tm
    rope_spec = pl.BlockSpec((tm, half), lambda i, j: (i % tpb, 0))
    q, k, v, g = _proj(
        h, w_in.astype(BF16),
        [(qk_total, functools.partial(_rotary_epi, 1.0), BF16),
         (qk_total, functools.partial(_rotary_epi, dk ** -0.5), BF16),
         (v_total, lambda acc, cos, sin: acc, BF16),
         (v_total, lambda acc, cos, sin: _silu(acc), BF16)],
        extra=(cos, sin), extra_specs=(rope_spec, rope_spec), tm=tm, tn=1024, name="ret_proj")
    a = _ret_core(q.reshape(bsz, seq, qk_total), k.reshape(bsz, seq, qk_total),
                  v.reshape(bsz, seq, v_total), g.reshape(bsz, seq, v_total), gn_w, gn_b)
    return [a.reshape(bsz * seq, v_total)]


def kernel(x, c, rel_bias, ada_w, ada_b, ln_w, ln_b, mlp_w1, mlp_w2, lru_w_in, lru_conv_w, lru_conv_b,
           lru_gate_w, lru_gate_b, lru_lambda, lru_w_out, nsa_w_in, nsa_cmp_pe, nsa_cmp_w1, nsa_cmp_b1,
           nsa_cmp_w2, nsa_w_out, ret_w_in, ret_gn_w, ret_gn_b, ret_w_out):
    bsz, seq, d = x.shape
    depth = ada_w.shape[0]
    mods = _ada_mods(c, ada_w, ada_b)
    xf = x.reshape(bsz * seq, d)
    h = _modulate(xf, mods, 0, seq)
    for layer in range(depth):
        mixer, inst = layer % N_MIXERS, layer // N_MIXERS
        sub = 2 * layer
        if mixer == 0:
            a_list = _lru_mixer(h, bsz, seq, lru_w_in[inst], lru_conv_w[inst], lru_conv_b[inst],
                                lru_gate_w[inst], lru_gate_b[inst], lru_lambda[inst])
            w_out = lru_w_out[inst]
        elif mixer == 1:
            a_list = _nsa_mixer(h, bsz, seq, rel_bias, nsa_w_in[inst], nsa_cmp_pe[inst], nsa_cmp_w1[inst],
                                nsa_cmp_b1[inst], nsa_cmp_w2[inst])
            w_out = nsa_w_out[inst]
        else:
            a_list = _ret_mixer(h, bsz, seq, ret_w_in[inst], ret_gn_w[inst], ret_gn_b[inst])
            w_out = ret_w_out[inst]
        xf, h = _out_ln(a_list, w_out.astype(BF16), xf, mods, sub, sub + 1, ln_w[layer, 0], ln_b[layer, 0], seq)
        sub_next = sub + 2 if layer + 1 < depth else None
        xf, h = _mlp(h, mlp_w1[layer].astype(BF16), mlp_w2[layer].astype(BF16), xf, mods, sub + 1, sub_next,
                     ln_w[layer, 1], ln_b[layer, 1], seq)
    return xf.reshape(bsz, seq, d)
```

```python
import functools
import math

import numpy as np
import jax
import jax.numpy as jnp
from jax import lax
from jax.experimental import pallas as pl
from jax.experimental.pallas import tpu as pltpu

F32 = jnp.float32
BF16 = jnp.bfloat16

DN_DEPTH = 4
N_MIXERS = 3
DN_ALPHA = (2 * DN_DEPTH) ** 0.25
LN_EPS = 1e-5
GN_EPS = 1e-5
LRU_BLOCKS = 8
CONV_WIDTH = 4
LRU_C = 8.0
NSA_HEADS = 16
NSA_HEAD_DIM = 128
NSA_KV_GROUPS = 4
NSA_HPG = NSA_HEADS // NSA_KV_GROUPS
N_BRANCH = 3
CMP_BLOCK = 32
CMP_STRIDE = 16
SEL_BLOCK = 64
SEL_TOP_N = 16
SEL_FORCE = 1e6
WINDOW = 512
NEG_BIG = 1e30
T5_BUCKETS = 32
T5_MAX_DIST = 1024
RET_HEADS = 8
RET_CHUNK = 128
ROPE_BASE = 10000.0

LANES = 128
VMEM_LIMIT_BYTES = 56 * 2 ** 20

NEG_MASK = -(2.0 ** 100)
MASK_BUCKET = T5_BUCKETS
F32_TINY = float(np.finfo(np.float32).tiny)
LOG2E = math.log2(math.e)

N_CMP_SLOTS = 4 * LANES
CMP_ROWS = CMP_STRIDE * N_CMP_SLOTS + 4 * CMP_STRIDE


def _cparams(*sem):
    return pltpu.CompilerParams(dimension_semantics=sem, vmem_limit_bytes=VMEM_LIMIT_BYTES)


def _sigmoid(x):
    return 1.0 / (1.0 + jnp.exp(-x))


def _silu(x):
    return x * _sigmoid(x)


def _gelu_tanh(x):
    return 0.5 * x * (1.0 + jnp.tanh(math.sqrt(2.0 / math.pi) * (x + 0.044715 * (x * x * x))))


def _dot(a, b):
    return jnp.dot(a, b, preferred_element_type=F32)


def _dot_nt(a, b):
    return lax.dot_general(a, b, (((1,), (1,)), ((), ())), preferred_element_type=F32)


def _dot_tn(a, b):
    return lax.dot_general(a, b, (((0,), (0,)), ((), ())), preferred_element_type=F32)


def _ada_kernel(c_ref, w_ref, b_ref, o_ref):
    cond = _silu(c_ref[...]).astype(BF16)
    o_ref[...] = _dot(cond, w_ref[...].astype(BF16)) + b_ref[...]


def _ada_mods(c, ada_w, ada_b):
    depth, _, d, d3 = ada_w.shape
    bsz = c.shape[0]
    n = depth * 2
    rows = 8
    c_pad = jnp.zeros((rows, d), F32).at[:bsz].set(c)
    tn = 1024
    out = pl.pallas_call(
        _ada_kernel,
        grid=(n, d3 // tn),
        in_specs=[
            pl.BlockSpec((rows, d), lambda s, j: (0, 0)),
            pl.BlockSpec((None, d, tn), lambda s, j: (s, 0, j)),
            pl.BlockSpec((None, 1, tn), lambda s, j: (s, 0, j)),
        ],
        out_specs=pl.BlockSpec((None, rows, tn), lambda s, j: (s, 0, j)),
        out_shape=jax.ShapeDtypeStruct((n, rows, d3), F32),
        compiler_params=_cparams("parallel", "parallel"),
        name="ada_mods",
    )(c_pad, ada_w.reshape(n, d, d3), ada_b.reshape(n, 1, d3))
    return out[:, :bsz].reshape(n, bsz, 3, d).transpose(0, 2, 1, 3).reshape(n, 3, bsz, 1, d)


SHIFT, SCALE, GATE = 0, 1, 2


def _mod_spec(d, sub, which, tiles_per_batch):
    return pl.BlockSpec((None, None, None, 1, d),
                        lambda i, *_: (sub, which, i // tiles_per_batch, 0, 0))


def _row_spec(d):
    return pl.BlockSpec((1, d), lambda *_: (0, 0))


def _modulate_kernel(x_ref, sc_ref, sh_ref, h_ref):
    h_ref[...] = (x_ref[...] * (1.0 + sc_ref[...]) + sh_ref[...]).astype(BF16)


def _modulate(x, mods, sub, seq, tm=512):
    t, d = x.shape
    tpb = seq // tm
    return pl.pallas_call(
        _modulate_kernel,
        grid=(t // tm,),
        in_specs=[pl.BlockSpec((tm, d), lambda i: (i, 0)),
                  _mod_spec(d, sub, SCALE, tpb), _mod_spec(d, sub, SHIFT, tpb)],
        out_specs=pl.BlockSpec((tm, d), lambda i: (i, 0)),
        out_shape=jax.ShapeDtypeStruct((t, d), BF16),
        compiler_params=_cparams("parallel"),
        name="modulate",
    )(x, mods, mods)


def _proj_kernel(segs, n_extra, h_ref, w_ref, *refs):
    extra = refs[:n_extra]
    outs = refs[n_extra:]
    j = pl.program_id(1)
    for (lo, hi, epi), o_ref in zip(segs, outs):
        def _store(o_ref=o_ref, epi=epi):
            acc = _dot(h_ref[...], w_ref[...])
            o_ref[...] = epi(acc, *[e[...] for e in extra]).astype(o_ref.dtype)
        if len(segs) == 1:
            _store()
        else:
            pl.when(jnp.logical_and(j >= lo, j < hi))(_store)


def _proj(h, w, segs, extra=(), extra_specs=(), tm=1024, tn=512, name="proj"):
    t, k = h.shape
    n = w.shape[1]
    bounds, lo = [], 0
    for n_cols, epi, _ in segs:
        assert n_cols % tn == 0
        bounds.append((lo, lo + n_cols // tn, epi))
        lo += n_cols // tn
    assert lo * tn == n
    out_specs = [
        pl.BlockSpec((tm, tn), functools.partial(
            lambda i, j, lo, hi: (i, jnp.clip(j - lo, 0, hi - lo - 1)), lo=lo_, hi=hi_))
        for lo_, hi_, _ in bounds]
    out_shape = [jax.ShapeDtypeStruct((t, n_cols), dt) for n_cols, _, dt in segs]
    return pl.pallas_call(
        functools.partial(_proj_kernel, bounds, len(extra)),
        grid=(t // tm, n // tn),
        in_specs=[pl.BlockSpec((tm, k), lambda i, j: (i, 0)),
                  pl.BlockSpec((k, tn), lambda i, j: (0, j))] + list(extra_specs),
        out_specs=out_specs,
        out_shape=out_shape,
        compiler_params=_cparams("parallel", "arbitrary"),
        name=name,
    )(h, w, *extra)


def _ln_mod_store(y, x_ref, gate_ref, lnw_ref, lnb_ref, sc_ref, sh_ref, xo_ref, ho_ref, rows=slice(None)):
    z = DN_ALPHA * x_ref[rows, :] + (1.0 + gate_ref[...]) * y
    mu = jnp.mean(z, axis=-1, keepdims=True)
    zc = z - mu
    var = jnp.mean(zc * zc, axis=-1, keepdims=True)
    xn = zc * lax.rsqrt(var + LN_EPS) * lnw_ref[...] + lnb_ref[...]
    xo_ref[rows, :] = xn
    if ho_ref is not None:
        ho_ref[rows, :] = (xn * (1.0 + sc_ref[...]) + sh_ref[...]).astype(BF16)


OUT_LN_CHUNKS = 2


def _out_ln_kernel(n_a, has_next, *refs):
    a_refs = refs[:n_a]
    w_ref, x_ref, gate_ref, lnw_ref, lnb_ref = refs[n_a:n_a + 5]
    rest = refs[n_a + 5:]
    if has_next:
        sc_ref, sh_ref, xo_ref, ho_ref = rest
    else:
        (xo_ref,), sc_ref, sh_ref, ho_ref = rest, None, None, None
    rc = x_ref.shape[0] // OUT_LN_CHUNKS
    for c in range(OUT_LN_CHUNKS):
        rows = slice(c * rc, (c + 1) * rc)
        a = a_refs[0][rows, :]
        for r in a_refs[1:]:
            a = a.astype(F32) + r[rows, :].astype(F32)
        y = _dot(a.astype(BF16), w_ref[...])
        _ln_mod_store(y, x_ref, gate_ref, lnw_ref, lnb_ref, sc_ref, sh_ref, xo_ref, ho_ref, rows)


def _out_ln(a_list, w, x, mods, sub, sub_next, ln_w, ln_b, seq, tm=512):
    t, d = x.shape
    k = w.shape[0]
    tpb = seq // tm
    has_next = sub_next is not None
    in_specs = [pl.BlockSpec((tm, k), lambda i: (i, 0)) for _ in a_list]
    in_specs += [pl.BlockSpec((k, d), lambda i: (0, 0), pipeline_mode=pl.Buffered(1)),
                 pl.BlockSpec((tm, d), lambda i: (i, 0)),
                 _mod_spec(d, sub, GATE, tpb), _row_spec(d), _row_spec(d)]
    args = list(a_list) + [w, x, mods, ln_w.reshape(1, d), ln_b.reshape(1, d)]
    out_specs = [pl.BlockSpec((tm, d), lambda i: (i, 0))]
    out_shape = [jax.ShapeDtypeStruct((t, d), F32)]
    if has_next:
        in_specs += [_mod_spec(d, sub_next, SCALE, tpb), _mod_spec(d, sub_next, SHIFT, tpb)]
        args += [mods, mods]
        out_specs.append(pl.BlockSpec((tm, d), lambda i: (i, 0)))
        out_shape.append(jax.ShapeDtypeStruct((t, d), BF16))
    res = pl.pallas_call(
        functools.partial(_out_ln_kernel, len(a_list), has_next),
        grid=(t // tm,),
        in_specs=in_specs, out_specs=out_specs, out_shape=out_shape,
        compiler_params=_cparams("parallel"),
        name="out_ln",
    )(*args)
    return (res[0], res[1]) if has_next else (res[0], None)


MLP_EPILOGUE_CHUNKS = 2


MLP_WEIGHT_SLOTS = 3


def _mlp_kernel(has_next, n_i, n_j, h_ref, w1_hbm, w2_hbm, x_ref, gate_ref, lnw_ref, lnb_ref, *rest):
    if has_next:
        sc_ref, sh_ref, xo_ref, ho_ref, acc_ref, w1_buf, w2_buf, sem = rest
    else:
        (xo_ref, acc_ref, w1_buf, w2_buf, sem), sc_ref, sh_ref, ho_ref = rest, None, None, None
    j = pl.program_id(1)
    last = n_j - 1
    tm = h_ref.shape[0]
    tf = w1_buf.shape[-1]
    step = pl.program_id(0) * n_j + j
    n_steps = n_i * n_j
    ahead = MLP_WEIGHT_SLOTS - 1

    def weight_copies(s):
        slot = s % MLP_WEIGHT_SLOTS
        col = (s % n_j) * tf
        if not isinstance(s, int):
            col = pl.multiple_of(col, tf)
        return (pltpu.make_async_copy(w1_hbm.at[:, pl.ds(col, tf)], w1_buf.at[slot], sem.at[0, slot]),
                pltpu.make_async_copy(w2_hbm.at[pl.ds(col, tf), :], w2_buf.at[slot], sem.at[1, slot]))

    @pl.when(step == 0)
    def _():
        for s in range(ahead):
            for cp in weight_copies(s):
                cp.start()

    @pl.when(step + ahead < n_steps)
    def _():
        for cp in weight_copies(step + ahead):
            cp.start()

    for cp in weight_copies(step):
        cp.wait()
    slot = step % MLP_WEIGHT_SLOTS

    def partial_out(rows):
        hid = _dot(h_ref[rows, :], w1_buf[slot])
        hid = jnp.square(jnp.maximum(hid, 0.0)).astype(BF16)
        return _dot(hid, w2_buf[slot])

    @pl.when(j == 0)
    def _():
        acc_ref[...] = partial_out(slice(None))

    @pl.when(jnp.logical_and(j > 0, j < last))
    def _():
        acc_ref[...] += partial_out(slice(None))

    @pl.when(j == last)
    def _():
        rc = tm // MLP_EPILOGUE_CHUNKS
        for c in range(MLP_EPILOGUE_CHUNKS):
            rows = slice(c * rc, (c + 1) * rc)
            y = acc_ref[rows, :] + partial_out(rows)
            _ln_mod_store(y, x_ref, gate_ref, lnw_ref, lnb_ref, sc_ref, sh_ref, xo_ref, ho_ref, rows)


def _mlp(h, w1, w2, x, mods, sub, sub_next, ln_w, ln_b, seq, tm=512, tf=1024):
    t, d = x.shape
    ff = w1.shape[1]
    assert ff // tf >= 2 and (t // tm) * (ff // tf) >= MLP_WEIGHT_SLOTS
    tpb = seq // tm
    has_next = sub_next is not None
    in_specs = [pl.BlockSpec((tm, d), lambda i, j: (i, 0)),
                pl.BlockSpec(memory_space=pl.ANY),
                pl.BlockSpec(memory_space=pl.ANY),
                pl.BlockSpec((tm, d), lambda i, j: (i, 0)),
                _mod_spec(d, sub, GATE, tpb), _row_spec(d), _row_spec(d)]
    args = [h, w1, w2, x, mods, ln_w.reshape(1, d), ln_b.reshape(1, d)]
    out_specs = [pl.BlockSpec((tm, d), lambda i, j: (i, 0))]
    out_shape = [jax.ShapeDtypeStruct((t, d), F32)]
    if has_next:
        in_specs += [_mod_spec(d, sub_next, SCALE, tpb), _mod_spec(d, sub_next, SHIFT, tpb)]
        args += [mods, mods]
        out_specs.append(pl.BlockSpec((tm, d), lambda i, j: (i, 0)))
        out_shape.append(jax.ShapeDtypeStruct((t, d), BF16))
    res = pl.pallas_call(
        functools.partial(_mlp_kernel, has_next, t // tm, ff // tf),
        grid=(t // tm, ff // tf),
        in_specs=in_specs, out_specs=out_specs, out_shape=out_shape,
        scratch_shapes=[pltpu.VMEM((tm, d), F32),
                        pltpu.VMEM((MLP_WEIGHT_SLOTS, d, tf), BF16), pltpu.VMEM((MLP_WEIGHT_SLOTS, tf, d), BF16),
                        pltpu.SemaphoreType.DMA((2, MLP_WEIGHT_SLOTS))],
        compiler_params=_cparams("arbitrary", "arbitrary"),
        name="mlp",
    )(*args)
    return (res[0], res[1]) if has_next else (res[0], None)


HALO = 8


def _lru_scan_kernel(ts, gy_ref, xb_ref, cw_ref, cb_ref, gw_ref, gb_ref, lam_ref, o_ref,
                     buf, a_scr, u_scr, h_scr):
    i = pl.program_id(1)
    width = xb_ref.shape[-1]
    bd = width // LRU_BLOCKS

    @pl.when(i == 0)
    def _():
        buf[0:HALO, :] = jnp.zeros((HALO, width), F32)
        h_scr[...] = jnp.zeros_like(h_scr)

    @pl.when(i > 0)
    def _():
        buf[0:HALO, :] = buf[ts:ts + HALO, :]

    buf[HALO:HALO + ts, :] = xb_ref[...]
    cw = cw_ref[...]
    taps = [buf[HALO - (CONV_WIDTH - 1) + k:HALO - (CONV_WIDTH - 1) + k + ts, :] * cw[k:k + 1, :]
            for k in range(CONV_WIDTH)]
    conv = taps[0]
    for tap in taps[1:]:
        conv = conv + tap
    xc = cb_ref[...] + conv
    xcb = xc.astype(BF16)
    gates = []
    for jg in range(2):
        cols = [_dot(xcb[:, n * bd:(n + 1) * bd], gw_ref[jg, n]) for n in range(LRU_BLOCKS)]
        gates.append(jnp.concatenate(cols, axis=1) + gb_ref[jg:jg + 1, :])
    r = 0.5 * (1.0 + jnp.tanh(0.5 * gates[0]))
    ig = 0.5 * (1.0 + jnp.tanh(0.5 * gates[1]))
    neg_lam = -lam_ref[...]
    softplus = jnp.maximum(neg_lam, 0.0) + jnp.log1p(jnp.exp(-jnp.abs(neg_lam)))
    log_a = -LRU_C * r * softplus
    a = jnp.exp(log_a)
    a_scr[...] = a
    one_m_a2 = -jnp.tanh(log_a) * (1.0 + a * a)
    root = one_m_a2 * lax.rsqrt(jnp.maximum(one_m_a2, F32_TINY))
    u_scr[...] = root * (ig * xc)

    def step(t, h):
        h = a_scr[pl.ds(t, 1), :] * h + u_scr[pl.ds(t, 1), :]
        u_scr[pl.ds(t, 1), :] = h
        return h

    h_scr[...] = lax.fori_loop(0, ts, step, h_scr[...], unroll=8)
    o_ref[...] = (gy_ref[...] * u_scr[...]).astype(BF16)


def _lru_scan(gy, xb, conv_w, conv_b, gate_w, gate_b, lam, ts=512):
    bsz, seq, width = xb.shape
    bd = width // LRU_BLOCKS
    tile = pl.BlockSpec((None, ts, width), lambda b, i: (b, i, 0))
    return pl.pallas_call(
        functools.partial(_lru_scan_kernel, ts),
        grid=(bsz, seq // ts),
        in_specs=[tile, tile,
                  pl.BlockSpec((CONV_WIDTH, width), lambda b, i: (0, 0)),
                  pl.BlockSpec((1, width), lambda b, i: (0, 0)),
                  pl.BlockSpec((2, LRU_BLOCKS, bd, bd), lambda b, i: (0, 0, 0, 0)),
                  pl.BlockSpec((2, width), lambda b, i: (0, 0)),
                  pl.BlockSpec((1, width), lambda b, i: (0, 0))],
        out_specs=tile,
        out_shape=jax.ShapeDtypeStruct((bsz, seq, width), BF16),
        scratch_shapes=[pltpu.VMEM((ts + HALO, width), F32), pltpu.VMEM((ts, width), F32),
                        pltpu.VMEM((ts, width), F32), pltpu.VMEM((1, width), F32)],
        compiler_params=_cparams("parallel", "arbitrary"),
        name="lru_scan",
    )(gy, xb, conv_w, conv_b.reshape(1, width), gate_w.astype(BF16), gate_b, lam.reshape(1, width))


def _lru_mixer(h, bsz, seq, w_in, conv_w, conv_b, gate_w, gate_b, lam):
    width = w_in.shape[1] // 2
    gy, xb = _proj(h, w_in.astype(BF16),
                   [(width, lambda acc: _gelu_tanh(acc), BF16), (width, lambda acc: acc, F32)],
                   tn=1024, name="lru_proj")
    a = _lru_scan(gy.reshape(bsz, seq, width), xb.reshape(bsz, seq, width),
                  conv_w, conv_b, gate_w, gate_b, lam)
    return [a.reshape(bsz * seq, width)]


def _t5_bucket_np(dist):
    n = np.maximum(np.asarray(dist, np.int64), 0)
    max_exact = T5_BUCKETS // 2
    n_large = T5_BUCKETS - max_exact
    ratio = T5_MAX_DIST // max_exact
    thresholds = []
    for k in range(1, n_large):
        m = max_exact
        while m ** n_large < ratio ** k * max_exact ** n_large:
            m += 1
        thresholds.append(m)
    large = max_exact + sum((n >= th).astype(np.int64) for th in thresholds)
    return np.where(n < max_exact, n, large).astype(np.int32)


def _bias_table_kernel(sub_far, rb_ref, idx_ref, o_ref):
    hd = pl.program_id(0)
    idx = idx_ref[...]
    base = rb_ref[T5_BUCKETS - 1, hd] if sub_far else 0.0
    acc = jnp.full(idx.shape, NEG_MASK, F32)
    for b in range(T5_BUCKETS):
        acc = jnp.where(idx == b, (rb_ref[b, hd] - base) * LOG2E, acc)
    o_ref[...] = acc


def _bias_table(rel_bias, idx_np, sub_far, name):
    rows, cols = idx_np.shape
    tr = 128
    n_heads = rel_bias.shape[1]
    return pl.pallas_call(
        functools.partial(_bias_table_kernel, sub_far),
        grid=(n_heads, rows // tr),
        in_specs=[pl.BlockSpec(memory_space=pltpu.SMEM),
                  pl.BlockSpec((tr, cols), lambda hd, i: (i, 0))],
        out_specs=pl.BlockSpec((None, tr, cols), lambda hd, i: (hd, i, 0)),
        out_shape=jax.ShapeDtypeStruct((n_heads, rows, cols), F32),
        compiler_params=_cparams("parallel", "parallel"),
        name=name,
    )(rel_bias, jnp.asarray(idx_np))


def _compress_kernel(seq, x_ref, pe_ref, w1_ref, b1_ref, w2_ref, o_ref, xs):
    xs[0:seq, :] = x_ref[...]
    xs[seq:CMP_ROWS, :] = jnp.zeros((CMP_ROWS - seq, LANES), F32)
    acc = jnp.zeros((N_CMP_SLOTS, LANES), F32)
    for l in range(CMP_BLOCK):
        rows = xs[pl.ds(l, N_CMP_SLOTS, stride=CMP_STRIDE), :] + pe_ref[l:l + 1, :]
        acc = acc + _dot(rows.astype(BF16), w1_ref[l])
    hid = _gelu_tanh(acc + b1_ref[...])
    o_ref[...] = _dot(hid.astype(BF16), w2_ref[...]).astype(BF16)


def _compress(cmp_tok, pe, w1, b1, w2):
    bsz, seq, _ = cmp_tok.shape
    g = NSA_KV_GROUPS
    dh = NSA_HEAD_DIM
    return pl.pallas_call(
        functools.partial(_compress_kernel, seq),
        grid=(bsz, 2, g),
        in_specs=[pl.BlockSpec((None, seq, dh), lambda b, j, gi: (b, 0, j * g + gi)),
                  pl.BlockSpec((None, CMP_BLOCK, dh), lambda b, j, gi: (j, 0, 0)),
                  pl.BlockSpec((None, CMP_BLOCK, dh, dh), lambda b, j, gi: (j, 0, 0, 0)),
                  pl.BlockSpec((None, 1, dh), lambda b, j, gi: (j, 0, 0)),
                  pl.BlockSpec((None, dh, dh), lambda b, j, gi: (j, 0, 0))],
        out_specs=pl.BlockSpec((None, None, None, N_CMP_SLOTS, dh), lambda b, j, gi: (b, j, gi, 0, 0)),
        out_shape=jax.ShapeDtypeStruct((bsz, 2, g, N_CMP_SLOTS, dh), BF16),
        scratch_shapes=[pltpu.VMEM((CMP_ROWS, dh), F32)],
        compiler_params=_cparams("parallel", "parallel", "parallel"),
        name="nsa_compress",
    )(cmp_tok, pe, w1.astype(BF16), b1.reshape(2, 1, dh), w2.astype(BF16))


CMP_COLS = N_CMP_SLOTS // LANES
CMP_COL_TOKENS = CMP_STRIDE * LANES


def _cmp_table_idx(tq):
    ql = np.arange(tq)[:, None]
    m = np.arange(N_CMP_SLOTS)[None, :]
    e = np.where(m <= (tq - CMP_BLOCK) // CMP_STRIDE, -m, N_CMP_SLOTS - m)
    dist = CMP_STRIDE * e + ql - (CMP_BLOCK - 1)
    return _t5_bucket_np(dist)


def _sel_sum_matrix():
    i = np.arange(N_CMP_SLOTS)[:, None]
    j = np.arange(LANES)[None, :]
    ratio = SEL_BLOCK // CMP_STRIDE
    return ((i >= ratio * j - 1) & (i <= ratio * j + ratio - 1)).astype(np.float32)


CMP_SUB = 2


def _cmp_attn_kernel(tq, q_ref, kc_ref, vc_ref, tb_ref, sm_ref, g_ref, o_ref, val_ref):
    i = pl.program_id(2)
    lane = lax.broadcasted_iota(jnp.int32, (tq, LANES), 1)
    row = lax.broadcasted_iota(jnp.int32, (tq, LANES), 0)

    def body(nc):
        width = nc * LANES
        kc = kc_ref[0:width, :]
        vc = vc_ref[0:width, :]
        ql = lax.broadcasted_iota(jnp.int32, (tq, width), 0)
        blk = lax.broadcasted_iota(jnp.int32, (tq, width), 1)
        rel = ql - CMP_STRIDE * blk - (CMP_BLOCK - 1)
        for sub in range(CMP_SUB):
            rows = slice(sub * tq, (sub + 1) * tq)
            tile = i * CMP_SUB + sub
            t0 = tile * tq
            gate = g_ref[rows, :]
            causal = jnp.where(t0 + rel >= 0, 0.0, NEG_MASK)
            row_ok = (t0 + lax.broadcasted_iota(jnp.int32, (tq, 1), 0)) >= CMP_BLOCK - 1
            shift = (tile * (tq // CMP_STRIDE)) % LANES
            keep = lane >= shift
            p_sum = jnp.zeros((tq, width), F32)
            for hd in range(NSA_HPG):
                rot = {}
                for c in range(nc):
                    for k in ((c - (nc - 1)) % CMP_COLS, (c - nc) % CMP_COLS):
                        if k not in rot:
                            rot[k] = pltpu.roll(tb_ref[hd, :, k * LANES:(k + 1) * LANES], shift, axis=1)
                tb = jnp.concatenate(
                    [jnp.where(keep, rot[(c - (nc - 1)) % CMP_COLS], rot[(c - nc) % CMP_COLS])
                     for c in range(nc)], axis=1)
                s = _dot_nt(q_ref[rows, hd * LANES:(hd + 1) * LANES], kc) + tb + causal
                e = jnp.exp2(s - jnp.max(s, axis=-1, keepdims=True))
                p = e * jnp.where(row_ok, 1.0 / jnp.sum(e, axis=-1, keepdims=True), 0.0)
                p_sum = p_sum + p
                o_ref[rows, hd * LANES:(hd + 1) * LANES] = (
                    _dot(p.astype(BF16), vc) * gate[:, hd:hd + 1]).astype(o_ref.dtype)
            sm = sm_ref[0:width, :]
            hi = p_sum.astype(BF16)
            r1 = p_sum - hi.astype(F32)
            mid = r1.astype(BF16)
            lo = (r1 - mid.astype(F32)).astype(BF16)
            score = (_dot(hi, sm) + _dot(mid, sm)) + _dot(lo, sm)
            cur = (t0 + row) >> 6
            forced = jnp.logical_or(lane == 0, jnp.logical_or(lane == cur, lane == cur - 1))
            val_ref[rows, :] = jnp.where(forced, SEL_FORCE, jnp.where(lane <= cur, score, -SEL_FORCE))

    last_q = (i + 1) * (CMP_SUB * tq) - 1
    n_cols = jnp.minimum((last_q - (CMP_BLOCK - 1)) // CMP_COL_TOKENS + 1, CMP_COLS)
    for nc in range(1, CMP_COLS + 1):
        pl.when(n_cols == nc)(functools.partial(body, nc))


N_FORCED = 3


def _topk_mask_kernel(n_top, val_ref, m_ref):
    val0 = val_ref[...]
    lane = lax.broadcasted_iota(jnp.int32, val0.shape, 1).astype(F32)
    forced = val0 >= 0.5 * SEL_FORCE
    val = jnp.where(forced, -jnp.inf, val0)
    sel = jnp.where(forced, 1.0, 0.0)
    for _ in range(n_top - N_FORCED):
        mx = jnp.max(val, axis=-1, keepdims=True)
        first = jnp.min(jnp.where(val == mx, lane, float(LANES)), axis=-1, keepdims=True)
        pick = lane == first
        sel = jnp.where(pick, 1.0, sel)
        val = jnp.where(pick, -jnp.inf, val)
    m_ref[...] = jnp.where(jnp.logical_and(sel > 0.0, val0 >= 0.0), 0.0, NEG_MASK).astype(BF16)


def _topk_mask(val, n_top, tr=1024):
    rows = val.shape[0]
    return pl.pallas_call(
        functools.partial(_topk_mask_kernel, n_top),
        grid=(rows // tr,),
        in_specs=[pl.BlockSpec((tr, LANES), lambda i: (i, 0))],
        out_specs=pl.BlockSpec((tr, LANES), lambda i: (i, 0)),
        out_shape=jax.ShapeDtypeStruct((rows, LANES), BF16),
        compiler_params=_cparams("parallel"),
        name="nsa_topk_mask",
    )(val)


def _cmp_attn(q, kvc, table, gates, tq=256):
    bsz, seq, _ = q.shape
    g = NSA_KV_GROUPS
    qw = NSA_HPG * NSA_HEAD_DIM
    rows = tq * CMP_SUB
    assert CMP_COL_TOKENS % rows == 0 and tq % CMP_STRIDE == 0
    return pl.pallas_call(
        functools.partial(_cmp_attn_kernel, tq),
        grid=(bsz, g, seq // rows),
        in_specs=[pl.BlockSpec((None, rows, qw), lambda b, gi, i: (b, i, gi)),
                  pl.BlockSpec((None, None, None, N_CMP_SLOTS, NSA_HEAD_DIM), lambda b, gi, i: (b, 0, gi, 0, 0)),
                  pl.BlockSpec((None, None, None, N_CMP_SLOTS, NSA_HEAD_DIM), lambda b, gi, i: (b, 1, gi, 0, 0)),
                  pl.BlockSpec((NSA_HPG, tq, N_CMP_SLOTS), lambda b, gi, i: (gi, 0, 0)),
                  pl.BlockSpec((N_CMP_SLOTS, LANES), lambda b, gi, i: (0, 0)),
                  pl.BlockSpec((None, rows, LANES), lambda b, gi, i: (b, i, gi))],
        out_specs=[pl.BlockSpec((None, rows, qw), lambda b, gi, i: (b, i, gi)),
                   pl.BlockSpec((None, None, rows, LANES), lambda b, gi, i: (b, gi, i, 0))],
        out_shape=[jax.ShapeDtypeStruct((bsz, seq, g * qw), BF16),
                   jax.ShapeDtypeStruct((bsz, g, seq, LANES), F32)],
        compiler_params=_cparams("parallel", "parallel", "parallel"),
        name="nsa_cmp_attn",
    )(q, kvc, kvc, table, jnp.asarray(_sel_sum_matrix(), BF16), gates)


SEL_TQ = 256
SEL_TK = 256
SEL_N_DELTA = -(-(T5_MAX_DIST - 1 + SEL_TK) // SEL_TQ)
SEL_N_NEAR = -(-SEL_N_DELTA // (SEL_TK // SEL_TQ))
SEL_STREAMS = 4


def _sel_table_idx():
    d = np.arange(SEL_N_DELTA)[:, None, None]
    ql = np.arange(SEL_TQ)[None, :, None]
    kl = np.arange(SEL_TK)[None, None, :]
    dist = ql - kl + SEL_TQ * d
    idx = np.where(dist >= 0, _t5_bucket_np(dist), MASK_BUCKET)
    return idx.reshape(SEL_N_DELTA * SEL_TQ, SEL_TK).astype(np.int32)


def _sel_attn_kernel(q_ref, m_ref, k_ref, v_ref, oh_ref, nb_ref, g_ref, o_ref, qa, m_scr, acc):
    tq, tk = SEL_TQ, SEL_TK
    ratio = tk // tq
    i = pl.program_id(2)
    kd = i // ratio
    sel_mask = m_ref[...]
    for hd in range(NSA_HPG):
        qa[hd * tq:(hd + 1) * tq, 0:LANES] = q_ref[:, hd * LANES:(hd + 1) * LANES]
        qa[hd * tq:(hd + 1) * tq, LANES:2 * LANES] = sel_mask
    m_scr[...] = jnp.full(m_scr.shape, -jnp.inf, F32)
    acc[...] = jnp.zeros_like(acc)
    n_kt = k_ref.shape[0] // tk
    ones = jnp.ones((tk, LANES), BF16)

    def update(st, tiles):
        scores, values = [], []
        for kt, valid, delta in tiles:
            k0 = pl.multiple_of(jnp.clip(kt, 0, n_kt - 1) * tk, tk)
            ka = jnp.concatenate([k_ref[pl.ds(k0, tk), :], oh_ref[pl.ds(k0, tk), :]], axis=1)
            s = _dot_nt(qa[...], ka)
            if valid is not None:
                s = s + jnp.where(valid, 0.0, NEG_MASK)
            if delta is not None:
                s = s + jnp.concatenate([nb_ref[hd, delta] for hd in range(NSA_HPG)], axis=0)
            scores.append(s)
            values.append(jnp.concatenate([v_ref[pl.ds(k0, tk), :], ones], axis=1))
        s_max = scores[0]
        for s in scores[1:]:
            s_max = jnp.maximum(s_max, s)
        m_prev = m_scr[st]
        m_new = jnp.maximum(m_prev, jnp.max(s_max, axis=-1, keepdims=True))
        alpha = jnp.exp2(m_prev - m_new)
        m_rep = jnp.concatenate([m_new] * (tk // LANES), axis=1)
        pv = None
        for s, va in zip(scores, values):
            part = _dot(jnp.exp2(s - m_rep).astype(BF16), va)
            pv = part if pv is None else pv + part
        acc[st] = jnp.concatenate([alpha, alpha], axis=1) * acc[st] + pv
        m_scr[st] = m_new

    n_far = jnp.maximum(kd - (SEL_N_NEAR - 1), 0)
    pair = 2 * SEL_STREAMS

    def far_pairs(it, carry):
        for st in range(SEL_STREAMS):
            kt = it * pair + 2 * st
            update(st, [(kt, None, None), (kt + 1, None, None)])
        return carry

    lax.fori_loop(0, n_far // pair, far_pairs, 0)
    base = (n_far // pair) * pair
    rem = n_far - base

    @pl.when(rem >= SEL_STREAMS)
    def _():
        for st in range(SEL_STREAMS):
            update(st, [(base + st, None, None)])

    base = base + jnp.where(rem >= SEL_STREAMS, SEL_STREAMS, 0)
    left = n_far - base
    slots = [(base + f, f < left, None) for f in range(SEL_STREAMS - 1)]
    for back in range(SEL_N_NEAR - 1, -1, -1):
        kt = kd - back
        slots.append((kt, kt >= 0, (i - kd * ratio) + back * ratio))
    for st in range(SEL_STREAMS):
        update(st, slots[st::SEL_STREAMS])

    m_all = m_scr[0]
    for st in range(1, SEL_STREAMS):
        m_all = jnp.maximum(m_all, m_scr[st])
    tot = jnp.zeros((NSA_HPG * tq, 2 * LANES), F32)
    for st in range(SEL_STREAMS):
        w = jnp.exp2(m_scr[st] - m_all)
        tot = tot + jnp.concatenate([w, w], axis=1) * acc[st]
    gate = g_ref[...]
    out = tot[:, :LANES] / jnp.maximum(tot[:, LANES:], 1e-30)
    for hd in range(NSA_HPG):
        o_ref[:, hd * LANES:(hd + 1) * LANES] = (
            out[hd * tq:(hd + 1) * tq, :] * gate[:, NSA_HPG + hd:NSA_HPG + hd + 1]).astype(o_ref.dtype)


def _sel_attn(q, sel_mask, kv_tok, onehot, table, gates):
    bsz, seq, _ = q.shape
    g = NSA_KV_GROUPS
    dh = NSA_HEAD_DIM
    qw = NSA_HPG * dh
    tq, tk = SEL_TQ, SEL_TK
    rows = NSA_HPG * tq
    return pl.pallas_call(
        _sel_attn_kernel,
        grid=(bsz, g, seq // tq),
        in_specs=[pl.BlockSpec((None, tq, qw), lambda b, gi, i: (b, i, gi)),
                  pl.BlockSpec((None, None, tq, LANES), lambda b, gi, i: (b, gi, i, 0)),
                  pl.BlockSpec((None, seq, dh), lambda b, gi, i: (b, 0, gi)),
                  pl.BlockSpec((None, seq, dh), lambda b, gi, i: (b, 0, g + gi)),
                  pl.BlockSpec((seq, LANES), lambda b, gi, i: (0, 0)),
                  pl.BlockSpec((NSA_HPG, SEL_N_DELTA, tq, tk), lambda b, gi, i: (gi, 0, 0, 0)),
                  pl.BlockSpec((None, tq, LANES), lambda b, gi, i: (b, i, gi))],
        out_specs=pl.BlockSpec((None, tq, qw), lambda b, gi, i: (b, i, gi)),
        out_shape=jax.ShapeDtypeStruct((bsz, seq, g * qw), BF16),
        scratch_shapes=[pltpu.VMEM((rows, 2 * LANES), BF16), pltpu.VMEM((SEL_STREAMS, rows, LANES), F32),
                        pltpu.VMEM((SEL_STREAMS, rows, 2 * dh), F32)],
        compiler_params=_cparams("parallel", "parallel", "parallel"),
        name="nsa_sel_attn",
    )(q, sel_mask, kv_tok, kv_tok, onehot, table, gates)


WIN_TQ = 128
WIN_SUB = 4
WIN_KEYS = WINDOW + WIN_TQ


def _win_table_idx():
    ql = np.arange(WIN_TQ)[:, None]
    kl = np.arange(WIN_KEYS)[None, :]
    dist = ql + WINDOW - kl
    return np.where((dist >= 0) & (dist < WINDOW), _t5_bucket_np(dist), MASK_BUCKET).astype(np.int32)


def _win_attn_kernel(seq, q_ref, k_ref, v_ref, wb_ref, g_ref, o_ref, kpad, vpad, qs):
    tq = WIN_TQ
    i = pl.program_id(2)

    @pl.when(i == 0)
    def _():
        kpad[0:WINDOW, :] = jnp.zeros((WINDOW, LANES), BF16)
        vpad[0:WINDOW, :] = jnp.zeros((WINDOW, LANES), BF16)
        kpad[WINDOW:WINDOW + seq, :] = k_ref[...]
        vpad[WINDOW:WINDOW + seq, :] = v_ref[...]

    bias = wb_ref[...].reshape(NSA_HPG * tq, WIN_KEYS)

    def step(header_visible):
        for sub in range(WIN_SUB):
            rows = slice(sub * tq, (sub + 1) * tq)
            t0 = pl.multiple_of((i * WIN_SUB + sub) * tq, tq)
            for hd in range(NSA_HPG):
                qs[sub, hd * tq:(hd + 1) * tq, :] = q_ref[rows, hd * LANES:(hd + 1) * LANES]
            keys = kpad[pl.ds(t0, WIN_KEYS), :]
            vals = vpad[pl.ds(t0, WIN_KEYS), :]
            s = _dot_nt(qs[sub], keys) + bias
            if header_visible:
                kl = lax.broadcasted_iota(jnp.int32, s.shape, 1)
                s = jnp.where(kl >= WINDOW - t0, s, NEG_MASK)
            e = jnp.exp2(s - jnp.max(s, axis=-1, keepdims=True))
            out = _dot(e.astype(BF16), vals) * (1.0 / jnp.sum(e, axis=-1, keepdims=True))
            gate = g_ref[rows, :]
            for hd in range(NSA_HPG):
                o_ref[rows, hd * LANES:(hd + 1) * LANES] = (
                    out[hd * tq:(hd + 1) * tq, :]
                    * gate[:, 2 * NSA_HPG + hd:2 * NSA_HPG + hd + 1]).astype(o_ref.dtype)

    n_header_steps = -(-WINDOW // (WIN_SUB * tq))
    pl.when(i < n_header_steps)(functools.partial(step, True))
    pl.when(i >= n_header_steps)(functools.partial(step, False))


def _win_attn(q, kv_tok, table, gates):
    bsz, seq, _ = q.shape
    g = NSA_KV_GROUPS
    dh = NSA_HEAD_DIM
    qw = NSA_HPG * dh
    tq = WIN_TQ
    rows = tq * WIN_SUB
    return pl.pallas_call(
        functools.partial(_win_attn_kernel, seq),
        grid=(bsz, g, seq // rows),
        in_specs=[pl.BlockSpec((None, rows, qw), lambda b, gi, i: (b, i, gi)),
                  pl.BlockSpec((None, seq, dh), lambda b, gi, i: (b, 0, 2 * g + gi)),
                  pl.BlockSpec((None, seq, dh), lambda b, gi, i: (b, 0, 3 * g + gi)),
                  pl.BlockSpec((NSA_HPG, tq, WIN_KEYS), lambda b, gi, i: (gi, 0, 0)),
                  pl.BlockSpec((None, rows, LANES), lambda b, gi, i: (b, i, gi))],
        out_specs=pl.BlockSpec((None, rows, qw), lambda b, gi, i: (b, i, gi)),
        out_shape=jax.ShapeDtypeStruct((bsz, seq, g * qw), BF16),
        scratch_shapes=[pltpu.VMEM((seq + WINDOW, dh), BF16), pltpu.VMEM((seq + WINDOW, dh), BF16),
                        pltpu.VMEM((WIN_SUB, NSA_HPG * tq, dh), BF16)],
        compiler_params=_cparams("parallel", "parallel", "arbitrary"),
        name="nsa_win_attn",
    )(q, kv_tok, kv_tok, table, gates)


def _nsa_mixer(h, bsz, seq, rel_bias, w_in, cmp_pe, cmp_w1, cmp_b1, cmp_w2):
    g, hpg, dh = NSA_KV_GROUPS, NSA_HPG, NSA_HEAD_DIM
    qw = NSA_HEADS * dh
    kvw = g * dh
    n_main = qw + 2 * N_BRANCH * kvw
    scale = dh ** -0.5 * LOG2E
    q, cmp_tok, kv_tok = _proj(
        h, w_in[:, :n_main].astype(BF16),
        [(qw, lambda acc: acc * scale, BF16), (2 * kvw, lambda acc: acc, F32), (4 * kvw, lambda acc: acc, BF16)],
        tn=1024, name="nsa_proj")
    src = np.zeros((g * LANES,), np.int32)
    valid = np.zeros((g * LANES,), bool)
    for gi in range(g):
        for br in range(N_BRANCH):
            for hd in range(hpg):
                src[gi * LANES + br * hpg + hd] = n_main + br * NSA_HEADS + gi * hpg + hd
                valid[gi * LANES + br * hpg + hd] = True
    w_gate = jnp.where(jnp.asarray(valid)[None, :], w_in[:, src], 0.0).astype(BF16)
    (gates,) = _proj(h, w_gate, [(g * LANES, lambda acc: _sigmoid(acc), F32)], name="nsa_gate_proj")

    q = q.reshape(bsz, seq, qw)
    gates = gates.reshape(bsz, seq, g * LANES)
    kv_tok = kv_tok.reshape(bsz, seq, 4 * kvw)
    kvc = _compress(cmp_tok.reshape(bsz, seq, 2 * kvw), cmp_pe, cmp_w1, cmp_b1, cmp_w2)

    n_top = min(SEL_TOP_N, seq // SEL_BLOCK)
    cmp_tq = 256
    cmp_table = _bias_table(rel_bias, _cmp_table_idx(cmp_tq), False, "nsa_cmp_bias")
    o_c, sel_val = _cmp_attn(q, kvc, cmp_table, gates, cmp_tq)
    sel_mask = _topk_mask(sel_val.reshape(bsz * g * seq, LANES), n_top).reshape(bsz, g, seq, LANES)

    sel_table = _bias_table(rel_bias, _sel_table_idx(), True, "nsa_sel_bias")
    sel_table = sel_table.reshape(NSA_HEADS, SEL_N_DELTA, SEL_TQ, SEL_TK)
    onehot = (jnp.arange(seq, dtype=jnp.int32)[:, None] // SEL_BLOCK
              == jnp.arange(LANES, dtype=jnp.int32)[None, :]).astype(BF16)
    o_s = _sel_attn(q, sel_mask, kv_tok, onehot, sel_table, gates)

    win_table = _bias_table(rel_bias, _win_table_idx(), False, "nsa_win_bias")
    o_w = _win_attn(q, kv_tok, win_table, gates)
    t = bsz * seq
    return [o_c.reshape(t, qw), o_s.reshape(t, qw), o_w.reshape(t, qw)]


def _rotary_epi(scale, acc, cos, sin):
    half = cos.shape[-1]
    outs = []
    for hd in range(acc.shape[1] // (2 * half)):
        x1 = acc[:, 2 * hd * half:(2 * hd + 1) * half]
        x2 = acc[:, (2 * hd + 1) * half:(2 * hd + 2) * half]
        outs += [x1 * cos - x2 * sin, x1 * sin + x2 * cos]
    rot = jnp.concatenate(outs, axis=1)
    return rot * scale if scale != 1.0 else rot


RET_HPS = 2


def _ret_kernel(tc, q_ref, k_ref, v_ref, g_ref, dec_ref, xi_ref, zeta_ref, cd_ref, gw_ref, gb_ref,
                o_ref, state):
    i = pl.program_id(2)
    dk = q_ref.shape[-1] // RET_HPS
    dv = v_ref.shape[-1] // RET_HPS

    @pl.when(i == 0)
    def _():
        state[...] = jnp.zeros_like(state)

    for c in range(tc // RET_CHUNK):
        rows = slice(c * RET_CHUNK, (c + 1) * RET_CHUNK)
        for hh in range(RET_HPS):
            qk_cols = slice(hh * dk, (hh + 1) * dk)
            v_cols = slice(hh * dv, (hh + 1) * dv)
            qc = q_ref[rows, qk_cols]
            kc = k_ref[rows, qk_cols]
            vc = v_ref[rows, v_cols]
            inner = _dot_nt(qc, kc) * dec_ref[hh]
            st = state[hh]
            out = _dot(inner.astype(BF16), vc) + _dot(qc, st.astype(BF16)) * xi_ref[hh]
            kz = (kc.astype(F32) * zeta_ref[hh]).astype(BF16)
            state[hh] = st * cd_ref[hh] + _dot_tn(kz, vc)
            mu = jnp.mean(out, axis=-1, keepdims=True)
            oc = out - mu
            var = jnp.mean(oc * oc, axis=-1, keepdims=True)
            y = (oc * lax.rsqrt(var + GN_EPS)) * gw_ref[:, v_cols] + gb_ref[:, v_cols]
            o_ref[rows, v_cols] = (g_ref[rows, v_cols] * y).astype(BF16)


def _ret_core(q, k, v, g, gn_w, gn_b, tc=512):
    bsz, seq, qk_total = q.shape
    v_total = v.shape[-1]
    nh = RET_HEADS
    dk = qk_total // nh
    dv = v_total // nh
    cs = RET_CHUNK
    hps = RET_HPS
    log_g = jnp.log1p(-jnp.exp2(-5.0 - jnp.arange(nh, dtype=F32)))
    idx = jnp.arange(cs, dtype=F32)
    diff = idx[:, None] - idx[None, :]
    decay = jnp.where(diff >= 0, jnp.exp(jnp.maximum(diff, 0.0) * log_g[:, None, None]), 0.0)
    xi = jnp.exp((idx + 1.0) * log_g[:, None])[:, :, None]
    zeta = jnp.exp((cs - 1.0 - idx) * log_g[:, None])[:, :, None]
    chunk_decay = jnp.exp(cs * log_g).reshape(nh, 1, 1)
    return pl.pallas_call(
        functools.partial(_ret_kernel, tc),
        grid=(bsz, nh // hps, seq // tc),
        in_specs=[pl.BlockSpec((None, tc, hps * dk), lambda b, hp, i: (b, i, hp)),
                  pl.BlockSpec((None, tc, hps * dk), lambda b, hp, i: (b, i, hp)),
                  pl.BlockSpec((None, tc, hps * dv), lambda b, hp, i: (b, i, hp)),
                  pl.BlockSpec((None, tc, hps * dv), lambda b, hp, i: (b, i, hp)),
                  pl.BlockSpec((hps, cs, cs), lambda b, hp, i: (hp, 0, 0)),
                  pl.BlockSpec((hps, cs, 1), lambda b, hp, i: (hp, 0, 0)),
                  pl.BlockSpec((hps, cs, 1), lambda b, hp, i: (hp, 0, 0)),
                  pl.BlockSpec((hps, 1, 1), lambda b, hp, i: (hp, 0, 0)),
                  pl.BlockSpec((1, hps * dv), lambda b, hp, i: (0, hp)),
                  pl.BlockSpec((1, hps * dv), lambda b, hp, i: (0, hp))],
        out_specs=pl.BlockSpec((None, tc, hps * dv), lambda b, hp, i: (b, i, hp)),
        out_shape=jax.ShapeDtypeStruct((bsz, seq, v_total), BF16),
        scratch_shapes=[pltpu.VMEM((hps, dk, dv), F32)],
        compiler_params=_cparams("parallel", "parallel", "arbitrary"),
        name="ret_core",
    )(q, k, v, g, decay, xi, zeta, chunk_decay, gn_w.reshape(1, v_total), gn_b.reshape(1, v_total))


def _ret_mixer(h, bsz, seq, w_in, gn_w, gn_b, tm=1024):
    qk_total = w_in.shape[0]
    v_total = (w_in.shape[1] - 2 * qk_total) // 2
    dk = qk_total // RET_HEADS
    half = dk // 2
    pos = jnp.arange(seq, dtype=F32)
    inv = ROPE_BASE ** (-jnp.arange(half, dtype=F32) / half)
    ang = pos[:, None] * inv[None, :]
    cos, sin = jnp.cos(ang), jnp.sin(ang)
    tpb = seq // tm
    rope_spec = pl.BlockSpec((tm, half), lambda i, j: (i % tpb, 0))
    q, k, v, g = _proj(
        h, w_in.astype(BF16),
        [(qk_total, functools.partial(_rotary_epi, 1.0), BF16),
         (qk_total, functools.partial(_rotary_epi, dk ** -0.5), BF16),
         (v_total, lambda acc, cos, sin: acc, BF16),
         (v_total, lambda acc, cos, sin: _silu(acc), BF16)],
        extra=(cos, sin), extra_specs=(rope_spec, rope_spec), tm=tm, tn=1024, name="ret_proj")
    a = _ret_core(q.reshape(bsz, seq, qk_total), k.reshape(bsz, seq, qk_total),
                  v.reshape(bsz, seq, v_total), g.reshape(bsz, seq, v_total), gn_w, gn_b)
    return [a.reshape(bsz * seq, v_total)]


def kernel(x, c, rel_bias, ada_w, ada_b, ln_w, ln_b, mlp_w1, mlp_w2, lru_w_in, lru_conv_w, lru_conv_b,
           lru_gate_w, lru_gate_b, lru_lambda, lru_w_out, nsa_w_in, nsa_cmp_pe, nsa_cmp_w1, nsa_cmp_b1,
           nsa_cmp_w2, nsa_w_out, ret_w_in, ret_gn_w, ret_gn_b, ret_w_out):
    bsz, seq, d = x.shape
    depth = ada_w.shape[0]
    mods = _ada_mods(c, ada_w, ada_b)
    xf = x.reshape(bsz * seq, d)
    h = _modulate(xf, mods, 0, seq)
    for layer in range(depth):
        mixer, inst = layer % N_MIXERS, layer // N_MIXERS
        sub = 2 * layer
        if mixer == 0:
            a_list = _lru_mixer(h, bsz, seq, lru_w_in[inst], lru_conv_w[inst], lru_conv_b[inst],
                                lru_gate_w[inst], lru_gate_b[inst], lru_lambda[inst])
            w_out = lru_w_out[inst]
        elif mixer == 1:
            a_list = _nsa_mixer(h, bsz, seq, rel_bias, nsa_w_in[inst], nsa_cmp_pe[inst], nsa_cmp_w1[inst],
                                nsa_cmp_b1[inst], nsa_cmp_w2[inst])
            w_out = nsa_w_out[inst]
        else:
            a_list = _ret_mixer(h, bsz, seq, ret_w_in[inst], ret_gn_w[inst], ret_gn_b[inst])
            w_out = ret_w_out[inst]
        xf, h = _out_ln(a_list, w_out.astype(BF16), xf, mods, sub, sub + 1, ln_w[layer, 0], ln_b[layer, 0], seq)
        sub_next = sub + 2 if layer + 1 < depth else None
        xf, h = _mlp(h, mlp_w1[layer].astype(BF16), mlp_w2[layer].astype(BF16), xf, mods, sub + 1, sub_next,
                     ln_w[layer, 1], ln_b[layer, 1], seq)
    return xf.reshape(bsz, seq, d)
```

```python
import functools
import math

import numpy as np
import jax
import jax.numpy as jnp
from jax import lax
from jax.experimental import pallas as pl
from jax.experimental.pallas import tpu as pltpu

F32 = jnp.float32
BF16 = jnp.bfloat16

DN_DEPTH = 4
N_MIXERS = 3
DN_ALPHA = (2 * DN_DEPTH) ** 0.25
LN_EPS = 1e-5
GN_EPS = 1e-5
LRU_BLOCKS = 8
CONV_WIDTH = 4
LRU_C = 8.0
NSA_HEADS = 16
NSA_HEAD_DIM = 128
NSA_KV_GROUPS = 4
NSA_HPG = NSA_HEADS // NSA_KV_GROUPS
N_BRANCH = 3
CMP_BLOCK = 32
CMP_STRIDE = 16
SEL_BLOCK = 64
SEL_TOP_N = 16
SEL_FORCE = 1e6
WINDOW = 512
NEG_BIG = 1e30
T5_BUCKETS = 32
T5_MAX_DIST = 1024
RET_HEADS = 8
RET_CHUNK = 128
ROPE_BASE = 10000.0

LANES = 128
VMEM_LIMIT_BYTES = 56 * 2 ** 20

NEG_MASK = -(2.0 ** 100)
MASK_BUCKET = T5_BUCKETS
F32_TINY = float(np.finfo(np.float32).tiny)
LOG2E = math.log2(math.e)

N_CMP_SLOTS = 4 * LANES
CMP_ROWS = CMP_STRIDE * N_CMP_SLOTS + 4 * CMP_STRIDE


def _cparams(*sem):
    return pltpu.CompilerParams(dimension_semantics=sem, vmem_limit_bytes=VMEM_LIMIT_BYTES)


def _sigmoid(x):
    return 1.0 / (1.0 + jnp.exp(-x))


def _silu(x):
    return x * _sigmoid(x)


def _gelu_tanh(x):
    return 0.5 * x * (1.0 + jnp.tanh(math.sqrt(2.0 / math.pi) * (x + 0.044715 * (x * x * x))))


def _dot(a, b):
    return jnp.dot(a, b, preferred_element_type=F32)


def _dot_nt(a, b):
    return lax.dot_general(a, b, (((1,), (1,)), ((), ())), preferred_element_type=F32)


def _dot_tn(a, b):
    return lax.dot_general(a, b, (((0,), (0,)), ((), ())), preferred_element_type=F32)


def _ada_kernel(c_ref, w_ref, b_ref, o_ref):
    cond = _silu(c_ref[...]).astype(BF16)
    o_ref[...] = _dot(cond, w_ref[...].astype(BF16)) + b_ref[...]


def _ada_mods(c, ada_w, ada_b):
    depth, _, d, d3 = ada_w.shape
    bsz = c.shape[0]
    n = depth * 2
    rows = 8
    c_pad = jnp.zeros((rows, d), F32).at[:bsz].set(c)
    tn = 1024
    out = pl.pallas_call(
        _ada_kernel,
        grid=(n, d3 // tn),
        in_specs=[
            pl.BlockSpec((rows, d), lambda s, j: (0, 0)),
            pl.BlockSpec((None, d, tn), lambda s, j: (s, 0, j)),
            pl.BlockSpec((None, 1, tn), lambda s, j: (s, 0, j)),
        ],
        out_specs=pl.BlockSpec((None, rows, tn), lambda s, j: (s, 0, j)),
        out_shape=jax.ShapeDtypeStruct((n, rows, d3), F32),
        compiler_params=_cparams("parallel", "parallel"),
        name="ada_mods",
    )(c_pad, ada_w.reshape(n, d, d3), ada_b.reshape(n, 1, d3))
    return out[:, :bsz].reshape(n, bsz, 3, d).transpose(0, 2, 1, 3).reshape(n, 3, bsz, 1, d)


SHIFT, SCALE, GATE = 0, 1, 2


def _mod_spec(d, sub, which, tiles_per_batch):
    return pl.BlockSpec((None, None, None, 1, d),
                        lambda i, *_: (sub, which, i // tiles_per_batch, 0, 0))


def _row_spec(d):
    return pl.BlockSpec((1, d), lambda *_: (0, 0))


def _modulate_kernel(x_ref, sc_ref, sh_ref, h_ref):
    h_ref[...] = (x_ref[...] * (1.0 + sc_ref[...]) + sh_ref[...]).astype(BF16)


def _modulate(x, mods, sub, seq, tm=512):
    t, d = x.shape
    tpb = seq // tm
    return pl.pallas_call(
        _modulate_kernel,
        grid=(t // tm,),
        in_specs=[pl.BlockSpec((tm, d), lambda i: (i, 0)),
                  _mod_spec(d, sub, SCALE, tpb), _mod_spec(d, sub, SHIFT, tpb)],
        out_specs=pl.BlockSpec((tm, d), lambda i: (i, 0)),
        out_shape=jax.ShapeDtypeStruct((t, d), BF16),
        compiler_params=_cparams("parallel"),
        name="modulate",
    )(x, mods, mods)


def _proj_kernel(segs, n_extra, h_ref, w_ref, *refs):
    extra = refs[:n_extra]
    outs = refs[n_extra:]
    j = pl.program_id(1)
    for (lo, hi, epi), o_ref in zip(segs, outs):
        def _store(o_ref=o_ref, epi=epi):
            acc = _dot(h_ref[...], w_ref[...])
            o_ref[...] = epi(acc, *[e[...] for e in extra]).astype(o_ref.dtype)
        if len(segs) == 1:
            _store()
        else:
            pl.when(jnp.logical_and(j >= lo, j < hi))(_store)


def _proj(h, w, segs, extra=(), extra_specs=(), tm=1024, tn=512, name="proj"):
    t, k = h.shape
    n = w.shape[1]
    bounds, lo = [], 0
    for n_cols, epi, _ in segs:
        assert n_cols % tn == 0
        bounds.append((lo, lo + n_cols // tn, epi))
        lo += n_cols // tn
    assert lo * tn == n
    out_specs = [
        pl.BlockSpec((tm, tn), functools.partial(
            lambda i, j, lo, hi: (i, jnp.clip(j - lo, 0, hi - lo - 1)), lo=lo_, hi=hi_))
        for lo_, hi_, _ in bounds]
    out_shape = [jax.ShapeDtypeStruct((t, n_cols), dt) for n_cols, _, dt in segs]
    return pl.pallas_call(
        functools.partial(_proj_kernel, bounds, len(extra)),
        grid=(t // tm, n // tn),
        in_specs=[pl.BlockSpec((tm, k), lambda i, j: (i, 0)),
                  pl.BlockSpec((k, tn), lambda i, j: (0, j))] + list(extra_specs),
        out_specs=out_specs,
        out_shape=out_shape,
        compiler_params=_cparams("parallel", "arbitrary"),
        name=name,
    )(h, w, *extra)


def _ln_mod_store(y, x_ref, gate_ref, lnw_ref, lnb_ref, sc_ref, sh_ref, xo_ref, ho_ref, rows=slice(None)):
    z = DN_ALPHA * x_ref[rows, :] + (1.0 + gate_ref[...]) * y
    mu = jnp.mean(z, axis=-1, keepdims=True)
    zc = z - mu
    var = jnp.mean(zc * zc, axis=-1, keepdims=True)
    xn = zc * lax.rsqrt(var + LN_EPS) * lnw_ref[...] + lnb_ref[...]
    xo_ref[rows, :] = xn
    if ho_ref is not None:
        ho_ref[rows, :] = (xn * (1.0 + sc_ref[...]) + sh_ref[...]).astype(BF16)


OUT_LN_CHUNKS = 2


def _out_ln_kernel(n_a, has_next, *refs):
    a_refs = refs[:n_a]
    w_ref, x_ref, gate_ref, lnw_ref, lnb_ref = refs[n_a:n_a + 5]
    rest = refs[n_a + 5:]
    if has_next:
        sc_ref, sh_ref, xo_ref, ho_ref = rest
    else:
        (xo_ref,), sc_ref, sh_ref, ho_ref = rest, None, None, None
    rc = x_ref.shape[0] // OUT_LN_CHUNKS
    for c in range(OUT_LN_CHUNKS):
        rows = slice(c * rc, (c + 1) * rc)
        a = a_refs[0][rows, :]
        for r in a_refs[1:]:
            a = a.astype(F32) + r[rows, :].astype(F32)
        y = _dot(a.astype(BF16), w_ref[...])
        _ln_mod_store(y, x_ref, gate_ref, lnw_ref, lnb_ref, sc_ref, sh_ref, xo_ref, ho_ref, rows)


def _out_ln(a_list, w, x, mods, sub, sub_next, ln_w, ln_b, seq, tm=512):
    t, d = x.shape
    k = w.shape[0]
    tpb = seq // tm
    has_next = sub_next is not None
    in_specs = [pl.BlockSpec((tm, k), lambda i: (i, 0)) for _ in a_list]
    in_specs += [pl.BlockSpec((k, d), lambda i: (0, 0), pipeline_mode=pl.Buffered(1)),
                 pl.BlockSpec((tm, d), lambda i: (i, 0)),
                 _mod_spec(d, sub, GATE, tpb), _row_spec(d), _row_spec(d)]
    args = list(a_list) + [w, x, mods, ln_w.reshape(1, d), ln_b.reshape(1, d)]
    out_specs = [pl.BlockSpec((tm, d), lambda i: (i, 0))]
    out_shape = [jax.ShapeDtypeStruct((t, d), F32)]
    if has_next:
        in_specs += [_mod_spec(d, sub_next, SCALE, tpb), _mod_spec(d, sub_next, SHIFT, tpb)]
        args += [mods, mods]
        out_specs.append(pl.BlockSpec((tm, d), lambda i: (i, 0)))
        out_shape.append(jax.ShapeDtypeStruct((t, d), BF16))
    res = pl.pallas_call(
        functools.partial(_out_ln_kernel, len(a_list), has_next),
        grid=(t // tm,),
        in_specs=in_specs, out_specs=out_specs, out_shape=out_shape,
        compiler_params=_cparams("parallel"),
        name="out_ln",
    )(*args)
    return (res[0], res[1]) if has_next else (res[0], None)


MLP_EPILOGUE_CHUNKS = 2


MLP_WEIGHT_SLOTS = 3


def _mlp_kernel(has_next, n_i, n_j, h_ref, w1_hbm, w2_hbm, x_hbm, gate_ref, lnw_ref, lnb_ref, *rest):
    if has_next:
        sc_ref, sh_ref, xo_hbm, ho_hbm, acc_ref, w1_buf, w2_buf, x_buf, xo_buf, ho_buf, sem, x_sem, o_sem = rest
    else:
        xo_hbm, acc_ref, w1_buf, w2_buf, x_buf, xo_buf, sem, x_sem, o_sem = rest
        sc_ref = sh_ref = ho_hbm = ho_buf = None
    i = pl.program_id(0)
    j = pl.program_id(1)
    last = n_j - 1
    tm = h_ref.shape[0]
    tf = w1_buf.shape[-1]
    step = i * n_j + j
    n_steps = n_i * n_j
    ahead = MLP_WEIGHT_SLOTS - 1

    def weight_copies(s):
        slot = s % MLP_WEIGHT_SLOTS
        col = (s % n_j) * tf
        if not isinstance(s, int):
            col = pl.multiple_of(col, tf)
        return (pltpu.make_async_copy(w1_hbm.at[:, pl.ds(col, tf)], w1_buf.at[slot], sem.at[0, slot]),
                pltpu.make_async_copy(w2_hbm.at[pl.ds(col, tf), :], w2_buf.at[slot], sem.at[1, slot]))

    @pl.when(step == 0)
    def _():
        for s in range(ahead):
            for cp in weight_copies(s):
                cp.start()

    @pl.when(step + ahead < n_steps)
    def _():
        for cp in weight_copies(step + ahead):
            cp.start()

    for cp in weight_copies(step):
        cp.wait()
    slot = step % MLP_WEIGHT_SLOTS

    def partial_out(rows):
        hid = _dot(h_ref[rows, :], w1_buf[slot])
        hid = jnp.square(jnp.maximum(hid, 0.0)).astype(BF16)
        return _dot(hid, w2_buf[slot])

    def x_copy(tile):
        return pltpu.make_async_copy(x_hbm.at[pl.ds(pl.multiple_of(tile * tm, tm), tm), :], x_buf, x_sem.at[0])

    def out_copies(tile):
        rows = pl.ds(pl.multiple_of(tile * tm, tm), tm)
        cps = [pltpu.make_async_copy(xo_buf, xo_hbm.at[rows, :], o_sem.at[0])]
        if has_next:
            cps.append(pltpu.make_async_copy(ho_buf, ho_hbm.at[rows, :], o_sem.at[1]))
        return cps

    @pl.when(j == 0)
    def _():
        x_copy(i).start()
        acc_ref[...] = partial_out(slice(None))

    @pl.when(jnp.logical_and(j > 0, j < last))
    def _():
        acc_ref[...] += partial_out(slice(None))

    @pl.when(jnp.logical_and(j == last, i > 0))
    def _():
        for cp in out_copies(i - 1):
            cp.wait()

    @pl.when(j == last)
    def _():
        x_copy(i).wait()
        rc = tm // MLP_EPILOGUE_CHUNKS
        for c in range(MLP_EPILOGUE_CHUNKS):
            rows = slice(c * rc, (c + 1) * rc)
            y = acc_ref[rows, :] + partial_out(rows)
            _ln_mod_store(y, x_buf, gate_ref, lnw_ref, lnb_ref, sc_ref, sh_ref, xo_buf, ho_buf, rows)
        for cp in out_copies(i):
            cp.start()

    @pl.when(step == n_steps - 1)
    def _():
        for cp in out_copies(i):
            cp.wait()


def _mlp(h, w1, w2, x, mods, sub, sub_next, ln_w, ln_b, seq, tm=512, tf=1024):
    t, d = x.shape
    ff = w1.shape[1]
    assert ff // tf >= 2 and (t // tm) * (ff // tf) >= MLP_WEIGHT_SLOTS
    tpb = seq // tm
    has_next = sub_next is not None
    in_specs = [pl.BlockSpec((tm, d), lambda i, j: (i, 0)),
                pl.BlockSpec(memory_space=pl.ANY),
                pl.BlockSpec(memory_space=pl.ANY),
                pl.BlockSpec(memory_space=pl.ANY),
                _mod_spec(d, sub, GATE, tpb), _row_spec(d), _row_spec(d)]
    args = [h, w1, w2, x, mods, ln_w.reshape(1, d), ln_b.reshape(1, d)]
    out_specs = [pl.BlockSpec(memory_space=pl.ANY)]
    out_shape = [jax.ShapeDtypeStruct((t, d), F32)]
    staging = [pltpu.VMEM((tm, d), F32), pltpu.VMEM((tm, d), F32)]
    if has_next:
        in_specs += [_mod_spec(d, sub_next, SCALE, tpb), _mod_spec(d, sub_next, SHIFT, tpb)]
        args += [mods, mods]
        out_specs.append(pl.BlockSpec(memory_space=pl.ANY))
        out_shape.append(jax.ShapeDtypeStruct((t, d), BF16))
        staging.append(pltpu.VMEM((tm, d), BF16))
    res = pl.pallas_call(
        functools.partial(_mlp_kernel, has_next, t // tm, ff // tf),
        grid=(t // tm, ff // tf),
        in_specs=in_specs, out_specs=out_specs, out_shape=out_shape,
        scratch_shapes=[pltpu.VMEM((tm, d), F32),
                        pltpu.VMEM((MLP_WEIGHT_SLOTS, d, tf), BF16), pltpu.VMEM((MLP_WEIGHT_SLOTS, tf, d), BF16),
                        *staging,
                        pltpu.SemaphoreType.DMA((2, MLP_WEIGHT_SLOTS)), pltpu.SemaphoreType.DMA((1,)),
                        pltpu.SemaphoreType.DMA((2,))],
        compiler_params=_cparams("arbitrary", "arbitrary"),
        name="mlp",
    )(*args)
    return (res[0], res[1]) if has_next else (res[0], None)


HALO = 8


def _lru_scan_kernel(ts, gy_ref, xb_ref, cw_ref, cb_ref, gw_ref, gb_ref, lam_ref, o_ref,
                     buf, a_scr, u_scr, h_scr):
    i = pl.program_id(1)
    width = xb_ref.shape[-1]
    bd = width // LRU_BLOCKS

    @pl.when(i == 0)
    def _():
        buf[0:HALO, :] = jnp.zeros((HALO, width), F32)
        h_scr[...] = jnp.zeros_like(h_scr)

    @pl.when(i > 0)
    def _():
        buf[0:HALO, :] = buf[ts:ts + HALO, :]

    buf[HALO:HALO + ts, :] = xb_ref[...]
    cw = cw_ref[...]
    taps = [buf[HALO - (CONV_WIDTH - 1) + k:HALO - (CONV_WIDTH - 1) + k + ts, :] * cw[k:k + 1, :]
            for k in range(CONV_WIDTH)]
    conv = taps[0]
    for tap in taps[1:]:
        conv = conv + tap
    xc = cb_ref[...] + conv
    xcb = xc.astype(BF16)
    gates = []
    for jg in range(2):
        cols = [_dot(xcb[:, n * bd:(n + 1) * bd], gw_ref[jg, n]) for n in range(LRU_BLOCKS)]
        gates.append(jnp.concatenate(cols, axis=1) + gb_ref[jg:jg + 1, :])
    r = 0.5 * (1.0 + jnp.tanh(0.5 * gates[0]))
    ig = 0.5 * (1.0 + jnp.tanh(0.5 * gates[1]))
    neg_lam = -lam_ref[...]
    softplus = jnp.maximum(neg_lam, 0.0) + jnp.log1p(jnp.exp(-jnp.abs(neg_lam)))
    log_a = -LRU_C * r * softplus
    a = jnp.exp(log_a)
    a_scr[...] = a
    one_m_a2 = -jnp.tanh(log_a) * (1.0 + a * a)
    root = one_m_a2 * lax.rsqrt(jnp.maximum(one_m_a2, F32_TINY))
    u_scr[...] = root * (ig * xc)

    def step(t, h):
        h = a_scr[pl.ds(t, 1), :] * h + u_scr[pl.ds(t, 1), :]
        u_scr[pl.ds(t, 1), :] = h
        return h

    h_scr[...] = lax.fori_loop(0, ts, step, h_scr[...], unroll=8)
    o_ref[...] = (gy_ref[...] * u_scr[...]).astype(BF16)


def _lru_scan(gy, xb, conv_w, conv_b, gate_w, gate_b, lam, ts=512):
    bsz, seq, width = xb.shape
    bd = width // LRU_BLOCKS
    tile = pl.BlockSpec((None, ts, width), lambda b, i: (b, i, 0))
    return pl.pallas_call(
        functools.partial(_lru_scan_kernel, ts),
        grid=(bsz, seq // ts),
        in_specs=[tile, tile,
                  pl.BlockSpec((CONV_WIDTH, width), lambda b, i: (0, 0)),
                  pl.BlockSpec((1, width), lambda b, i: (0, 0)),
                  pl.BlockSpec((2, LRU_BLOCKS, bd, bd), lambda b, i: (0, 0, 0, 0)),
                  pl.BlockSpec((2, width), lambda b, i: (0, 0)),
                  pl.BlockSpec((1, width), lambda b, i: (0, 0))],
        out_specs=tile,
        out_shape=jax.ShapeDtypeStruct((bsz, seq, width), BF16),
        scratch_shapes=[pltpu.VMEM((ts + HALO, width), F32), pltpu.VMEM((ts, width), F32),
                        pltpu.VMEM((ts, width), F32), pltpu.VMEM((1, width), F32)],
        compiler_params=_cparams("parallel", "arbitrary"),
        name="lru_scan",
    )(gy, xb, conv_w, conv_b.reshape(1, width), gate_w.astype(BF16), gate_b, lam.reshape(1, width))


def _lru_mixer(h, bsz, seq, w_in, conv_w, conv_b, gate_w, gate_b, lam):
    width = w_in.shape[1] // 2
    gy, xb = _proj(h, w_in.astype(BF16),
                   [(width, lambda acc: _gelu_tanh(acc), BF16), (width, lambda acc: acc, F32)],
                   tn=1024, name="lru_proj")
    a = _lru_scan(gy.reshape(bsz, seq, width), xb.reshape(bsz, seq, width),
                  conv_w, conv_b, gate_w, gate_b, lam)
    return [a.reshape(bsz * seq, width)]


def _t5_bucket_np(dist):
    n = np.maximum(np.asarray(dist, np.int64), 0)
    max_exact = T5_BUCKETS // 2
    n_large = T5_BUCKETS - max_exact
    ratio = T5_MAX_DIST // max_exact
    thresholds = []
    for k in range(1, n_large):
        m = max_exact
        while m ** n_large < ratio ** k * max_exact ** n_large:
            m += 1
        thresholds.append(m)
    large = max_exact + sum((n >= th).astype(np.int64) for th in thresholds)
    return np.where(n < max_exact, n, large).astype(np.int32)


def _bias_table_kernel(sub_far, rb_ref, idx_ref, o_ref):
    hd = pl.program_id(0)
    idx = idx_ref[...]
    base = rb_ref[T5_BUCKETS - 1, hd] if sub_far else 0.0
    acc = jnp.full(idx.shape, NEG_MASK, F32)
    for b in range(T5_BUCKETS):
        acc = jnp.where(idx == b, (rb_ref[b, hd] - base) * LOG2E, acc)
    o_ref[...] = acc


def _bias_table(rel_bias, idx_np, sub_far, name):
    rows, cols = idx_np.shape
    tr = 128
    n_heads = rel_bias.shape[1]
    return pl.pallas_call(
        functools.partial(_bias_table_kernel, sub_far),
        grid=(n_heads, rows // tr),
        in_specs=[pl.BlockSpec(memory_space=pltpu.SMEM),
                  pl.BlockSpec((tr, cols), lambda hd, i: (i, 0))],
        out_specs=pl.BlockSpec((None, tr, cols), lambda hd, i: (hd, i, 0)),
        out_shape=jax.ShapeDtypeStruct((n_heads, rows, cols), F32),
        compiler_params=_cparams("parallel", "parallel"),
        name=name,
    )(rel_bias, jnp.asarray(idx_np))


def _compress_kernel(seq, x_ref, pe_ref, w1_ref, b1_ref, w2_ref, o_ref, xs):
    xs[0:seq, :] = x_ref[...]
    xs[seq:CMP_ROWS, :] = jnp.zeros((CMP_ROWS - seq, LANES), F32)
    acc = jnp.zeros((N_CMP_SLOTS, LANES), F32)
    for l in range(CMP_BLOCK):
        rows = xs[pl.ds(l, N_CMP_SLOTS, stride=CMP_STRIDE), :] + pe_ref[l:l + 1, :]
        acc = acc + _dot(rows.astype(BF16), w1_ref[l])
    hid = _gelu_tanh(acc + b1_ref[...])
    o_ref[...] = _dot(hid.astype(BF16), w2_ref[...]).astype(BF16)


def _compress(cmp_tok, pe, w1, b1, w2):
    bsz, seq, _ = cmp_tok.shape
    g = NSA_KV_GROUPS
    dh = NSA_HEAD_DIM
    return pl.pallas_call(
        functools.partial(_compress_kernel, seq),
        grid=(bsz, 2, g),
        in_specs=[pl.BlockSpec((None, seq, dh), lambda b, j, gi: (b, 0, j * g + gi)),
                  pl.BlockSpec((None, CMP_BLOCK, dh), lambda b, j, gi: (j, 0, 0)),
                  pl.BlockSpec((None, CMP_BLOCK, dh, dh), lambda b, j, gi: (j, 0, 0, 0)),
                  pl.BlockSpec((None, 1, dh), lambda b, j, gi: (j, 0, 0)),
                  pl.BlockSpec((None, dh, dh), lambda b, j, gi: (j, 0, 0))],
        out_specs=pl.BlockSpec((None, None, None, N_CMP_SLOTS, dh), lambda b, j, gi: (b, j, gi, 0, 0)),
        out_shape=jax.ShapeDtypeStruct((bsz, 2, g, N_CMP_SLOTS, dh), BF16),
        scratch_shapes=[pltpu.VMEM((CMP_ROWS, dh), F32)],
        compiler_params=_cparams("parallel", "parallel", "parallel"),
        name="nsa_compress",
    )(cmp_tok, pe, w1.astype(BF16), b1.reshape(2, 1, dh), w2.astype(BF16))


CMP_COLS = N_CMP_SLOTS // LANES
CMP_COL_TOKENS = CMP_STRIDE * LANES


def _cmp_table_idx(tq):
    ql = np.arange(tq)[:, None]
    m = np.arange(N_CMP_SLOTS)[None, :]
    e = np.where(m <= (tq - CMP_BLOCK) // CMP_STRIDE, -m, N_CMP_SLOTS - m)
    dist = CMP_STRIDE * e + ql - (CMP_BLOCK - 1)
    return _t5_bucket_np(dist)


def _sel_sum_matrix():
    i = np.arange(N_CMP_SLOTS)[:, None]
    j = np.arange(LANES)[None, :]
    ratio = SEL_BLOCK // CMP_STRIDE
    return ((i >= ratio * j - 1) & (i <= ratio * j + ratio - 1)).astype(np.float32)


CMP_SUB = 2


def _cmp_attn_kernel(tq, q_ref, kc_ref, vc_ref, tb_ref, sm_ref, g_ref, o_ref, val_ref):
    i = pl.program_id(2)
    lane = lax.broadcasted_iota(jnp.int32, (tq, LANES), 1)
    row = lax.broadcasted_iota(jnp.int32, (tq, LANES), 0)

    def body(nc):
        width = nc * LANES
        kc = kc_ref[0:width, :]
        vc = vc_ref[0:width, :]
        ql = lax.broadcasted_iota(jnp.int32, (tq, width), 0)
        blk = lax.broadcasted_iota(jnp.int32, (tq, width), 1)
        rel = ql - CMP_STRIDE * blk - (CMP_BLOCK - 1)
        for sub in range(CMP_SUB):
            rows = slice(sub * tq, (sub + 1) * tq)
            tile = i * CMP_SUB + sub
            t0 = tile * tq
            gate = g_ref[rows, :]
            causal = jnp.where(t0 + rel >= 0, 0.0, NEG_MASK)
            row_ok = (t0 + lax.broadcasted_iota(jnp.int32, (tq, 1), 0)) >= CMP_BLOCK - 1
            shift = (tile * (tq // CMP_STRIDE)) % LANES
            keep = lane >= shift
            p_sum = jnp.zeros((tq, width), F32)
            for hd in range(NSA_HPG):
                rot = {}
                for c in range(nc):
                    for k in ((c - (nc - 1)) % CMP_COLS, (c - nc) % CMP_COLS):
                        if k not in rot:
                            rot[k] = pltpu.roll(tb_ref[hd, :, k * LANES:(k + 1) * LANES], shift, axis=1)
                tb = jnp.concatenate(
                    [jnp.where(keep, rot[(c - (nc - 1)) % CMP_COLS], rot[(c - nc) % CMP_COLS])
                     for c in range(nc)], axis=1)
                s = _dot_nt(q_ref[rows, hd * LANES:(hd + 1) * LANES], kc) + tb + causal
                e = jnp.exp2(s - jnp.max(s, axis=-1, keepdims=True))
                p = e * jnp.where(row_ok, 1.0 / jnp.sum(e, axis=-1, keepdims=True), 0.0)
                p_sum = p_sum + p
                o_ref[rows, hd * LANES:(hd + 1) * LANES] = (
                    _dot(p.astype(BF16), vc) * gate[:, hd:hd + 1]).astype(o_ref.dtype)
            sm = sm_ref[0:width, :]
            hi = p_sum.astype(BF16)
            r1 = p_sum - hi.astype(F32)
            mid = r1.astype(BF16)
            lo = (r1 - mid.astype(F32)).astype(BF16)
            score = (_dot(hi, sm) + _dot(mid, sm)) + _dot(lo, sm)
            cur = (t0 + row) >> 6
            forced = jnp.logical_or(lane == 0, jnp.logical_or(lane == cur, lane == cur - 1))
            val_ref[rows, :] = jnp.where(forced, SEL_FORCE, jnp.where(lane <= cur, score, -SEL_FORCE))

    last_q = (i + 1) * (CMP_SUB * tq) - 1
    n_cols = jnp.minimum((last_q - (CMP_BLOCK - 1)) // CMP_COL_TOKENS + 1, CMP_COLS)
    for nc in range(1, CMP_COLS + 1):
        pl.when(n_cols == nc)(functools.partial(body, nc))


N_FORCED = 3


def _topk_mask_kernel(n_top, val_ref, m_ref):
    val0 = val_ref[...]
    lane = lax.broadcasted_iota(jnp.int32, val0.shape, 1).astype(F32)
    forced = val0 >= 0.5 * SEL_FORCE
    val = jnp.where(forced, -jnp.inf, val0)
    sel = jnp.where(forced, 1.0, 0.0)
    for _ in range(n_top - N_FORCED):
        mx = jnp.max(val, axis=-1, keepdims=True)
        first = jnp.min(jnp.where(val == mx, lane, float(LANES)), axis=-1, keepdims=True)
        pick = lane == first
        sel = jnp.where(pick, 1.0, sel)
        val = jnp.where(pick, -jnp.inf, val)
    m_ref[...] = jnp.where(jnp.logical_and(sel > 0.0, val0 >= 0.0), 0.0, NEG_MASK).astype(BF16)


def _topk_mask(val, n_top, tr=1024):
    rows = val.shape[0]
    return pl.pallas_call(
        functools.partial(_topk_mask_kernel, n_top),
        grid=(rows // tr,),
        in_specs=[pl.BlockSpec((tr, LANES), lambda i: (i, 0))],
        out_specs=pl.BlockSpec((tr, LANES), lambda i: (i, 0)),
        out_shape=jax.ShapeDtypeStruct((rows, LANES), BF16),
        compiler_params=_cparams("parallel"),
        name="nsa_topk_mask",
    )(val)


def _cmp_attn(q, kvc, table, gates, tq=256):
    bsz, seq, _ = q.shape
    g = NSA_KV_GROUPS
    qw = NSA_HPG * NSA_HEAD_DIM
    rows = tq * CMP_SUB
    assert CMP_COL_TOKENS % rows == 0 and tq % CMP_STRIDE == 0
    return pl.pallas_call(
        functools.partial(_cmp_attn_kernel, tq),
        grid=(bsz, g, seq // rows),
        in_specs=[pl.BlockSpec((None, rows, qw), lambda b, gi, i: (b, i, gi)),
                  pl.BlockSpec((None, None, None, N_CMP_SLOTS, NSA_HEAD_DIM), lambda b, gi, i: (b, 0, gi, 0, 0)),
                  pl.BlockSpec((None, None, None, N_CMP_SLOTS, NSA_HEAD_DIM), lambda b, gi, i: (b, 1, gi, 0, 0)),
                  pl.BlockSpec((NSA_HPG, tq, N_CMP_SLOTS), lambda b, gi, i: (gi, 0, 0)),
                  pl.BlockSpec((N_CMP_SLOTS, LANES), lambda b, gi, i: (0, 0)),
                  pl.BlockSpec((None, rows, LANES), lambda b, gi, i: (b, i, gi))],
        out_specs=[pl.BlockSpec((None, rows, qw), lambda b, gi, i: (b, i, gi)),
                   pl.BlockSpec((None, None, rows, LANES), lambda b, gi, i: (b, gi, i, 0))],
        out_shape=[jax.ShapeDtypeStruct((bsz, seq, g * qw), BF16),
                   jax.ShapeDtypeStruct((bsz, g, seq, LANES), F32)],
        compiler_params=_cparams("parallel", "parallel", "parallel"),
        name="nsa_cmp_attn",
    )(q, kvc, kvc, table, jnp.asarray(_sel_sum_matrix(), BF16), gates)


SEL_TQ = 256
SEL_TK = 256
SEL_N_DELTA = -(-(T5_MAX_DIST - 1 + SEL_TK) // SEL_TQ)
SEL_N_NEAR = -(-SEL_N_DELTA // (SEL_TK // SEL_TQ))
SEL_STREAMS = 4


def _sel_table_idx():
    d = np.arange(SEL_N_DELTA)[:, None, None]
    ql = np.arange(SEL_TQ)[None, :, None]
    kl = np.arange(SEL_TK)[None, None, :]
    dist = ql - kl + SEL_TQ * d
    idx = np.where(dist >= 0, _t5_bucket_np(dist), MASK_BUCKET)
    return idx.reshape(SEL_N_DELTA * SEL_TQ, SEL_TK).astype(np.int32)


def _sel_attn_kernel(q_ref, m_ref, k_ref, v_ref, oh_ref, nb_ref, g_ref, o_ref, qa, m_scr, acc):
    tq, tk = SEL_TQ, SEL_TK
    ratio = tk // tq
    i = pl.program_id(2)
    kd = i // ratio
    sel_mask = m_ref[...]
    for hd in range(NSA_HPG):
        qa[hd * tq:(hd + 1) * tq, 0:LANES] = q_ref[:, hd * LANES:(hd + 1) * LANES]
        qa[hd * tq:(hd + 1) * tq, LANES:2 * LANES] = sel_mask
    m_scr[...] = jnp.full(m_scr.shape, -jnp.inf, F32)
    acc[...] = jnp.zeros_like(acc)
    n_kt = k_ref.shape[0] // tk
    ones = jnp.ones((tk, LANES), BF16)

    def update(st, tiles):
        scores, values = [], []
        for kt, valid, delta in tiles:
            k0 = pl.multiple_of(jnp.clip(kt, 0, n_kt - 1) * tk, tk)
            ka = jnp.concatenate([k_ref[pl.ds(k0, tk), :], oh_ref[pl.ds(k0, tk), :]], axis=1)
            s = _dot_nt(qa[...], ka)
            if valid is not None:
                s = s + jnp.where(valid, 0.0, NEG_MASK)
            if delta is not None:
                s = s + jnp.concatenate([nb_ref[hd, delta] for hd in range(NSA_HPG)], axis=0)
            scores.append(s)
            values.append(jnp.concatenate([v_ref[pl.ds(k0, tk), :], ones], axis=1))
        s_max = scores[0]
        for s in scores[1:]:
            s_max = jnp.maximum(s_max, s)
        m_prev = m_scr[st]
        m_new = jnp.maximum(m_prev, jnp.max(s_max, axis=-1, keepdims=True))
        alpha = jnp.exp2(m_prev - m_new)
        m_rep = jnp.concatenate([m_new] * (tk // LANES), axis=1)
        pv = None
        for s, va in zip(scores, values):
            part = _dot(jnp.exp2(s - m_rep).astype(BF16), va)
            pv = part if pv is None else pv + part
        acc[st] = jnp.concatenate([alpha, alpha], axis=1) * acc[st] + pv
        m_scr[st] = m_new

    n_far = jnp.maximum(kd - (SEL_N_NEAR - 1), 0)
    pair = 2 * SEL_STREAMS

    def far_pairs(it, carry):
        for st in range(SEL_STREAMS):
            kt = it * pair + 2 * st
            update(st, [(kt, None, None), (kt + 1, None, None)])
        return carry

    lax.fori_loop(0, n_far // pair, far_pairs, 0)
    base = (n_far // pair) * pair
    rem = n_far - base

    @pl.when(rem >= SEL_STREAMS)
    def _():
        for st in range(SEL_STREAMS):
            update(st, [(base + st, None, None)])

    base = base + jnp.where(rem >= SEL_STREAMS, SEL_STREAMS, 0)
    left = n_far - base
    slots = [(base + f, f < left, None) for f in range(SEL_STREAMS - 1)]
    for back in range(SEL_N_NEAR - 1, -1, -1):
        kt = kd - back
        slots.append((kt, kt >= 0, (i - kd * ratio) + back * ratio))
    for st in range(SEL_STREAMS):
        update(st, slots[st::SEL_STREAMS])

    m_all = m_scr[0]
    for st in range(1, SEL_STREAMS):
        m_all = jnp.maximum(m_all, m_scr[st])
    tot = jnp.zeros((NSA_HPG * tq, 2 * LANES), F32)
    for st in range(SEL_STREAMS):
        w = jnp.exp2(m_scr[st] - m_all)
        tot = tot + jnp.concatenate([w, w], axis=1) * acc[st]
    gate = g_ref[...]
    out = tot[:, :LANES] / jnp.maximum(tot[:, LANES:], 1e-30)
    for hd in range(NSA_HPG):
        o_ref[:, hd * LANES:(hd + 1) * LANES] = (
            out[hd * tq:(hd + 1) * tq, :] * gate[:, NSA_HPG + hd:NSA_HPG + hd + 1]).astype(o_ref.dtype)


def _sel_attn(q, sel_mask, kv_tok, onehot, table, gates):
    bsz, seq, _ = q.shape
    g = NSA_KV_GROUPS
    dh = NSA_HEAD_DIM
    qw = NSA_HPG * dh
    tq, tk = SEL_TQ, SEL_TK
    rows = NSA_HPG * tq
    return pl.pallas_call(
        _sel_attn_kernel,
        grid=(bsz, g, seq // tq),
        in_specs=[pl.BlockSpec((None, tq, qw), lambda b, gi, i: (b, i, gi)),
                  pl.BlockSpec((None, None, tq, LANES), lambda b, gi, i: (b, gi, i, 0)),
                  pl.BlockSpec((None, seq, dh), lambda b, gi, i: (b, 0, gi)),
                  pl.BlockSpec((None, seq, dh), lambda b, gi, i: (b, 0, g + gi)),
                  pl.BlockSpec((seq, LANES), lambda b, gi, i: (0, 0)),
                  pl.BlockSpec((NSA_HPG, SEL_N_DELTA, tq, tk), lambda b, gi, i: (gi, 0, 0, 0)),
                  pl.BlockSpec((None, tq, LANES), lambda b, gi, i: (b, i, gi))],
        out_specs=pl.BlockSpec((None, tq, qw), lambda b, gi, i: (b, i, gi)),
        out_shape=jax.ShapeDtypeStruct((bsz, seq, g * qw), BF16),
        scratch_shapes=[pltpu.VMEM((rows, 2 * LANES), BF16), pltpu.VMEM((SEL_STREAMS, rows, LANES), F32),
                        pltpu.VMEM((SEL_STREAMS, rows, 2 * dh), F32)],
        compiler_params=_cparams("parallel", "parallel", "parallel"),
        name="nsa_sel_attn",
    )(q, sel_mask, kv_tok, kv_tok, onehot, table, gates)


WIN_TQ = 128
WIN_SUB = 4
WIN_KEYS = WINDOW + WIN_TQ


def _win_table_idx():
    ql = np.arange(WIN_TQ)[:, None]
    kl = np.arange(WIN_KEYS)[None, :]
    dist = ql + WINDOW - kl
    return np.where((dist >= 0) & (dist < WINDOW), _t5_bucket_np(dist), MASK_BUCKET).astype(np.int32)


def _win_attn_kernel(seq, q_ref, k_ref, v_ref, wb_ref, g_ref, o_ref, kpad, vpad, qs):
    tq = WIN_TQ
    i = pl.program_id(2)

    @pl.when(i == 0)
    def _():
        kpad[0:WINDOW, :] = jnp.zeros((WINDOW, LANES), BF16)
        vpad[0:WINDOW, :] = jnp.zeros((WINDOW, LANES), BF16)
        kpad[WINDOW:WINDOW + seq, :] = k_ref[...]
        vpad[WINDOW:WINDOW + seq, :] = v_ref[...]

    bias = wb_ref[...].reshape(NSA_HPG * tq, WIN_KEYS)

    def step(header_visible):
        for sub in range(WIN_SUB):
            rows = slice(sub * tq, (sub + 1) * tq)
            t0 = pl.multiple_of((i * WIN_SUB + sub) * tq, tq)
            for hd in range(NSA_HPG):
                qs[sub, hd * tq:(hd + 1) * tq, :] = q_ref[rows, hd * LANES:(hd + 1) * LANES]
            keys = kpad[pl.ds(t0, WIN_KEYS), :]
            vals = vpad[pl.ds(t0, WIN_KEYS), :]
            s = _dot_nt(qs[sub], keys) + bias
            if header_visible:
                kl = lax.broadcasted_iota(jnp.int32, s.shape, 1)
                s = jnp.where(kl >= WINDOW - t0, s, NEG_MASK)
            e = jnp.exp2(s - jnp.max(s, axis=-1, keepdims=True))
            out = _dot(e.astype(BF16), vals) * (1.0 / jnp.sum(e, axis=-1, keepdims=True))
            gate = g_ref[rows, :]
            for hd in range(NSA_HPG):
                o_ref[rows, hd * LANES:(hd + 1) * LANES] = (
                    out[hd * tq:(hd + 1) * tq, :]
                    * gate[:, 2 * NSA_HPG + hd:2 * NSA_HPG + hd + 1]).astype(o_ref.dtype)

    n_header_steps = -(-WINDOW // (WIN_SUB * tq))
    pl.when(i < n_header_steps)(functools.partial(step, True))
    pl.when(i >= n_header_steps)(functools.partial(step, False))


def _win_attn(q, kv_tok, table, gates):
    bsz, seq, _ = q.shape
    g = NSA_KV_GROUPS
    dh = NSA_HEAD_DIM
    qw = NSA_HPG * dh
    tq = WIN_TQ
    rows = tq * WIN_SUB
    return pl.pallas_call(
        functools.partial(_win_attn_kernel, seq),
        grid=(bsz, g, seq // rows),
        in_specs=[pl.BlockSpec((None, rows, qw), lambda b, gi, i: (b, i, gi)),
                  pl.BlockSpec((None, seq, dh), lambda b, gi, i: (b, 0, 2 * g + gi)),
                  pl.BlockSpec((None, seq, dh), lambda b, gi, i: (b, 0, 3 * g + gi)),
                  pl.BlockSpec((NSA_HPG, tq, WIN_KEYS), lambda b, gi, i: (gi, 0, 0)),
                  pl.BlockSpec((None, rows, LANES), lambda b, gi, i: (b, i, gi))],
        out_specs=pl.BlockSpec((None, rows, qw), lambda b, gi, i: (b, i, gi)),
        out_shape=jax.ShapeDtypeStruct((bsz, seq, g * qw), BF16),
        scratch_shapes=[pltpu.VMEM((seq + WINDOW, dh), BF16), pltpu.VMEM((seq + WINDOW, dh), BF16),
                        pltpu.VMEM((WIN_SUB, NSA_HPG * tq, dh), BF16)],
        compiler_params=_cparams("parallel", "parallel", "arbitrary"),
        name="nsa_win_attn",
    )(q, kv_tok, kv_tok, table, gates)


def _nsa_mixer(h, bsz, seq, rel_bias, w_in, cmp_pe, cmp_w1, cmp_b1, cmp_w2):
    g, hpg, dh = NSA_KV_GROUPS, NSA_HPG, NSA_HEAD_DIM
    qw = NSA_HEADS * dh
    kvw = g * dh
    n_main = qw + 2 * N_BRANCH * kvw
    scale = dh ** -0.5 * LOG2E
    q, cmp_tok, kv_tok = _proj(
        h, w_in[:, :n_main].astype(BF16),
        [(qw, lambda acc: acc * scale, BF16), (2 * kvw, lambda acc: acc, F32), (4 * kvw, lambda acc: acc, BF16)],
        tn=1024, name="nsa_proj")
    src = np.zeros((g * LANES,), np.int32)
    valid = np.zeros((g * LANES,), bool)
    for gi in range(g):
        for br in range(N_BRANCH):
            for hd in range(hpg):
                src[gi * LANES + br * hpg + hd] = n_main + br * NSA_HEADS + gi * hpg + hd
                valid[gi * LANES + br * hpg + hd] = True
    w_gate = jnp.where(jnp.asarray(valid)[None, :], w_in[:, src], 0.0).astype(BF16)
    (gates,) = _proj(h, w_gate, [(g * LANES, lambda acc: _sigmoid(acc), F32)], name="nsa_gate_proj")

    q = q.reshape(bsz, seq, qw)
    gates = gates.reshape(bsz, seq, g * LANES)
    kv_tok = kv_tok.reshape(bsz, seq, 4 * kvw)
    kvc = _compress(cmp_tok.reshape(bsz, seq, 2 * kvw), cmp_pe, cmp_w1, cmp_b1, cmp_w2)

    n_top = min(SEL_TOP_N, seq // SEL_BLOCK)
    cmp_tq = 256
    cmp_table = _bias_table(rel_bias, _cmp_table_idx(cmp_tq), False, "nsa_cmp_bias")
    o_c, sel_val = _cmp_attn(q, kvc, cmp_table, gates, cmp_tq)
    sel_mask = _topk_mask(sel_val.reshape(bsz * g * seq, LANES), n_top).reshape(bsz, g, seq, LANES)

    sel_table = _bias_table(rel_bias, _sel_table_idx(), True, "nsa_sel_bias")
    sel_table = sel_table.reshape(NSA_HEADS, SEL_N_DELTA, SEL_TQ, SEL_TK)
    onehot = (jnp.arange(seq, dtype=jnp.int32)[:, None] // SEL_BLOCK
              == jnp.arange(LANES, dtype=jnp.int32)[None, :]).astype(BF16)
    o_s = _sel_attn(q, sel_mask, kv_tok, onehot, sel_table, gates)

    win_table = _bias_table(rel_bias, _win_table_idx(), False, "nsa_win_bias")
    o_w = _win_attn(q, kv_tok, win_table, gates)
    t = bsz * seq
    return [o_c.reshape(t, qw), o_s.reshape(t, qw), o_w.reshape(t, qw)]


def _rotary_epi(scale, acc, cos, sin):
    half = cos.shape[-1]
    outs = []
    for hd in range(acc.shape[1] // (2 * half)):
        x1 = acc[:, 2 * hd * half:(2 * hd + 1) * half]
        x2 = acc[:, (2 * hd + 1) * half:(2 * hd + 2) * half]
        outs += [x1 * cos - x2 * sin, x1 * sin + x2 * cos]
    rot = jnp.concatenate(outs, axis=1)
    return rot * scale if scale != 1.0 else rot


RET_HPS = 2


def _ret_kernel(tc, q_ref, k_ref, v_ref, g_ref, dec_ref, xi_ref, zeta_ref, cd_ref, gw_ref, gb_ref,
                o_ref, state):
    i = pl.program_id(2)
    dk = q_ref.shape[-1] // RET_HPS
    dv = v_ref.shape[-1] // RET_HPS

    @pl.when(i == 0)
    def _():
        state[...] = jnp.zeros_like(state)

    for c in range(tc // RET_CHUNK):
        rows = slice(c * RET_CHUNK, (c + 1) * RET_CHUNK)
        for hh in range(RET_HPS):
            qk_cols = slice(hh * dk, (hh + 1) * dk)
            v_cols = slice(hh * dv, (hh + 1) * dv)
            qc = q_ref[rows, qk_cols]
            kc = k_ref[rows, qk_cols]
            vc = v_ref[rows, v_cols]
            inner = _dot_nt(qc, kc) * dec_ref[hh]
            st = state[hh]
            out = _dot(inner.astype(BF16), vc) + _dot(qc, st.astype(BF16)) * xi_ref[hh]
            kz = (kc.astype(F32) * zeta_ref[hh]).astype(BF16)
            state[hh] = st * cd_ref[hh] + _dot_tn(kz, vc)
            mu = jnp.mean(out, axis=-1, keepdims=True)
            oc = out - mu
            var = jnp.mean(oc * oc, axis=-1, keepdims=True)
            y = (oc * lax.rsqrt(var + GN_EPS)) * gw_ref[:, v_cols] + gb_ref[:, v_cols]
            o_ref[rows, v_cols] = (g_ref[rows, v_cols] * y).astype(BF16)


def _ret_core(q, k, v, g, gn_w, gn_b, tc=512):
    bsz, seq, qk_total = q.shape
    v_total = v.shape[-1]
    nh = RET_HEADS
    dk = qk_total // nh
    dv = v_total // nh
    cs = RET_CHUNK
    hps = RET_HPS
    log_g = jnp.log1p(-jnp.exp2(-5.0 - jnp.arange(nh, dtype=F32)))
    idx = jnp.arange(cs, dtype=F32)
    diff = idx[:, None] - idx[None, :]
    decay = jnp.where(diff >= 0, jnp.exp(jnp.maximum(diff, 0.0) * log_g[:, None, None]), 0.0)
    xi = jnp.exp((idx + 1.0) * log_g[:, None])[:, :, None]
    zeta = jnp.exp((cs - 1.0 - idx) * log_g[:, None])[:, :, None]
    chunk_decay = jnp.exp(cs * log_g).reshape(nh, 1, 1)
    return pl.pallas_call(
        functools.partial(_ret_kernel, tc),
        grid=(bsz, nh // hps, seq // tc),
        in_specs=[pl.BlockSpec((None, tc, hps * dk), lambda b, hp, i: (b, i, hp)),
                  pl.BlockSpec((None, tc, hps * dk), lambda b, hp, i: (b, i, hp)),
                  pl.BlockSpec((None, tc, hps * dv), lambda b, hp, i: (b, i, hp)),
                  pl.BlockSpec((None, tc, hps * dv), lambda b, hp, i: (b, i, hp)),
                  pl.BlockSpec((hps, cs, cs), lambda b, hp, i: (hp, 0, 0)),
                  pl.BlockSpec((hps, cs, 1), lambda b, hp, i: (hp, 0, 0)),
                  pl.BlockSpec((hps, cs, 1), lambda b, hp, i: (hp, 0, 0)),
                  pl.BlockSpec((hps, 1, 1), lambda b, hp, i: (hp, 0, 0)),
                  pl.BlockSpec((1, hps * dv), lambda b, hp, i: (0, hp)),
                  pl.BlockSpec((1, hps * dv), lambda b, hp, i: (0, hp))],
        out_specs=pl.BlockSpec((None, tc, hps * dv), lambda b, hp, i: (b, i, hp)),
        out_shape=jax.ShapeDtypeStruct((bsz, seq, v_total), BF16),
        scratch_shapes=[pltpu.VMEM((hps, dk, dv), F32)],
        compiler_params=_cparams("parallel", "parallel", "arbitrary"),
        name="ret_core",
    )(q, k, v, g, decay, xi, zeta, chunk_decay, gn_w.reshape(1, v_total), gn_b.reshape(1, v_total))


def _ret_mixer(h, bsz, seq, w_in, gn_w, gn_b, tm=1024):
    qk_total = w_in.shape[0]
    v_total = (w_in.shape[1] - 2 * qk_total) // 2
    dk = qk_total // RET_HEADS
    half = dk // 2
    pos = jnp.arange(seq, dtype=F32)
    inv = ROPE_BASE ** (-jnp.arange(half, dtype=F32) / half)
    ang = pos[:, None] * inv[None, :]
    cos, sin = jnp.cos(ang), jnp.sin(ang)
    tpb = seq // tm
    rope_spec = pl.BlockSpec((tm, half), lambda i, j: (i % tpb, 0))
    q, k, v, g = _proj(
        h, w_in.astype(BF16),
        [(qk_total, functools.partial(_rotary_epi, 1.0), BF16),
         (qk_total, functools.partial(_rotary_epi, dk ** -0.5), BF16),
         (v_total, lambda acc, cos, sin: acc, BF16),
         (v_total, lambda acc, cos, sin: _silu(acc), BF16)],
        extra=(cos, sin), extra_specs=(rope_spec, rope_spec), tm=tm, tn=1024, name="ret_proj")
    a = _ret_core(q.reshape(bsz, seq, qk_total), k.reshape(bsz, seq, qk_total),
                  v.reshape(bsz, seq, v_total), g.reshape(bsz, seq, v_total), gn_w, gn_b)
    return [a.reshape(bsz * seq, v_total)]


def kernel(x, c, rel_bias, ada_w, ada_b, ln_w, ln_b, mlp_w1, mlp_w2, lru_w_in, lru_conv_w, lru_conv_b,
           lru_gate_w, lru_gate_b, lru_lambda, lru_w_out, nsa_w_in, nsa_cmp_pe, nsa_cmp_w1, nsa_cmp_b1,
           nsa_cmp_w2, nsa_w_out, ret_w_in, ret_gn_w, ret_gn_b, ret_w_out):
    bsz, seq, d = x.shape
    depth = ada_w.shape[0]
    mods = _ada_mods(c, ada_w, ada_b)
    xf = x.reshape(bsz * seq, d)
    h = _modulate(xf, mods, 0, seq)
    for layer in range(depth):
        mixer, inst = layer % N_MIXERS, layer // N_MIXERS
        sub = 2 * layer
        if mixer == 0:
            a_list = _lru_mixer(h, bsz, seq, lru_w_in[inst], lru_conv_w[inst], lru_conv_b[inst],
                                lru_gate_w[inst], lru_gate_b[inst], lru_lambda[inst])
            w_out = lru_w_out[inst]
        elif mixer == 1:
            a_list = _nsa_mixer(h, bsz, seq, rel_bias, nsa_w_in[inst], nsa_cmp_pe[inst], nsa_cmp_w1[inst],
                                nsa_cmp_b1[inst], nsa_cmp_w2[inst])
            w_out = nsa_w_out[inst]
        else:
            a_list = _ret_mixer(h, bsz, seq, ret_w_in[inst], ret_gn_w[inst], ret_gn_b[inst])
            w_out = ret_w_out[inst]
        xf, h = _out_ln(a_list, w_out.astype(BF16), xf, mods, sub, sub + 1, ln_w[layer, 0], ln_b[layer, 0], seq)
        sub_next = sub + 2 if layer + 1 < depth else None
        xf, h = _mlp(h, mlp_w1[layer].astype(BF16), mlp_w2[layer].astype(BF16), xf, mods, sub + 1, sub_next,
                     ln_w[layer, 1], ln_b[layer, 1], seq)
    return xf.reshape(bsz, seq, d)
```

```python
import functools
import math

import numpy as np
import jax
import jax.numpy as jnp
from jax import lax
from jax.experimental import pallas as pl
from jax.experimental.pallas import tpu as pltpu

F32 = jnp.float32
BF16 = jnp.bfloat16

DN_DEPTH = 4
N_MIXERS = 3
DN_ALPHA = (2 * DN_DEPTH) ** 0.25
LN_EPS = 1e-5
GN_EPS = 1e-5
LRU_BLOCKS = 8
CONV_WIDTH = 4
LRU_C = 8.0
NSA_HEADS = 16
NSA_HEAD_DIM = 128
NSA_KV_GROUPS = 4
NSA_HPG = NSA_HEADS // NSA_KV_GROUPS
N_BRANCH = 3
CMP_BLOCK = 32
CMP_STRIDE = 16
SEL_BLOCK = 64
SEL_TOP_N = 16
SEL_FORCE = 1e6
WINDOW = 512
NEG_BIG = 1e30
T5_BUCKETS = 32
T5_MAX_DIST = 1024
RET_HEADS = 8
RET_CHUNK = 128
ROPE_BASE = 10000.0

LANES = 128
VMEM_LIMIT_BYTES = 56 * 2 ** 20

NEG_MASK = -(2.0 ** 100)
MASK_BUCKET = T5_BUCKETS
F32_TINY = float(np.finfo(np.float32).tiny)
LOG2E = math.log2(math.e)

N_CMP_SLOTS = 4 * LANES
CMP_ROWS = CMP_STRIDE * N_CMP_SLOTS + 4 * CMP_STRIDE


def _cparams(*sem):
    return pltpu.CompilerParams(dimension_semantics=sem, vmem_limit_bytes=VMEM_LIMIT_BYTES)


def _sigmoid(x):
    return 1.0 / (1.0 + jnp.exp(-x))


def _silu(x):
    return x * _sigmoid(x)


def _gelu_tanh(x):
    return 0.5 * x * (1.0 + jnp.tanh(math.sqrt(2.0 / math.pi) * (x + 0.044715 * (x * x * x))))


def _dot(a, b):
    return jnp.dot(a, b, preferred_element_type=F32)


def _dot_nt(a, b):
    return lax.dot_general(a, b, (((1,), (1,)), ((), ())), preferred_element_type=F32)


def _dot_tn(a, b):
    return lax.dot_general(a, b, (((0,), (0,)), ((), ())), preferred_element_type=F32)


def _ada_kernel(c_ref, w_ref, b_ref, o_ref):
    cond = _silu(c_ref[...]).astype(BF16)
    o_ref[...] = _dot(cond, w_ref[...].astype(BF16)) + b_ref[...]


def _ada_mods(c, ada_w, ada_b):
    depth, _, d, d3 = ada_w.shape
    bsz = c.shape[0]
    n = depth * 2
    rows = 8
    c_pad = jnp.zeros((rows, d), F32).at[:bsz].set(c)
    tn = 1024
    out = pl.pallas_call(
        _ada_kernel,
        grid=(n, d3 // tn),
        in_specs=[
            pl.BlockSpec((rows, d), lambda s, j: (0, 0)),
            pl.BlockSpec((None, d, tn), lambda s, j: (s, 0, j)),
            pl.BlockSpec((None, 1, tn), lambda s, j: (s, 0, j)),
        ],
        out_specs=pl.BlockSpec((None, rows, tn), lambda s, j: (s, 0, j)),
        out_shape=jax.ShapeDtypeStruct((n, rows, d3), F32),
        compiler_params=_cparams("parallel", "parallel"),
        name="ada_mods",
    )(c_pad, ada_w.reshape(n, d, d3), ada_b.reshape(n, 1, d3))
    return out[:, :bsz].reshape(n, bsz, 3, d).transpose(0, 2, 1, 3).reshape(n, 3, bsz, 1, d)


SHIFT, SCALE, GATE = 0, 1, 2


def _mod_spec(d, sub, which, tiles_per_batch):
    return pl.BlockSpec((None, None, None, 1, d),
                        lambda i, *_: (sub, which, i // tiles_per_batch, 0, 0))


def _row_spec(d):
    return pl.BlockSpec((1, d), lambda *_: (0, 0))


def _modulate_kernel(x_ref, sc_ref, sh_ref, h_ref):
    h_ref[...] = (x_ref[...] * (1.0 + sc_ref[...]) + sh_ref[...]).astype(BF16)


def _modulate(x, mods, sub, seq, tm=512):
    t, d = x.shape
    tpb = seq // tm
    return pl.pallas_call(
        _modulate_kernel,
        grid=(t // tm,),
        in_specs=[pl.BlockSpec((tm, d), lambda i: (i, 0)),
                  _mod_spec(d, sub, SCALE, tpb), _mod_spec(d, sub, SHIFT, tpb)],
        out_specs=pl.BlockSpec((tm, d), lambda i: (i, 0)),
        out_shape=jax.ShapeDtypeStruct((t, d), BF16),
        compiler_params=_cparams("parallel"),
        name="modulate",
    )(x, mods, mods)


def _proj_kernel(segs, n_extra, h_ref, w_ref, *refs):
    extra = refs[:n_extra]
    outs = refs[n_extra:]
    j = pl.program_id(1)
    for (lo, hi, epi), o_ref in zip(segs, outs):
        def _store(o_ref=o_ref, epi=epi):
            acc = _dot(h_ref[...], w_ref[...])
            o_ref[...] = epi(acc, *[e[...] for e in extra]).astype(o_ref.dtype)
        if len(segs) == 1:
            _store()
        else:
            pl.when(jnp.logical_and(j >= lo, j < hi))(_store)


def _proj(h, w, segs, extra=(), extra_specs=(), tm=1024, tn=512, name="proj"):
    t, k = h.shape
    n = w.shape[1]
    bounds, lo = [], 0
    for n_cols, epi, _ in segs:
        assert n_cols % tn == 0
        bounds.append((lo, lo + n_cols // tn, epi))
        lo += n_cols // tn
    assert lo * tn == n
    out_specs = [
        pl.BlockSpec((tm, tn), functools.partial(
            lambda i, j, lo, hi: (i, jnp.clip(j - lo, 0, hi - lo - 1)), lo=lo_, hi=hi_))
        for lo_, hi_, _ in bounds]
    out_shape = [jax.ShapeDtypeStruct((t, n_cols), dt) for n_cols, _, dt in segs]
    return pl.pallas_call(
        functools.partial(_proj_kernel, bounds, len(extra)),
        grid=(t // tm, n // tn),
        in_specs=[pl.BlockSpec((tm, k), lambda i, j: (i, 0)),
                  pl.BlockSpec((k, tn), lambda i, j: (0, j))] + list(extra_specs),
        out_specs=out_specs,
        out_shape=out_shape,
        compiler_params=_cparams("parallel", "arbitrary"),
        name=name,
    )(h, w, *extra)


def _ln_mod_store(y, x_ref, gate_ref, lnw_ref, lnb_ref, sc_ref, sh_ref, xo_ref, ho_ref, rows=slice(None)):
    z = DN_ALPHA * x_ref[rows, :] + (1.0 + gate_ref[...]) * y
    mu = jnp.mean(z, axis=-1, keepdims=True)
    zc = z - mu
    var = jnp.mean(zc * zc, axis=-1, keepdims=True)
    xn = zc * lax.rsqrt(var + LN_EPS) * lnw_ref[...] + lnb_ref[...]
    xo_ref[rows, :] = xn
    if ho_ref is not None:
        ho_ref[rows, :] = (xn * (1.0 + sc_ref[...]) + sh_ref[...]).astype(BF16)


OUT_LN_CHUNKS = 2


def _out_ln_kernel(n_a, has_next, *refs):
    a_refs = refs[:n_a]
    w_ref, x_ref, gate_ref, lnw_ref, lnb_ref = refs[n_a:n_a + 5]
    rest = refs[n_a + 5:]
    if has_next:
        sc_ref, sh_ref, xo_ref, ho_ref = rest
    else:
        (xo_ref,), sc_ref, sh_ref, ho_ref = rest, None, None, None
    rc = x_ref.shape[0] // OUT_LN_CHUNKS
    for c in range(OUT_LN_CHUNKS):
        rows = slice(c * rc, (c + 1) * rc)
        a = a_refs[0][rows, :]
        for r in a_refs[1:]:
            a = a.astype(F32) + r[rows, :].astype(F32)
        y = _dot(a.astype(BF16), w_ref[...])
        _ln_mod_store(y, x_ref, gate_ref, lnw_ref, lnb_ref, sc_ref, sh_ref, xo_ref, ho_ref, rows)


def _out_ln(a_list, w, x, mods, sub, sub_next, ln_w, ln_b, seq, tm=512):
    t, d = x.shape
    k = w.shape[0]
    tpb = seq // tm
    has_next = sub_next is not None
    in_specs = [pl.BlockSpec((tm, k), lambda i: (i, 0)) for _ in a_list]
    in_specs += [pl.BlockSpec((k, d), lambda i: (0, 0), pipeline_mode=pl.Buffered(1)),
                 pl.BlockSpec((tm, d), lambda i: (i, 0)),
                 _mod_spec(d, sub, GATE, tpb), _row_spec(d), _row_spec(d)]
    args = list(a_list) + [w, x, mods, ln_w.reshape(1, d), ln_b.reshape(1, d)]
    out_specs = [pl.BlockSpec((tm, d), lambda i: (i, 0))]
    out_shape = [jax.ShapeDtypeStruct((t, d), F32)]
    if has_next:
        in_specs += [_mod_spec(d, sub_next, SCALE, tpb), _mod_spec(d, sub_next, SHIFT, tpb)]
        args += [mods, mods]
        out_specs.append(pl.BlockSpec((tm, d), lambda i: (i, 0)))
        out_shape.append(jax.ShapeDtypeStruct((t, d), BF16))
    res = pl.pallas_call(
        functools.partial(_out_ln_kernel, len(a_list), has_next),
        grid=(t // tm,),
        in_specs=in_specs, out_specs=out_specs, out_shape=out_shape,
        compiler_params=_cparams("parallel"),
        name="out_ln",
    )(*args)
    return (res[0], res[1]) if has_next else (res[0], None)


MLP_EPILOGUE_CHUNKS = 2


MLP_WEIGHT_SLOTS = 4


def _mlp_kernel(has_next, n_i, n_j, h_ref, w1_hbm, w2_hbm, x_hbm, gate_ref, lnw_ref, lnb_ref, *rest):
    if has_next:
        sc_ref, sh_ref, xo_hbm, ho_hbm, acc_ref, w1_buf, w2_buf, x_buf, xo_buf, ho_buf, sem, x_sem, o_sem = rest
    else:
        xo_hbm, acc_ref, w1_buf, w2_buf, x_buf, xo_buf, sem, x_sem, o_sem = rest
        sc_ref = sh_ref = ho_hbm = ho_buf = None
    i = pl.program_id(0)
    j = pl.program_id(1)
    last = n_j - 1
    tm = h_ref.shape[0]
    tf = w1_buf.shape[-1]
    step = i * n_j + j
    n_steps = n_i * n_j
    ahead = MLP_WEIGHT_SLOTS - 1

    def weight_copies(s):
        slot = s % MLP_WEIGHT_SLOTS
        col = (s % n_j) * tf
        if not isinstance(s, int):
            col = pl.multiple_of(col, tf)
        return (pltpu.make_async_copy(w1_hbm.at[:, pl.ds(col, tf)], w1_buf.at[slot], sem.at[0, slot]),
                pltpu.make_async_copy(w2_hbm.at[pl.ds(col, tf), :], w2_buf.at[slot], sem.at[1, slot]))

    @pl.when(step == 0)
    def _():
        for s in range(ahead):
            for cp in weight_copies(s):
                cp.start()

    @pl.when(step + ahead < n_steps)
    def _():
        for cp in weight_copies(step + ahead):
            cp.start()

    for cp in weight_copies(step):
        cp.wait()
    slot = step % MLP_WEIGHT_SLOTS

    def partial_out(rows):
        hid = _dot(h_ref[rows, :], w1_buf[slot])
        hid = jnp.square(jnp.maximum(hid, 0.0)).astype(BF16)
        return _dot(hid, w2_buf[slot])

    def x_copy(tile):
        return pltpu.make_async_copy(x_hbm.at[pl.ds(pl.multiple_of(tile * tm, tm), tm), :], x_buf, x_sem.at[0])

    def out_copies(tile):
        rows = pl.ds(pl.multiple_of(tile * tm, tm), tm)
        cps = [pltpu.make_async_copy(xo_buf, xo_hbm.at[rows, :], o_sem.at[0])]
        if has_next:
            cps.append(pltpu.make_async_copy(ho_buf, ho_hbm.at[rows, :], o_sem.at[1]))
        return cps

    @pl.when(j == 0)
    def _():
        x_copy(i).start()
        acc_ref[...] = partial_out(slice(None))

    @pl.when(jnp.logical_and(j > 0, j < last))
    def _():
        acc_ref[...] += partial_out(slice(None))

    @pl.when(jnp.logical_and(j == last, i > 0))
    def _():
        for cp in out_copies(i - 1):
            cp.wait()

    @pl.when(j == last)
    def _():
        x_copy(i).wait()
        rc = tm // MLP_EPILOGUE_CHUNKS
        for c in range(MLP_EPILOGUE_CHUNKS):
            rows = slice(c * rc, (c + 1) * rc)
            y = acc_ref[rows, :] + partial_out(rows)
            _ln_mod_store(y, x_buf, gate_ref, lnw_ref, lnb_ref, sc_ref, sh_ref, xo_buf, ho_buf, rows)
        for cp in out_copies(i):
            cp.start()

    @pl.when(step == n_steps - 1)
    def _():
        for cp in out_copies(i):
            cp.wait()


def _mlp(h, w1, w2, x, mods, sub, sub_next, ln_w, ln_b, seq, tm=512, tf=1024):
    t, d = x.shape
    ff = w1.shape[1]
    assert ff // tf >= 2 and (t // tm) * (ff // tf) >= MLP_WEIGHT_SLOTS
    tpb = seq // tm
    has_next = sub_next is not None
    in_specs = [pl.BlockSpec((tm, d), lambda i, j: (i, 0)),
                pl.BlockSpec(memory_space=pl.ANY),
                pl.BlockSpec(memory_space=pl.ANY),
                pl.BlockSpec(memory_space=pl.ANY),
                _mod_spec(d, sub, GATE, tpb), _row_spec(d), _row_spec(d)]
    args = [h, w1, w2, x, mods, ln_w.reshape(1, d), ln_b.reshape(1, d)]
    out_specs = [pl.BlockSpec(memory_space=pl.ANY)]
    out_shape = [jax.ShapeDtypeStruct((t, d), F32)]
    staging = [pltpu.VMEM((tm, d), F32), pltpu.VMEM((tm, d), F32)]
    if has_next:
        in_specs += [_mod_spec(d, sub_next, SCALE, tpb), _mod_spec(d, sub_next, SHIFT, tpb)]
        args += [mods, mods]
        out_specs.append(pl.BlockSpec(memory_space=pl.ANY))
        out_shape.append(jax.ShapeDtypeStruct((t, d), BF16))
        staging.append(pltpu.VMEM((tm, d), BF16))
    res = pl.pallas_call(
        functools.partial(_mlp_kernel, has_next, t // tm, ff // tf),
        grid=(t // tm, ff // tf),
        in_specs=in_specs, out_specs=out_specs, out_shape=out_shape,
        scratch_shapes=[pltpu.VMEM((tm, d), F32),
                        pltpu.VMEM((MLP_WEIGHT_SLOTS, d, tf), BF16), pltpu.VMEM((MLP_WEIGHT_SLOTS, tf, d), BF16),
                        *staging,
                        pltpu.SemaphoreType.DMA((2, MLP_WEIGHT_SLOTS)), pltpu.SemaphoreType.DMA((1,)),
                        pltpu.SemaphoreType.DMA((2,))],
        compiler_params=_cparams("arbitrary", "arbitrary"),
        name="mlp",
    )(*args)
    return (res[0], res[1]) if has_next else (res[0], None)


HALO = 8


def _lru_scan_kernel(ts, gy_ref, xb_ref, cw_ref, cb_ref, gw_ref, gb_ref, lam_ref, o_ref,
                     buf, a_scr, u_scr, h_scr):
    i = pl.program_id(1)
    width = xb_ref.shape[-1]
    bd = width // LRU_BLOCKS

    @pl.when(i == 0)
    def _():
        buf[0:HALO, :] = jnp.zeros((HALO, width), F32)
        h_scr[...] = jnp.zeros_like(h_scr)

    @pl.when(i > 0)
    def _():
        buf[0:HALO, :] = buf[ts:ts + HALO, :]

    buf[HALO:HALO + ts, :] = xb_ref[...]
    cw = cw_ref[...]
    taps = [buf[HALO - (CONV_WIDTH - 1) + k:HALO - (CONV_WIDTH - 1) + k + ts, :] * cw[k:k + 1, :]
            for k in range(CONV_WIDTH)]
    conv = taps[0]
    for tap in taps[1:]:
        conv = conv + tap
    xc = cb_ref[...] + conv
    xcb = xc.astype(BF16)
    gates = []
    for jg in range(2):
        cols = [_dot(xcb[:, n * bd:(n + 1) * bd], gw_ref[jg, n]) for n in range(LRU_BLOCKS)]
        gates.append(jnp.concatenate(cols, axis=1) + gb_ref[jg:jg + 1, :])
    r = 0.5 * (1.0 + jnp.tanh(0.5 * gates[0]))
    ig = 0.5 * (1.0 + jnp.tanh(0.5 * gates[1]))
    neg_lam = -lam_ref[...]
    softplus = jnp.maximum(neg_lam, 0.0) + jnp.log1p(jnp.exp(-jnp.abs(neg_lam)))
    log_a = -LRU_C * r * softplus
    a = jnp.exp(log_a)
    a_scr[...] = a
    one_m_a2 = -jnp.tanh(log_a) * (1.0 + a * a)
    root = one_m_a2 * lax.rsqrt(jnp.maximum(one_m_a2, F32_TINY))
    u_scr[...] = root * (ig * xc)

    def step(t, h):
        h = a_scr[pl.ds(t, 1), :] * h + u_scr[pl.ds(t, 1), :]
        u_scr[pl.ds(t, 1), :] = h
        return h

    h_scr[...] = lax.fori_loop(0, ts, step, h_scr[...], unroll=8)
    o_ref[...] = (gy_ref[...] * u_scr[...]).astype(BF16)


def _lru_scan(gy, xb, conv_w, conv_b, gate_w, gate_b, lam, ts=512):
    bsz, seq, width = xb.shape
    bd = width // LRU_BLOCKS
    tile = pl.BlockSpec((None, ts, width), lambda b, i: (b, i, 0))
    return pl.pallas_call(
        functools.partial(_lru_scan_kernel, ts),
        grid=(bsz, seq // ts),
        in_specs=[tile, tile,
                  pl.BlockSpec((CONV_WIDTH, width), lambda b, i: (0, 0)),
                  pl.BlockSpec((1, width), lambda b, i: (0, 0)),
                  pl.BlockSpec((2, LRU_BLOCKS, bd, bd), lambda b, i: (0, 0, 0, 0)),
                  pl.BlockSpec((2, width), lambda b, i: (0, 0)),
                  pl.BlockSpec((1, width), lambda b, i: (0, 0))],
        out_specs=tile,
        out_shape=jax.ShapeDtypeStruct((bsz, seq, width), BF16),
        scratch_shapes=[pltpu.VMEM((ts + HALO, width), F32), pltpu.VMEM((ts, width), F32),
                        pltpu.VMEM((ts, width), F32), pltpu.VMEM((1, width), F32)],
        compiler_params=_cparams("parallel", "arbitrary"),
        name="lru_scan",
    )(gy, xb, conv_w, conv_b.reshape(1, width), gate_w.astype(BF16), gate_b, lam.reshape(1, width))


def _lru_mixer(h, bsz, seq, w_in, conv_w, conv_b, gate_w, gate_b, lam):
    width = w_in.shape[1] // 2
    gy, xb = _proj(h, w_in.astype(BF16),
                   [(width, lambda acc: _gelu_tanh(acc), BF16), (width, lambda acc: acc, F32)],
                   tn=1024, name="lru_proj")
    a = _lru_scan(gy.reshape(bsz, seq, width), xb.reshape(bsz, seq, width),
                  conv_w, conv_b, gate_w, gate_b, lam)
    return [a.reshape(bsz * seq, width)]


def _t5_bucket_np(dist):
    n = np.maximum(np.asarray(dist, np.int64), 0)
    max_exact = T5_BUCKETS // 2
    n_large = T5_BUCKETS - max_exact
    ratio = T5_MAX_DIST // max_exact
    thresholds = []
    for k in range(1, n_large):
        m = max_exact
        while m ** n_large < ratio ** k * max_exact ** n_large:
            m += 1
        thresholds.append(m)
    large = max_exact + sum((n >= th).astype(np.int64) for th in thresholds)
    return np.where(n < max_exact, n, large).astype(np.int32)


def _bias_table_kernel(sub_far, rb_ref, idx_ref, o_ref):
    hd = pl.program_id(0)
    idx = idx_ref[...]
    base = rb_ref[T5_BUCKETS - 1, hd] if sub_far else 0.0
    acc = jnp.full(idx.shape, NEG_MASK, F32)
    for b in range(T5_BUCKETS):
        acc = jnp.where(idx == b, (rb_ref[b, hd] - base) * LOG2E, acc)
    o_ref[...] = acc


def _bias_table(rel_bias, idx_np, sub_far, name):
    rows, cols = idx_np.shape
    tr = 128
    n_heads = rel_bias.shape[1]
    return pl.pallas_call(
        functools.partial(_bias_table_kernel, sub_far),
        grid=(n_heads, rows // tr),
        in_specs=[pl.BlockSpec(memory_space=pltpu.SMEM),
                  pl.BlockSpec((tr, cols), lambda hd, i: (i, 0))],
        out_specs=pl.BlockSpec((None, tr, cols), lambda hd, i: (hd, i, 0)),
        out_shape=jax.ShapeDtypeStruct((n_heads, rows, cols), F32),
        compiler_params=_cparams("parallel", "parallel"),
        name=name,
    )(rel_bias, jnp.asarray(idx_np))


def _compress_kernel(seq, x_ref, pe_ref, w1_ref, b1_ref, w2_ref, o_ref, xs):
    xs[0:seq, :] = x_ref[...]
    xs[seq:CMP_ROWS, :] = jnp.zeros((CMP_ROWS - seq, LANES), F32)
    acc = jnp.zeros((N_CMP_SLOTS, LANES), F32)
    for l in range(CMP_BLOCK):
        rows = xs[pl.ds(l, N_CMP_SLOTS, stride=CMP_STRIDE), :] + pe_ref[l:l + 1, :]
        acc = acc + _dot(rows.astype(BF16), w1_ref[l])
    hid = _gelu_tanh(acc + b1_ref[...])
    o_ref[...] = _dot(hid.astype(BF16), w2_ref[...]).astype(BF16)


def _compress(cmp_tok, pe, w1, b1, w2):
    bsz, seq, _ = cmp_tok.shape
    g = NSA_KV_GROUPS
    dh = NSA_HEAD_DIM
    return pl.pallas_call(
        functools.partial(_compress_kernel, seq),
        grid=(bsz, 2, g),
        in_specs=[pl.BlockSpec((None, seq, dh), lambda b, j, gi: (b, 0, j * g + gi)),
                  pl.BlockSpec((None, CMP_BLOCK, dh), lambda b, j, gi: (j, 0, 0)),
                  pl.BlockSpec((None, CMP_BLOCK, dh, dh), lambda b, j, gi: (j, 0, 0, 0)),
                  pl.BlockSpec((None, 1, dh), lambda b, j, gi: (j, 0, 0)),
                  pl.BlockSpec((None, dh, dh), lambda b, j, gi: (j, 0, 0))],
        out_specs=pl.BlockSpec((None, None, None, N_CMP_SLOTS, dh), lambda b, j, gi: (b, j, gi, 0, 0)),
        out_shape=jax.ShapeDtypeStruct((bsz, 2, g, N_CMP_SLOTS, dh), BF16),
        scratch_shapes=[pltpu.VMEM((CMP_ROWS, dh), F32)],
        compiler_params=_cparams("parallel", "parallel", "parallel"),
        name="nsa_compress",
    )(cmp_tok, pe, w1.astype(BF16), b1.reshape(2, 1, dh), w2.astype(BF16))


CMP_COLS = N_CMP_SLOTS // LANES
CMP_COL_TOKENS = CMP_STRIDE * LANES


def _cmp_table_idx(tq):
    ql = np.arange(tq)[:, None]
    m = np.arange(N_CMP_SLOTS)[None, :]
    e = np.where(m <= (tq - CMP_BLOCK) // CMP_STRIDE, -m, N_CMP_SLOTS - m)
    dist = CMP_STRIDE * e + ql - (CMP_BLOCK - 1)
    return _t5_bucket_np(dist)


def _sel_sum_matrix():
    i = np.arange(N_CMP_SLOTS)[:, None]
    j = np.arange(LANES)[None, :]
    ratio = SEL_BLOCK // CMP_STRIDE
    return ((i >= ratio * j - 1) & (i <= ratio * j + ratio - 1)).astype(np.float32)


CMP_SUB = 2


def _cmp_attn_kernel(tq, q_ref, kc_ref, vc_ref, tb_ref, sm_ref, g_ref, o_ref, val_ref):
    i = pl.program_id(2)
    lane = lax.broadcasted_iota(jnp.int32, (tq, LANES), 1)
    row = lax.broadcasted_iota(jnp.int32, (tq, LANES), 0)

    def body(nc):
        width = nc * LANES
        kc = kc_ref[0:width, :]
        vc = vc_ref[0:width, :]
        ql = lax.broadcasted_iota(jnp.int32, (tq, width), 0)
        blk = lax.broadcasted_iota(jnp.int32, (tq, width), 1)
        rel = ql - CMP_STRIDE * blk - (CMP_BLOCK - 1)
        for sub in range(CMP_SUB):
            rows = slice(sub * tq, (sub + 1) * tq)
            tile = i * CMP_SUB + sub
            t0 = tile * tq
            gate = g_ref[rows, :]
            causal = jnp.where(t0 + rel >= 0, 0.0, NEG_MASK)
            row_ok = (t0 + lax.broadcasted_iota(jnp.int32, (tq, 1), 0)) >= CMP_BLOCK - 1
            shift = (tile * (tq // CMP_STRIDE)) % LANES
            keep = lane >= shift
            p_sum = jnp.zeros((tq, width), F32)
            for hd in range(NSA_HPG):
                rot = {}
                for c in range(nc):
                    for k in ((c - (nc - 1)) % CMP_COLS, (c - nc) % CMP_COLS):
                        if k not in rot:
                            rot[k] = pltpu.roll(tb_ref[hd, :, k * LANES:(k + 1) * LANES], shift, axis=1)
                tb = jnp.concatenate(
                    [jnp.where(keep, rot[(c - (nc - 1)) % CMP_COLS], rot[(c - nc) % CMP_COLS])
                     for c in range(nc)], axis=1)
                s = _dot_nt(q_ref[rows, hd * LANES:(hd + 1) * LANES], kc) + tb + causal
                e = jnp.exp2(s - jnp.max(s, axis=-1, keepdims=True))
                p = e * jnp.where(row_ok, 1.0 / jnp.sum(e, axis=-1, keepdims=True), 0.0)
                p_sum = p_sum + p
                o_ref[rows, hd * LANES:(hd + 1) * LANES] = (
                    _dot(p.astype(BF16), vc) * gate[:, hd:hd + 1]).astype(o_ref.dtype)
            sm = sm_ref[0:width, :]
            hi = p_sum.astype(BF16)
            r1 = p_sum - hi.astype(F32)
            mid = r1.astype(BF16)
            lo = (r1 - mid.astype(F32)).astype(BF16)
            score = (_dot(hi, sm) + _dot(mid, sm)) + _dot(lo, sm)
            cur = (t0 + row) >> 6
            forced = jnp.logical_or(lane == 0, jnp.logical_or(lane == cur, lane == cur - 1))
            val_ref[rows, :] = jnp.where(forced, SEL_FORCE, jnp.where(lane <= cur, score, -SEL_FORCE))

    last_q = (i + 1) * (CMP_SUB * tq) - 1
    n_cols = jnp.minimum((last_q - (CMP_BLOCK - 1)) // CMP_COL_TOKENS + 1, CMP_COLS)
    for nc in range(1, CMP_COLS + 1):
        pl.when(n_cols == nc)(functools.partial(body, nc))


N_FORCED = 3


def _topk_mask_kernel(n_top, val_ref, m_ref):
    val0 = val_ref[...]
    lane = lax.broadcasted_iota(jnp.int32, val0.shape, 1).astype(F32)
    forced = val0 >= 0.5 * SEL_FORCE
    val = jnp.where(forced, -jnp.inf, val0)
    sel = jnp.where(forced, 1.0, 0.0)
    for _ in range(n_top - N_FORCED):
        mx = jnp.max(val, axis=-1, keepdims=True)
        first = jnp.min(jnp.where(val == mx, lane, float(LANES)), axis=-1, keepdims=True)
        pick = lane == first
        sel = jnp.where(pick, 1.0, sel)
        val = jnp.where(pick, -jnp.inf, val)
    m_ref[...] = jnp.where(jnp.logical_and(sel > 0.0, val0 >= 0.0), 0.0, NEG_MASK).astype(BF16)


def _topk_mask(val, n_top, tr=1024):
    rows = val.shape[0]
    return pl.pallas_call(
        functools.partial(_topk_mask_kernel, n_top),
        grid=(rows // tr,),
        in_specs=[pl.BlockSpec((tr, LANES), lambda i: (i, 0))],
        out_specs=pl.BlockSpec((tr, LANES), lambda i: (i, 0)),
        out_shape=jax.ShapeDtypeStruct((rows, LANES), BF16),
        compiler_params=_cparams("parallel"),
        name="nsa_topk_mask",
    )(val)


def _cmp_attn(q, kvc, table, gates, tq=256):
    bsz, seq, _ = q.shape
    g = NSA_KV_GROUPS
    qw = NSA_HPG * NSA_HEAD_DIM
    rows = tq * CMP_SUB
    assert CMP_COL_TOKENS % rows == 0 and tq % CMP_STRIDE == 0
    return pl.pallas_call(
        functools.partial(_cmp_attn_kernel, tq),
        grid=(bsz, g, seq // rows),
        in_specs=[pl.BlockSpec((None, rows, qw), lambda b, gi, i: (b, i, gi)),
                  pl.BlockSpec((None, None, None, N_CMP_SLOTS, NSA_HEAD_DIM), lambda b, gi, i: (b, 0, gi, 0, 0)),
                  pl.BlockSpec((None, None, None, N_CMP_SLOTS, NSA_HEAD_DIM), lambda b, gi, i: (b, 1, gi, 0, 0)),
                  pl.BlockSpec((NSA_HPG, tq, N_CMP_SLOTS), lambda b, gi, i: (gi, 0, 0)),
                  pl.BlockSpec((N_CMP_SLOTS, LANES), lambda b, gi, i: (0, 0)),
                  pl.BlockSpec((None, rows, LANES), lambda b, gi, i: (b, i, gi))],
        out_specs=[pl.BlockSpec((None, rows, qw), lambda b, gi, i: (b, i, gi)),
                   pl.BlockSpec((None, None, rows, LANES), lambda b, gi, i: (b, gi, i, 0))],
        out_shape=[jax.ShapeDtypeStruct((bsz, seq, g * qw), BF16),
                   jax.ShapeDtypeStruct((bsz, g, seq, LANES), F32)],
        compiler_params=_cparams("parallel", "parallel", "parallel"),
        name="nsa_cmp_attn",
    )(q, kvc, kvc, table, jnp.asarray(_sel_sum_matrix(), BF16), gates)


SEL_TQ = 256
SEL_TK = 256
SEL_N_DELTA = -(-(T5_MAX_DIST - 1 + SEL_TK) // SEL_TQ)
SEL_N_NEAR = -(-SEL_N_DELTA // (SEL_TK // SEL_TQ))
SEL_STREAMS = 3


def _sel_table_idx():
    d = np.arange(SEL_N_DELTA)[:, None, None]
    ql = np.arange(SEL_TQ)[None, :, None]
    kl = np.arange(SEL_TK)[None, None, :]
    dist = ql - kl + SEL_TQ * d
    idx = np.where(dist >= 0, _t5_bucket_np(dist), MASK_BUCKET)
    return idx.reshape(SEL_N_DELTA * SEL_TQ, SEL_TK).astype(np.int32)


def _sel_attn_kernel(q_ref, m_ref, k_ref, v_ref, oh_ref, nb_ref, g_ref, o_ref, qa, m_scr, acc):
    tq, tk = SEL_TQ, SEL_TK
    ratio = tk // tq
    i = pl.program_id(2)
    kd = i // ratio
    sel_mask = m_ref[...]
    for hd in range(NSA_HPG):
        qa[hd * tq:(hd + 1) * tq, 0:LANES] = q_ref[:, hd * LANES:(hd + 1) * LANES]
        qa[hd * tq:(hd + 1) * tq, LANES:2 * LANES] = sel_mask
    m_scr[...] = jnp.full(m_scr.shape, -jnp.inf, F32)
    acc[...] = jnp.zeros_like(acc)
    n_kt = k_ref.shape[0] // tk
    ones = jnp.ones((tk, LANES), BF16)

    def update(st, tiles):
        scores, values = [], []
        for kt, valid, delta in tiles:
            k0 = pl.multiple_of(jnp.clip(kt, 0, n_kt - 1) * tk, tk)
            ka = jnp.concatenate([k_ref[pl.ds(k0, tk), :], oh_ref[pl.ds(k0, tk), :]], axis=1)
            s = _dot_nt(qa[...], ka)
            if valid is not None:
                s = s + jnp.where(valid, 0.0, NEG_MASK)
            if delta is not None:
                s = s + jnp.concatenate([nb_ref[hd, delta] for hd in range(NSA_HPG)], axis=0)
            scores.append(s)
            values.append(jnp.concatenate([v_ref[pl.ds(k0, tk), :], ones], axis=1))
        s_max = scores[0]
        for s in scores[1:]:
            s_max = jnp.maximum(s_max, s)
        m_prev = m_scr[st]
        m_new = jnp.maximum(m_prev, jnp.max(s_max, axis=-1, keepdims=True))
        alpha = jnp.exp2(m_prev - m_new)
        m_rep = jnp.concatenate([m_new] * (tk // LANES), axis=1)
        pv = None
        for s, va in zip(scores, values):
            part = _dot(jnp.exp2(s - m_rep).astype(BF16), va)
            pv = part if pv is None else pv + part
        acc[st] = jnp.concatenate([alpha, alpha], axis=1) * acc[st] + pv
        m_scr[st] = m_new

    n_far = jnp.maximum(kd - (SEL_N_NEAR - 1), 0)
    pair = 2 * SEL_STREAMS

    def far_pairs(it, carry):
        for st in range(SEL_STREAMS):
            kt = it * pair + 2 * st
            update(st, [(kt, None, None), (kt + 1, None, None)])
        return carry

    lax.fori_loop(0, n_far // pair, far_pairs, 0)
    base = (n_far // pair) * pair
    rem = n_far - base

    @pl.when(rem >= SEL_STREAMS)
    def _():
        for st in range(SEL_STREAMS):
            update(st, [(base + st, None, None)])

    base = base + jnp.where(rem >= SEL_STREAMS, SEL_STREAMS, 0)
    left = n_far - base
    slots = [(base + f, f < left, None) for f in range(SEL_STREAMS - 1)]
    for back in range(SEL_N_NEAR - 1, -1, -1):
        kt = kd - back
        slots.append((kt, kt >= 0, (i - kd * ratio) + back * ratio))
    for st in range(SEL_STREAMS):
        update(st, slots[st::SEL_STREAMS])

    m_all = m_scr[0]
    for st in range(1, SEL_STREAMS):
        m_all = jnp.maximum(m_all, m_scr[st])
    tot = jnp.zeros((NSA_HPG * tq, 2 * LANES), F32)
    for st in range(SEL_STREAMS):
        w = jnp.exp2(m_scr[st] - m_all)
        tot = tot + jnp.concatenate([w, w], axis=1) * acc[st]
    gate = g_ref[...]
    out = tot[:, :LANES] / jnp.maximum(tot[:, LANES:], 1e-30)
    for hd in range(NSA_HPG):
        o_ref[:, hd * LANES:(hd + 1) * LANES] = (
            out[hd * tq:(hd + 1) * tq, :] * gate[:, NSA_HPG + hd:NSA_HPG + hd + 1]).astype(o_ref.dtype)


def _sel_attn(q, sel_mask, kv_tok, onehot, table, gates):
    bsz, seq, _ = q.shape
    g = NSA_KV_GROUPS
    dh = NSA_HEAD_DIM
    qw = NSA_HPG * dh
    tq, tk = SEL_TQ, SEL_TK
    rows = NSA_HPG * tq
    return pl.pallas_call(
        _sel_attn_kernel,
        grid=(bsz, g, seq // tq),
        in_specs=[pl.BlockSpec((None, tq, qw), lambda b, gi, i: (b, i, gi)),
                  pl.BlockSpec((None, None, tq, LANES), lambda b, gi, i: (b, gi, i, 0)),
                  pl.BlockSpec((None, seq, dh), lambda b, gi, i: (b, 0, gi)),
                  pl.BlockSpec((None, seq, dh), lambda b, gi, i: (b, 0, g + gi)),
                  pl.BlockSpec((seq, LANES), lambda b, gi, i: (0, 0)),
                  pl.BlockSpec((NSA_HPG, SEL_N_DELTA, tq, tk), lambda b, gi, i: (gi, 0, 0, 0)),
                  pl.BlockSpec((None, tq, LANES), lambda b, gi, i: (b, i, gi))],
        out_specs=pl.BlockSpec((None, tq, qw), lambda b, gi, i: (b, i, gi)),
        out_shape=jax.ShapeDtypeStruct((bsz, seq, g * qw), BF16),
        scratch_shapes=[pltpu.VMEM((rows, 2 * LANES), BF16), pltpu.VMEM((SEL_STREAMS, rows, LANES), F32),
                        pltpu.VMEM((SEL_STREAMS, rows, 2 * dh), F32)],
        compiler_params=_cparams("parallel", "parallel", "parallel"),
        name="nsa_sel_attn",
    )(q, sel_mask, kv_tok, kv_tok, onehot, table, gates)


WIN_TQ = 128
WIN_SUB = 4
WIN_KEYS = WINDOW + WIN_TQ


def _win_table_idx():
    ql = np.arange(WIN_TQ)[:, None]
    kl = np.arange(WIN_KEYS)[None, :]
    dist = ql + WINDOW - kl
    return np.where((dist >= 0) & (dist < WINDOW), _t5_bucket_np(dist), MASK_BUCKET).astype(np.int32)


def _win_attn_kernel(seq, q_ref, k_ref, v_ref, wb_ref, g_ref, o_ref, kpad, vpad, qs):
    tq = WIN_TQ
    i = pl.program_id(2)

    @pl.when(i == 0)
    def _():
        kpad[0:WINDOW, :] = jnp.zeros((WINDOW, LANES), BF16)
        vpad[0:WINDOW, :] = jnp.zeros((WINDOW, LANES), BF16)
        kpad[WINDOW:WINDOW + seq, :] = k_ref[...]
        vpad[WINDOW:WINDOW + seq, :] = v_ref[...]

    bias = wb_ref[...].reshape(NSA_HPG * tq, WIN_KEYS)

    def step(header_visible):
        for sub in range(WIN_SUB):
            rows = slice(sub * tq, (sub + 1) * tq)
            t0 = pl.multiple_of((i * WIN_SUB + sub) * tq, tq)
            for hd in range(NSA_HPG):
                qs[sub, hd * tq:(hd + 1) * tq, :] = q_ref[rows, hd * LANES:(hd + 1) * LANES]
            keys = kpad[pl.ds(t0, WIN_KEYS), :]
            vals = vpad[pl.ds(t0, WIN_KEYS), :]
            s = _dot_nt(qs[sub], keys) + bias
            if header_visible:
                kl = lax.broadcasted_iota(jnp.int32, s.shape, 1)
                s = jnp.where(kl >= WINDOW - t0, s, NEG_MASK)
            e = jnp.exp2(s - jnp.max(s, axis=-1, keepdims=True))
            out = _dot(e.astype(BF16), vals) * (1.0 / jnp.sum(e, axis=-1, keepdims=True))
            gate = g_ref[rows, :]
            for hd in range(NSA_HPG):
                o_ref[rows, hd * LANES:(hd + 1) * LANES] = (
                    out[hd * tq:(hd + 1) * tq, :]
                    * gate[:, 2 * NSA_HPG + hd:2 * NSA_HPG + hd + 1]).astype(o_ref.dtype)

    n_header_steps = -(-WINDOW // (WIN_SUB * tq))
    pl.when(i < n_header_steps)(functools.partial(step, True))
    pl.when(i >= n_header_steps)(functools.partial(step, False))


def _win_attn(q, kv_tok, table, gates):
    bsz, seq, _ = q.shape
    g = NSA_KV_GROUPS
    dh = NSA_HEAD_DIM
    qw = NSA_HPG * dh
    tq = WIN_TQ
    rows = tq * WIN_SUB
    return pl.pallas_call(
        functools.partial(_win_attn_kernel, seq),
        grid=(bsz, g, seq // rows),
        in_specs=[pl.BlockSpec((None, rows, qw), lambda b, gi, i: (b, i, gi)),
                  pl.BlockSpec((None, seq, dh), lambda b, gi, i: (b, 0, 2 * g + gi)),
                  pl.BlockSpec((None, seq, dh), lambda b, gi, i: (b, 0, 3 * g + gi)),
                  pl.BlockSpec((NSA_HPG, tq, WIN_KEYS), lambda b, gi, i: (gi, 0, 0)),
                  pl.BlockSpec((None, rows, LANES), lambda b, gi, i: (b, i, gi))],
        out_specs=pl.BlockSpec((None, rows, qw), lambda b, gi, i: (b, i, gi)),
        out_shape=jax.ShapeDtypeStruct((bsz, seq, g * qw), BF16),
        scratch_shapes=[pltpu.VMEM((seq + WINDOW, dh), BF16), pltpu.VMEM((seq + WINDOW, dh), BF16),
                        pltpu.VMEM((WIN_SUB, NSA_HPG * tq, dh), BF16)],
        compiler_params=_cparams("parallel", "parallel", "arbitrary"),
        name="nsa_win_attn",
    )(q, kv_tok, kv_tok, table, gates)


def _nsa_mixer(h, bsz, seq, rel_bias, w_in, cmp_pe, cmp_w1, cmp_b1, cmp_w2):
    g, hpg, dh = NSA_KV_GROUPS, NSA_HPG, NSA_HEAD_DIM
    qw = NSA_HEADS * dh
    kvw = g * dh
    n_main = qw + 2 * N_BRANCH * kvw
    scale = dh ** -0.5 * LOG2E
    q, cmp_tok, kv_tok = _proj(
        h, w_in[:, :n_main].astype(BF16),
        [(qw, lambda acc: acc * scale, BF16), (2 * kvw, lambda acc: acc, F32), (4 * kvw, lambda acc: acc, BF16)],
        tn=1024, name="nsa_proj")
    src = np.zeros((g * LANES,), np.int32)
    valid = np.zeros((g * LANES,), bool)
    for gi in range(g):
        for br in range(N_BRANCH):
            for hd in range(hpg):
                src[gi * LANES + br * hpg + hd] = n_main + br * NSA_HEADS + gi * hpg + hd
                valid[gi * LANES + br * hpg + hd] = True
    w_gate = jnp.where(jnp.asarray(valid)[None, :], w_in[:, src], 0.0).astype(BF16)
    (gates,) = _proj(h, w_gate, [(g * LANES, lambda acc: _sigmoid(acc), F32)], name="nsa_gate_proj")

    q = q.reshape(bsz, seq, qw)
    gates = gates.reshape(bsz, seq, g * LANES)
    kv_tok = kv_tok.reshape(bsz, seq, 4 * kvw)
    kvc = _compress(cmp_tok.reshape(bsz, seq, 2 * kvw), cmp_pe, cmp_w1, cmp_b1, cmp_w2)

    n_top = min(SEL_TOP_N, seq // SEL_BLOCK)
    cmp_tq = 256
    cmp_table = _bias_table(rel_bias, _cmp_table_idx(cmp_tq), False, "nsa_cmp_bias")
    o_c, sel_val = _cmp_attn(q, kvc, cmp_table, gates, cmp_tq)
    sel_mask = _topk_mask(sel_val.reshape(bsz * g * seq, LANES), n_top).reshape(bsz, g, seq, LANES)

    sel_table = _bias_table(rel_bias, _sel_table_idx(), True, "nsa_sel_bias")
    sel_table = sel_table.reshape(NSA_HEADS, SEL_N_DELTA, SEL_TQ, SEL_TK)
    onehot = (jnp.arange(seq, dtype=jnp.int32)[:, None] // SEL_BLOCK
              == jnp.arange(LANES, dtype=jnp.int32)[None, :]).astype(BF16)
    o_s = _sel_attn(q, sel_mask, kv_tok, onehot, sel_table, gates)

    win_table = _bias_table(rel_bias, _win_table_idx(), False, "nsa_win_bias")
    o_w = _win_attn(q, kv_tok, win_table, gates)
    t = bsz * seq
    return [o_c.reshape(t, qw), o_s.reshape(t, qw), o_w.reshape(t, qw)]


def _rotary_epi(scale, acc, cos, sin):
    half = cos.shape[-1]
    outs = []
    for hd in range(acc.shape[1] // (2 * half)):
        x1 = acc[:, 2 * hd * half:(2 * hd + 1) * half]
        x2 = acc[:, (2 * hd + 1) * half:(2 * hd + 2) * half]
        outs += [x1 * cos - x2 * sin, x1 * sin + x2 * cos]
    rot = jnp.concatenate(outs, axis=1)
    return rot * scale if scale != 1.0 else rot


RET_HPS = 2


def _ret_kernel(tc, q_ref, k_ref, v_ref, g_ref, dec_ref, xi_ref, zeta_ref, cd_ref, gw_ref, gb_ref,
                o_ref, state):
    i = pl.program_id(2)
    dk = q_ref.shape[-1] // RET_HPS
    dv = v_ref.shape[-1] // RET_HPS

    @pl.when(i == 0)
    def _():
        state[...] = jnp.zeros_like(state)

    for c in range(tc // RET_CHUNK):
        rows = slice(c * RET_CHUNK, (c + 1) * RET_CHUNK)
        for hh in range(RET_HPS):
            qk_cols = slice(hh * dk, (hh + 1) * dk)
            v_cols = slice(hh * dv, (hh + 1) * dv)
            qc = q_ref[rows, qk_cols]
            kc = k_ref[rows, qk_cols]
            vc = v_ref[rows, v_cols]
            inner = _dot_nt(qc, kc) * dec_ref[hh]
            st = state[hh]
            out = _dot(inner.astype(BF16), vc) + _dot(qc, st.astype(BF16)) * xi_ref[hh]
            kz = (kc.astype(F32) * zeta_ref[hh]).astype(BF16)
            state[hh] = st * cd_ref[hh] + _dot_tn(kz, vc)
            mu = jnp.mean(out, axis=-1, keepdims=True)
            oc = out - mu
            var = jnp.mean(oc * oc, axis=-1, keepdims=True)
            y = (oc * lax.rsqrt(var + GN_EPS)) * gw_ref[:, v_cols] + gb_ref[:, v_cols]
            o_ref[rows, v_cols] = (g_ref[rows, v_cols] * y).astype(BF16)


def _ret_core(q, k, v, g, gn_w, gn_b, tc=512):
    bsz, seq, qk_total = q.shape
    v_total = v.shape[-1]
    nh = RET_HEADS
    dk = qk_total // nh
    dv = v_total // nh
    cs = RET_CHUNK
    hps = RET_HPS
    log_g = jnp.log1p(-jnp.exp2(-5.0 - jnp.arange(nh, dtype=F32)))
    idx = jnp.arange(cs, dtype=F32)
    diff = idx[:, None] - idx[None, :]
    decay = jnp.where(diff >= 0, jnp.exp(jnp.maximum(diff, 0.0) * log_g[:, None, None]), 0.0)
    xi = jnp.exp((idx + 1.0) * log_g[:, None])[:, :, None]
    zeta = jnp.exp((cs - 1.0 - idx) * log_g[:, None])[:, :, None]
    chunk_decay = jnp.exp(cs * log_g).reshape(nh, 1, 1)
    return pl.pallas_call(
        functools.partial(_ret_kernel, tc),
        grid=(bsz, nh // hps, seq // tc),
        in_specs=[pl.BlockSpec((None, tc, hps * dk), lambda b, hp, i: (b, i, hp)),
                  pl.BlockSpec((None, tc, hps * dk), lambda b, hp, i: (b, i, hp)),
                  pl.BlockSpec((None, tc, hps * dv), lambda b, hp, i: (b, i, hp)),
                  pl.BlockSpec((None, tc, hps * dv), lambda b, hp, i: (b, i, hp)),
                  pl.BlockSpec((hps, cs, cs), lambda b, hp, i: (hp, 0, 0)),
                  pl.BlockSpec((hps, cs, 1), lambda b, hp, i: (hp, 0, 0)),
                  pl.BlockSpec((hps, cs, 1), lambda b, hp, i: (hp, 0, 0)),
                  pl.BlockSpec((hps, 1, 1), lambda b, hp, i: (hp, 0, 0)),
                  pl.BlockSpec((1, hps * dv), lambda b, hp, i: (0, hp)),
                  pl.BlockSpec((1, hps * dv), lambda b, hp, i: (0, hp))],
        out_specs=pl.BlockSpec((None, tc, hps * dv), lambda b, hp, i: (b, i, hp)),
        out_shape=jax.ShapeDtypeStruct((bsz, seq, v_total), BF16),
        scratch_shapes=[pltpu.VMEM((hps, dk, dv), F32)],
        compiler_params=_cparams("parallel", "parallel", "arbitrary"),
        name="ret_core",
    )(q, k, v, g, decay, xi, zeta, chunk_decay, gn_w.reshape(1, v_total), gn_b.reshape(1, v_total))


def _ret_mixer(h, bsz, seq, w_in, gn_w, gn_b, tm=1024):
    qk_total = w_in.shape[0]
    v_total = (w_in.shape[1] - 2 * qk_total) // 2
    dk = qk_total // RET_HEADS
    half = dk // 2
    pos = jnp.arange(seq, dtype=F32)
    inv = ROPE_BASE ** (-jnp.arange(half, dtype=F32) / half)
    ang = pos[:, None] * inv[None, :]
    cos, sin = jnp.cos(ang), jnp.sin(ang)
    tpb = seq // tm
    rope_spec = pl.BlockSpec((tm, half), lambda i, j: (i % tpb, 0))
    q, k, v, g = _proj(
        h, w_in.astype(BF16),
        [(qk_total, functools.partial(_rotary_epi, 1.0), BF16),
         (qk_total, functools.partial(_rotary_epi, dk ** -0.5), BF16),
         (v_total, lambda acc, cos, sin: acc, BF16),
         (v_total, lambda acc, cos, sin: _silu(acc), BF16)],
        extra=(cos, sin), extra_specs=(rope_spec, rope_spec), tm=tm, tn=1024, name="ret_proj")
    a = _ret_core(q.reshape(bsz, seq, qk_total), k.reshape(bsz, seq, qk_total),
                  v.reshape(bsz, seq, v_total), g.reshape(bsz, seq, v_total), gn_w, gn_b)
    return [a.reshape(bsz * seq, v_total)]


def kernel(x, c, rel_bias, ada_w, ada_b, ln_w, ln_b, mlp_w1, mlp_w2, lru_w_in, lru_conv_w, lru_conv_b,
           lru_gate_w, lru_gate_b, lru_lambda, lru_w_out, nsa_w_in, nsa_cmp_pe, nsa_cmp_w1, nsa_cmp_b1,
           nsa_cmp_w2, nsa_w_out, ret_w_in, ret_gn_w, ret_gn_b, ret_w_out):
    bsz, seq, d = x.shape
    depth = ada_w.shape[0]
    mods = _ada_mods(c, ada_w, ada_b)
    xf = x.reshape(bsz * seq, d)
    h = _modulate(xf, mods, 0, seq)
    for layer in range(depth):
        mixer, inst = layer % N_MIXERS, layer // N_MIXERS
        sub = 2 * layer
        if mixer == 0:
            a_list = _lru_mixer(h, bsz, seq, lru_w_in[inst], lru_conv_w[inst], lru_conv_b[inst],
                                lru_gate_w[inst], lru_gate_b[inst], lru_lambda[inst])
            w_out = lru_w_out[inst]
        elif mixer == 1:
            a_list = _nsa_mixer(h, bsz, seq, rel_bias, nsa_w_in[inst], nsa_cmp_pe[inst], nsa_cmp_w1[inst],
                                nsa_cmp_b1[inst], nsa_cmp_w2[inst])
            w_out = nsa_w_out[inst]
        else:
            a_list = _ret_mixer(h, bsz, seq, ret_w_in[inst], ret_gn_w[inst], ret_gn_b[inst])
            w_out = ret_w_out[inst]
        xf, h = _out_ln(a_list, w_out.astype(BF16), xf, mods, sub, sub + 1, ln_w[layer, 0], ln_b[layer, 0], seq)
        sub_next = sub + 2 if layer + 1 < depth else None
        xf, h = _mlp(h, mlp_w1[layer].astype(BF16), mlp_w2[layer].astype(BF16), xf, mods, sub + 1, sub_next,
                     ln_w[layer, 1], ln_b[layer, 1], seq)
    return xf.reshape(bsz, seq, d)
```

```python
import functools
import math

import numpy as np
import jax
import jax.numpy as jnp
from jax import lax
from jax.experimental import pallas as pl
from jax.experimental.pallas import tpu as pltpu

F32 = jnp.float32
BF16 = jnp.bfloat16

DN_DEPTH = 4
N_MIXERS = 3
DN_ALPHA = (2 * DN_DEPTH) ** 0.25
LN_EPS = 1e-5
GN_EPS = 1e-5
LRU_BLOCKS = 8
CONV_WIDTH = 4
LRU_C = 8.0
NSA_HEADS = 16
NSA_HEAD_DIM = 128
NSA_KV_GROUPS = 4
NSA_HPG = NSA_HEADS // NSA_KV_GROUPS
N_BRANCH = 3
CMP_BLOCK = 32
CMP_STRIDE = 16
SEL_BLOCK = 64
SEL_TOP_N = 16
SEL_FORCE = 1e6
WINDOW = 512
NEG_BIG = 1e30
T5_BUCKETS = 32
T5_MAX_DIST = 1024
RET_HEADS = 8
ROPE_BASE = 10000.0

LANES = 128
VMEM_LIMIT_BYTES = 56 * 2 ** 20

NEG_MASK = -(2.0 ** 100)
MASK_BUCKET = T5_BUCKETS
F32_TINY = float(np.finfo(np.float32).tiny)
LOG2E = math.log2(math.e)

N_CMP_SLOTS = 4 * LANES
CMP_ROWS = CMP_STRIDE * N_CMP_SLOTS + 4 * CMP_STRIDE


def _cparams(*sem):
    return pltpu.CompilerParams(dimension_semantics=sem, vmem_limit_bytes=VMEM_LIMIT_BYTES)


def _sigmoid(x):
    return 1.0 / (1.0 + jnp.exp(-x))


def _silu(x):
    return x * _sigmoid(x)


def _gelu_tanh(x):
    return 0.5 * x * (1.0 + jnp.tanh(math.sqrt(2.0 / math.pi) * (x + 0.044715 * (x * x * x))))


def _dot(a, b):
    return jnp.dot(a, b, preferred_element_type=F32)


def _dot_nt(a, b):
    return lax.dot_general(a, b, (((1,), (1,)), ((), ())), preferred_element_type=F32)


def _dot_tn(a, b):
    return lax.dot_general(a, b, (((0,), (0,)), ((), ())), preferred_element_type=F32)


def _ada_kernel(c_ref, w_ref, b_ref, o_ref):
    cond = _silu(c_ref[...]).astype(BF16)
    o_ref[...] = _dot(cond, w_ref[...].astype(BF16)) + b_ref[...]


def _ada_mods(c, ada_w, ada_b):
    depth, _, d, d3 = ada_w.shape
    bsz = c.shape[0]
    n = depth * 2
    rows = 8
    c_pad = jnp.zeros((rows, d), F32).at[:bsz].set(c)
    tn = 1024
    out = pl.pallas_call(
        _ada_kernel,
        grid=(n, d3 // tn),
        in_specs=[
            pl.BlockSpec((rows, d), lambda s, j: (0, 0)),
            pl.BlockSpec((None, d, tn), lambda s, j: (s, 0, j)),
            pl.BlockSpec((None, 1, tn), lambda s, j: (s, 0, j)),
        ],
        out_specs=pl.BlockSpec((None, rows, tn), lambda s, j: (s, 0, j)),
        out_shape=jax.ShapeDtypeStruct((n, rows, d3), F32),
        compiler_params=_cparams("parallel", "parallel"),
        name="ada_mods",
    )(c_pad, ada_w.reshape(n, d, d3), ada_b.reshape(n, 1, d3))
    return out[:, :bsz].reshape(n, bsz, 3, d).transpose(0, 2, 1, 3).reshape(n, 3, bsz, 1, d)


SHIFT, SCALE, GATE = 0, 1, 2


def _mod_spec(d, sub, which, tiles_per_batch):
    return pl.BlockSpec((None, None, None, 1, d),
                        lambda i, *_: (sub, which, i // tiles_per_batch, 0, 0))


def _row_spec(d):
    return pl.BlockSpec((1, d), lambda *_: (0, 0))


def _modulate_kernel(x_ref, sc_ref, sh_ref, h_ref):
    h_ref[...] = (x_ref[...] * (1.0 + sc_ref[...]) + sh_ref[...]).astype(BF16)


def _modulate(x, mods, sub, seq, tm=512):
    t, d = x.shape
    tpb = seq // tm
    return pl.pallas_call(
        _modulate_kernel,
        grid=(t // tm,),
        in_specs=[pl.BlockSpec((tm, d), lambda i: (i, 0)),
                  _mod_spec(d, sub, SCALE, tpb), _mod_spec(d, sub, SHIFT, tpb)],
        out_specs=pl.BlockSpec((tm, d), lambda i: (i, 0)),
        out_shape=jax.ShapeDtypeStruct((t, d), BF16),
        compiler_params=_cparams("parallel"),
        name="modulate",
    )(x, mods, mods)


def _proj_kernel(segs, n_extra, h_ref, w_ref, *refs):
    extra = refs[:n_extra]
    outs = refs[n_extra:]
    j = pl.program_id(1)
    for (lo, hi, epi), o_ref in zip(segs, outs):
        def _store(o_ref=o_ref, epi=epi):
            acc = _dot(h_ref[...], w_ref[...])
            o_ref[...] = epi(acc, *[e[...] for e in extra]).astype(o_ref.dtype)
        if len(segs) == 1:
            _store()
        else:
            pl.when(jnp.logical_and(j >= lo, j < hi))(_store)


def _proj(h, w, segs, extra=(), extra_specs=(), tm=1024, tn=512, name="proj"):
    t, k = h.shape
    n = w.shape[1]
    bounds, lo = [], 0
    for n_cols, epi, _ in segs:
        assert n_cols % tn == 0
        bounds.append((lo, lo + n_cols // tn, epi))
        lo += n_cols // tn
    assert lo * tn == n
    out_specs = [
        pl.BlockSpec((tm, tn), functools.partial(
            lambda i, j, lo, hi: (i, jnp.clip(j - lo, 0, hi - lo - 1)), lo=lo_, hi=hi_))
        for lo_, hi_, _ in bounds]
    out_shape = [jax.ShapeDtypeStruct((t, n_cols), dt) for n_cols, _, dt in segs]
    return pl.pallas_call(
        functools.partial(_proj_kernel, bounds, len(extra)),
        grid=(t // tm, n // tn),
        in_specs=[pl.BlockSpec((tm, k), lambda i, j: (i, 0)),
                  pl.BlockSpec((k, tn), lambda i, j: (0, j))] + list(extra_specs),
        out_specs=out_specs,
        out_shape=out_shape,
        compiler_params=_cparams("parallel", "arbitrary"),
        name=name,
    )(h, w, *extra)


def _ln_mod_store(y, x_ref, gate_ref, lnw_ref, lnb_ref, sc_ref, sh_ref, xo_ref, ho_ref, rows=slice(None)):
    z = DN_ALPHA * x_ref[rows, :] + (1.0 + gate_ref[...]) * y
    mu = jnp.mean(z, axis=-1, keepdims=True)
    zc = z - mu
    var = jnp.mean(zc * zc, axis=-1, keepdims=True)
    xn = zc * lax.rsqrt(var + LN_EPS) * lnw_ref[...] + lnb_ref[...]
    xo_ref[rows, :] = xn
    if ho_ref is not None:
        ho_ref[rows, :] = (xn * (1.0 + sc_ref[...]) + sh_ref[...]).astype(BF16)


OUT_LN_CHUNKS = 2


def _out_ln_kernel(n_a, has_next, *refs):
    a_refs = refs[:n_a]
    w_ref, x_ref, gate_ref, lnw_ref, lnb_ref = refs[n_a:n_a + 5]
    rest = refs[n_a + 5:]
    if has_next:
        sc_ref, sh_ref, xo_ref, ho_ref = rest
    else:
        (xo_ref,), sc_ref, sh_ref, ho_ref = rest, None, None, None
    rc = x_ref.shape[0] // OUT_LN_CHUNKS
    for c in range(OUT_LN_CHUNKS):
        rows = slice(c * rc, (c + 1) * rc)
        a = a_refs[0][rows, :]
        for r in a_refs[1:]:
            a = a.astype(F32) + r[rows, :].astype(F32)
        y = _dot(a.astype(BF16), w_ref[...])
        _ln_mod_store(y, x_ref, gate_ref, lnw_ref, lnb_ref, sc_ref, sh_ref, xo_ref, ho_ref, rows)


def _out_ln(a_list, w, x, mods, sub, sub_next, ln_w, ln_b, seq, tm=512):
    t, d = x.shape
    k = w.shape[0]
    tpb = seq // tm
    has_next = sub_next is not None
    in_specs = [pl.BlockSpec((tm, k), lambda i: (i, 0)) for _ in a_list]
    in_specs += [pl.BlockSpec((k, d), lambda i: (0, 0), pipeline_mode=pl.Buffered(1)),
                 pl.BlockSpec((tm, d), lambda i: (i, 0)),
                 _mod_spec(d, sub, GATE, tpb), _row_spec(d), _row_spec(d)]
    args = list(a_list) + [w, x, mods, ln_w.reshape(1, d), ln_b.reshape(1, d)]
    out_specs = [pl.BlockSpec((tm, d), lambda i: (i, 0))]
    out_shape = [jax.ShapeDtypeStruct((t, d), F32)]
    if has_next:
        in_specs += [_mod_spec(d, sub_next, SCALE, tpb), _mod_spec(d, sub_next, SHIFT, tpb)]
        args += [mods, mods]
        out_specs.append(pl.BlockSpec((tm, d), lambda i: (i, 0)))
        out_shape.append(jax.ShapeDtypeStruct((t, d), BF16))
    res = pl.pallas_call(
        functools.partial(_out_ln_kernel, len(a_list), has_next),
        grid=(t // tm,),
        in_specs=in_specs, out_specs=out_specs, out_shape=out_shape,
        compiler_params=_cparams("parallel"),
        name="out_ln",
    )(*args)
    return (res[0], res[1]) if has_next else (res[0], None)


MLP_EPILOGUE_CHUNKS = 2


MLP_WEIGHT_SLOTS = 3


def _mlp_kernel(has_next, n_i, n_j, h_ref, w1_hbm, w2_hbm, x_hbm, gate_ref, lnw_ref, lnb_ref, *rest):
    if has_next:
        sc_ref, sh_ref, xo_hbm, ho_hbm, acc_ref, w1_buf, w2_buf, x_buf, xo_buf, ho_buf, sem, x_sem, o_sem = rest
    else:
        xo_hbm, acc_ref, w1_buf, w2_buf, x_buf, xo_buf, sem, x_sem, o_sem = rest
        sc_ref = sh_ref = ho_hbm = ho_buf = None
    i = pl.program_id(0)
    j = pl.program_id(1)
    last = n_j - 1
    tm = h_ref.shape[0]
    tf = w1_buf.shape[-1]
    step = i * n_j + j
    n_steps = n_i * n_j
    ahead = MLP_WEIGHT_SLOTS - 1

    def weight_copies(s):
        slot = s % MLP_WEIGHT_SLOTS
        col = (s % n_j) * tf
        if not isinstance(s, int):
            col = pl.multiple_of(col, tf)
        return (pltpu.make_async_copy(w1_hbm.at[:, pl.ds(col, tf)], w1_buf.at[slot], sem.at[0, slot]),
                pltpu.make_async_copy(w2_hbm.at[pl.ds(col, tf), :], w2_buf.at[slot], sem.at[1, slot]))

    @pl.when(step == 0)
    def _():
        for s in range(ahead):
            for cp in weight_copies(s):
                cp.start()

    @pl.when(step + ahead < n_steps)
    def _():
        for cp in weight_copies(step + ahead):
            cp.start()

    for cp in weight_copies(step):
        cp.wait()
    slot = step % MLP_WEIGHT_SLOTS

    def partial_out(rows):
        hid = _dot(h_ref[rows, :], w1_buf[slot])
        hid = jnp.square(jnp.maximum(hid, 0.0)).astype(BF16)
        return _dot(hid, w2_buf[slot])

    def x_copy(tile):
        return pltpu.make_async_copy(x_hbm.at[pl.ds(pl.multiple_of(tile * tm, tm), tm), :], x_buf, x_sem.at[0])

    def out_copies(tile):
        rows = pl.ds(pl.multiple_of(tile * tm, tm), tm)
        cps = [pltpu.make_async_copy(xo_buf, xo_hbm.at[rows, :], o_sem.at[0])]
        if has_next:
            cps.append(pltpu.make_async_copy(ho_buf, ho_hbm.at[rows, :], o_sem.at[1]))
        return cps

    @pl.when(j == 0)
    def _():
        x_copy(i).start()
        acc_ref[...] = partial_out(slice(None))

    @pl.when(jnp.logical_and(j > 0, j < last))
    def _():
        acc_ref[...] += partial_out(slice(None))

    @pl.when(jnp.logical_and(j == last, i > 0))
    def _():
        for cp in out_copies(i - 1):
            cp.wait()

    @pl.when(j == last)
    def _():
        x_copy(i).wait()
        rc = tm // MLP_EPILOGUE_CHUNKS
        for c in range(MLP_EPILOGUE_CHUNKS):
            rows = slice(c * rc, (c + 1) * rc)
            y = acc_ref[rows, :] + partial_out(rows)
            _ln_mod_store(y, x_buf, gate_ref, lnw_ref, lnb_ref, sc_ref, sh_ref, xo_buf, ho_buf, rows)
        for cp in out_copies(i):
            cp.start()

    @pl.when(step == n_steps - 1)
    def _():
        for cp in out_copies(i):
            cp.wait()


def _mlp(h, w1, w2, x, mods, sub, sub_next, ln_w, ln_b, seq, tm=512, tf=1024):
    t, d = x.shape
    ff = w1.shape[1]
    assert ff // tf >= 2 and (t // tm) * (ff // tf) >= MLP_WEIGHT_SLOTS
    tpb = seq // tm
    has_next = sub_next is not None
    in_specs = [pl.BlockSpec((tm, d), lambda i, j: (i, 0)),
                pl.BlockSpec(memory_space=pl.ANY),
                pl.BlockSpec(memory_space=pl.ANY),
                pl.BlockSpec(memory_space=pl.ANY),
                _mod_spec(d, sub, GATE, tpb), _row_spec(d), _row_spec(d)]
    args = [h, w1, w2, x, mods, ln_w.reshape(1, d), ln_b.reshape(1, d)]
    out_specs = [pl.BlockSpec(memory_space=pl.ANY)]
    out_shape = [jax.ShapeDtypeStruct((t, d), F32)]
    staging = [pltpu.VMEM((tm, d), F32), pltpu.VMEM((tm, d), F32)]
    if has_next:
        in_specs += [_mod_spec(d, sub_next, SCALE, tpb), _mod_spec(d, sub_next, SHIFT, tpb)]
        args += [mods, mods]
        out_specs.append(pl.BlockSpec(memory_space=pl.ANY))
        out_shape.append(jax.ShapeDtypeStruct((t, d), BF16))
        staging.append(pltpu.VMEM((tm, d), BF16))
    res = pl.pallas_call(
        functools.partial(_mlp_kernel, has_next, t // tm, ff // tf),
        grid=(t // tm, ff // tf),
        in_specs=in_specs, out_specs=out_specs, out_shape=out_shape,
        scratch_shapes=[pltpu.VMEM((tm, d), F32),
                        pltpu.VMEM((MLP_WEIGHT_SLOTS, d, tf), BF16), pltpu.VMEM((MLP_WEIGHT_SLOTS, tf, d), BF16),
                        *staging,
                        pltpu.SemaphoreType.DMA((2, MLP_WEIGHT_SLOTS)), pltpu.SemaphoreType.DMA((1,)),
                        pltpu.SemaphoreType.DMA((2,))],
        compiler_params=_cparams("arbitrary", "arbitrary"),
        name="mlp",
    )(*args)
    return (res[0], res[1]) if has_next else (res[0], None)


HALO = 8


def _lru_scan_kernel(ts, gy_ref, xb_ref, cw_ref, cb_ref, gw_ref, gb_ref, lam_ref, o_ref,
                     buf, a_scr, u_scr, h_scr):
    i = pl.program_id(1)
    width = xb_ref.shape[-1]
    bd = width // LRU_BLOCKS

    @pl.when(i == 0)
    def _():
        buf[0:HALO, :] = jnp.zeros((HALO, width), F32)
        h_scr[...] = jnp.zeros_like(h_scr)

    @pl.when(i > 0)
    def _():
        buf[0:HALO, :] = buf[ts:ts + HALO, :]

    buf[HALO:HALO + ts, :] = xb_ref[...]
    cw = cw_ref[...]
    taps = [buf[HALO - (CONV_WIDTH - 1) + k:HALO - (CONV_WIDTH - 1) + k + ts, :] * cw[k:k + 1, :]
            for k in range(CONV_WIDTH)]
    conv = taps[0]
    for tap in taps[1:]:
        conv = conv + tap
    xc = cb_ref[...] + conv
    xcb = xc.astype(BF16)
    gates = []
    for jg in range(2):
        cols = [_dot(xcb[:, n * bd:(n + 1) * bd], gw_ref[jg, n]) for n in range(LRU_BLOCKS)]
        gates.append(jnp.concatenate(cols, axis=1) + gb_ref[jg:jg + 1, :])
    r = 0.5 * (1.0 + jnp.tanh(0.5 * gates[0]))
    ig = 0.5 * (1.0 + jnp.tanh(0.5 * gates[1]))
    neg_lam = -lam_ref[...]
    softplus = jnp.maximum(neg_lam, 0.0) + jnp.log1p(jnp.exp(-jnp.abs(neg_lam)))
    log_a = -LRU_C * r * softplus
    a = jnp.exp(log_a)
    a_scr[...] = a
    one_m_a2 = -jnp.tanh(log_a) * (1.0 + a * a)
    root = one_m_a2 * lax.rsqrt(jnp.maximum(one_m_a2, F32_TINY))
    u_scr[...] = root * (ig * xc)

    def step(t, h):
        h = a_scr[pl.ds(t, 1), :] * h + u_scr[pl.ds(t, 1), :]
        u_scr[pl.ds(t, 1), :] = h
        return h

    h_scr[...] = lax.fori_loop(0, ts, step, h_scr[...], unroll=8)
    o_ref[...] = (gy_ref[...] * u_scr[...]).astype(BF16)


def _lru_scan(gy, xb, conv_w, conv_b, gate_w, gate_b, lam, ts=512):
    bsz, seq, width = xb.shape
    bd = width // LRU_BLOCKS
    tile = pl.BlockSpec((None, ts, width), lambda b, i: (b, i, 0))
    return pl.pallas_call(
        functools.partial(_lru_scan_kernel, ts),
        grid=(bsz, seq // ts),
        in_specs=[tile, tile,
                  pl.BlockSpec((CONV_WIDTH, width), lambda b, i: (0, 0)),
                  pl.BlockSpec((1, width), lambda b, i: (0, 0)),
                  pl.BlockSpec((2, LRU_BLOCKS, bd, bd), lambda b, i: (0, 0, 0, 0)),
                  pl.BlockSpec((2, width), lambda b, i: (0, 0)),
                  pl.BlockSpec((1, width), lambda b, i: (0, 0))],
        out_specs=tile,
        out_shape=jax.ShapeDtypeStruct((bsz, seq, width), BF16),
        scratch_shapes=[pltpu.VMEM((ts + HALO, width), F32), pltpu.VMEM((ts, width), F32),
                        pltpu.VMEM((ts, width), F32), pltpu.VMEM((1, width), F32)],
        compiler_params=_cparams("parallel", "arbitrary"),
        name="lru_scan",
    )(gy, xb, conv_w, conv_b.reshape(1, width), gate_w.astype(BF16), gate_b, lam.reshape(1, width))


def _lru_mixer(h, bsz, seq, w_in, conv_w, conv_b, gate_w, gate_b, lam):
    width = w_in.shape[1] // 2
    gy, xb = _proj(h, w_in.astype(BF16),
                   [(width, lambda acc: _gelu_tanh(acc), BF16), (width, lambda acc: acc, F32)],
                   tn=1024, name="lru_proj")
    a = _lru_scan(gy.reshape(bsz, seq, width), xb.reshape(bsz, seq, width),
                  conv_w, conv_b, gate_w, gate_b, lam)
    return [a.reshape(bsz * seq, width)]


def _t5_bucket_np(dist):
    n = np.maximum(np.asarray(dist, np.int64), 0)
    max_exact = T5_BUCKETS // 2
    n_large = T5_BUCKETS - max_exact
    ratio = T5_MAX_DIST // max_exact
    thresholds = []
    for k in range(1, n_large):
        m = max_exact
        while m ** n_large < ratio ** k * max_exact ** n_large:
            m += 1
        thresholds.append(m)
    large = max_exact + sum((n >= th).astype(np.int64) for th in thresholds)
    return np.where(n < max_exact, n, large).astype(np.int32)


def _bias_table_kernel(sub_far, rb_ref, idx_ref, o_ref):
    hd = pl.program_id(0)
    idx = idx_ref[...]
    base = rb_ref[T5_BUCKETS - 1, hd] if sub_far else 0.0
    acc = jnp.full(idx.shape, NEG_MASK, F32)
    for b in range(T5_BUCKETS):
        acc = jnp.where(idx == b, (rb_ref[b, hd] - base) * LOG2E, acc)
    o_ref[...] = acc


def _bias_table(rel_bias, idx_np, sub_far, name):
    rows, cols = idx_np.shape
    tr = 128
    n_heads = rel_bias.shape[1]
    return pl.pallas_call(
        functools.partial(_bias_table_kernel, sub_far),
        grid=(n_heads, rows // tr),
        in_specs=[pl.BlockSpec(memory_space=pltpu.SMEM),
                  pl.BlockSpec((tr, cols), lambda hd, i: (i, 0))],
        out_specs=pl.BlockSpec((None, tr, cols), lambda hd, i: (hd, i, 0)),
        out_shape=jax.ShapeDtypeStruct((n_heads, rows, cols), F32),
        compiler_params=_cparams("parallel", "parallel"),
        name=name,
    )(rel_bias, jnp.asarray(idx_np))


def _compress_kernel(seq, x_ref, pe_ref, w1_ref, b1_ref, w2_ref, o_ref, xs):
    xs[0:seq, :] = x_ref[...]
    xs[seq:CMP_ROWS, :] = jnp.zeros((CMP_ROWS - seq, LANES), F32)
    acc = jnp.zeros((N_CMP_SLOTS, LANES), F32)
    for l in range(CMP_BLOCK):
        rows = xs[pl.ds(l, N_CMP_SLOTS, stride=CMP_STRIDE), :] + pe_ref[l:l + 1, :]
        acc = acc + _dot(rows.astype(BF16), w1_ref[l])
    hid = _gelu_tanh(acc + b1_ref[...])
    o_ref[...] = _dot(hid.astype(BF16), w2_ref[...]).astype(BF16)


def _compress(cmp_tok, pe, w1, b1, w2):
    bsz, seq, _ = cmp_tok.shape
    g = NSA_KV_GROUPS
    dh = NSA_HEAD_DIM
    return pl.pallas_call(
        functools.partial(_compress_kernel, seq),
        grid=(bsz, 2, g),
        in_specs=[pl.BlockSpec((None, seq, dh), lambda b, j, gi: (b, 0, j * g + gi)),
                  pl.BlockSpec((None, CMP_BLOCK, dh), lambda b, j, gi: (j, 0, 0)),
                  pl.BlockSpec((None, CMP_BLOCK, dh, dh), lambda b, j, gi: (j, 0, 0, 0)),
                  pl.BlockSpec((None, 1, dh), lambda b, j, gi: (j, 0, 0)),
                  pl.BlockSpec((None, dh, dh), lambda b, j, gi: (j, 0, 0))],
        out_specs=pl.BlockSpec((None, None, None, N_CMP_SLOTS, dh), lambda b, j, gi: (b, j, gi, 0, 0)),
        out_shape=jax.ShapeDtypeStruct((bsz, 2, g, N_CMP_SLOTS, dh), BF16),
        scratch_shapes=[pltpu.VMEM((CMP_ROWS, dh), F32)],
        compiler_params=_cparams("parallel", "parallel", "parallel"),
        name="nsa_compress",
    )(cmp_tok, pe, w1.astype(BF16), b1.reshape(2, 1, dh), w2.astype(BF16))


CMP_COLS = N_CMP_SLOTS // LANES
CMP_COL_TOKENS = CMP_STRIDE * LANES


def _cmp_table_idx(tq):
    ql = np.arange(tq)[:, None]
    m = np.arange(N_CMP_SLOTS)[None, :]
    e = np.where(m <= (tq - CMP_BLOCK) // CMP_STRIDE, -m, N_CMP_SLOTS - m)
    dist = CMP_STRIDE * e + ql - (CMP_BLOCK - 1)
    return _t5_bucket_np(dist)


def _sel_sum_matrix():
    i = np.arange(N_CMP_SLOTS)[:, None]
    j = np.arange(LANES)[None, :]
    ratio = SEL_BLOCK // CMP_STRIDE
    return ((i >= ratio * j - 1) & (i <= ratio * j + ratio - 1)).astype(np.float32)


CMP_SUB = 4


def _cmp_attn_kernel(tq, q_ref, kc_ref, vc_ref, tb_ref, sm_ref, g_ref, o_ref, val_ref):
    i = pl.program_id(2)
    lane = lax.broadcasted_iota(jnp.int32, (tq, LANES), 1)
    row = lax.broadcasted_iota(jnp.int32, (tq, LANES), 0)

    def body(nc):
        width = nc * LANES
        kc = kc_ref[0:width, :]
        vc = vc_ref[0:width, :]
        ql = lax.broadcasted_iota(jnp.int32, (tq, width), 0)
        blk = lax.broadcasted_iota(jnp.int32, (tq, width), 1)
        rel = ql - CMP_STRIDE * blk - (CMP_BLOCK - 1)
        for sub in range(CMP_SUB):
            rows = slice(sub * tq, (sub + 1) * tq)
            tile = i * CMP_SUB + sub
            t0 = tile * tq
            gate = g_ref[rows, :]
            causal = jnp.where(t0 + rel >= 0, 0.0, NEG_MASK)
            row_ok = (t0 + lax.broadcasted_iota(jnp.int32, (tq, 1), 0)) >= CMP_BLOCK - 1
            shift = (tile * (tq // CMP_STRIDE)) % LANES
            keep = lane >= shift
            p_sum = jnp.zeros((tq, width), F32)
            for hd in range(NSA_HPG):
                rot = {}
                for c in range(nc):
                    for k in ((c - (nc - 1)) % CMP_COLS, (c - nc) % CMP_COLS):
                        if k not in rot:
                            rot[k] = pltpu.roll(tb_ref[hd, :, k * LANES:(k + 1) * LANES], shift, axis=1)
                tb = jnp.concatenate(
                    [jnp.where(keep, rot[(c - (nc - 1)) % CMP_COLS], rot[(c - nc) % CMP_COLS])
                     for c in range(nc)], axis=1)
                s = _dot_nt(q_ref[rows, hd * LANES:(hd + 1) * LANES], kc) + tb + causal
                e = jnp.exp2(s - jnp.max(s, axis=-1, keepdims=True))
                p = e * jnp.where(row_ok, 1.0 / jnp.sum(e, axis=-1, keepdims=True), 0.0)
                p_sum = p_sum + p
                o_ref[rows, hd * LANES:(hd + 1) * LANES] = (
                    _dot(p.astype(BF16), vc) * gate[:, hd:hd + 1]).astype(o_ref.dtype)
            sm = sm_ref[0:width, :]
            hi = p_sum.astype(BF16)
            r1 = p_sum - hi.astype(F32)
            mid = r1.astype(BF16)
            lo = (r1 - mid.astype(F32)).astype(BF16)
            score = (_dot(hi, sm) + _dot(mid, sm)) + _dot(lo, sm)
            cur = (t0 + row) >> 6
            forced = jnp.logical_or(lane == 0, jnp.logical_or(lane == cur, lane == cur - 1))
            val_ref[rows, :] = jnp.where(forced, SEL_FORCE, jnp.where(lane <= cur, score, -SEL_FORCE))

    last_q = (i + 1) * (CMP_SUB * tq) - 1
    n_cols = jnp.minimum((last_q - (CMP_BLOCK - 1)) // CMP_COL_TOKENS + 1, CMP_COLS)
    for nc in range(1, CMP_COLS + 1):
        pl.when(n_cols == nc)(functools.partial(body, nc))


N_FORCED = 3


def _topk_mask_kernel(n_top, val_ref, m_ref):
    val0 = val_ref[...]
    lane = lax.broadcasted_iota(jnp.int32, val0.shape, 1).astype(F32)
    forced = val0 >= 0.5 * SEL_FORCE
    val = jnp.where(forced, -jnp.inf, val0)
    sel = jnp.where(forced, 1.0, 0.0)
    for _ in range(n_top - N_FORCED):
        mx = jnp.max(val, axis=-1, keepdims=True)
        first = jnp.min(jnp.where(val == mx, lane, float(LANES)), axis=-1, keepdims=True)
        pick = lane == first
        sel = jnp.where(pick, 1.0, sel)
        val = jnp.where(pick, -jnp.inf, val)
    m_ref[...] = jnp.where(jnp.logical_and(sel > 0.0, val0 >= 0.0), 0.0, NEG_MASK).astype(BF16)


def _topk_mask(val, n_top, tr=1024):
    rows = val.shape[0]
    return pl.pallas_call(
        functools.partial(_topk_mask_kernel, n_top),
        grid=(rows // tr,),
        in_specs=[pl.BlockSpec((tr, LANES), lambda i: (i, 0))],
        out_specs=pl.BlockSpec((tr, LANES), lambda i: (i, 0)),
        out_shape=jax.ShapeDtypeStruct((rows, LANES), BF16),
        compiler_params=_cparams("parallel"),
        name="nsa_topk_mask",
    )(val)


def _cmp_attn(q, kvc, table, gates, tq=256):
    bsz, seq, _ = q.shape
    g = NSA_KV_GROUPS
    qw = NSA_HPG * NSA_HEAD_DIM
    rows = tq * CMP_SUB
    assert CMP_COL_TOKENS % rows == 0 and tq % CMP_STRIDE == 0
    return pl.pallas_call(
        functools.partial(_cmp_attn_kernel, tq),
        grid=(bsz, g, seq // rows),
        in_specs=[pl.BlockSpec((None, rows, qw), lambda b, gi, i: (b, i, gi)),
                  pl.BlockSpec((None, None, None, N_CMP_SLOTS, NSA_HEAD_DIM), lambda b, gi, i: (b, 0, gi, 0, 0)),
                  pl.BlockSpec((None, None, None, N_CMP_SLOTS, NSA_HEAD_DIM), lambda b, gi, i: (b, 1, gi, 0, 0)),
                  pl.BlockSpec((NSA_HPG, tq, N_CMP_SLOTS), lambda b, gi, i: (gi, 0, 0)),
                  pl.BlockSpec((N_CMP_SLOTS, LANES), lambda b, gi, i: (0, 0)),
                  pl.BlockSpec((None, rows, LANES), lambda b, gi, i: (b, i, gi))],
        out_specs=[pl.BlockSpec((None, rows, qw), lambda b, gi, i: (b, i, gi)),
                   pl.BlockSpec((None, None, rows, LANES), lambda b, gi, i: (b, gi, i, 0))],
        out_shape=[jax.ShapeDtypeStruct((bsz, seq, g * qw), BF16),
                   jax.ShapeDtypeStruct((bsz, g, seq, LANES), F32)],
        compiler_params=_cparams("parallel", "parallel", "parallel"),
        name="nsa_cmp_attn",
    )(q, kvc, kvc, table, jnp.asarray(_sel_sum_matrix(), BF16), gates)


SEL_TQ = 256
SEL_TK = 256
SEL_N_DELTA = -(-(T5_MAX_DIST - 1 + SEL_TK) // SEL_TQ)
SEL_N_NEAR = -(-SEL_N_DELTA // (SEL_TK // SEL_TQ))
SEL_STREAMS = 4


def _sel_table_idx():
    d = np.arange(SEL_N_DELTA)[:, None, None]
    ql = np.arange(SEL_TQ)[None, :, None]
    kl = np.arange(SEL_TK)[None, None, :]
    dist = ql - kl + SEL_TQ * d
    idx = np.where(dist >= 0, _t5_bucket_np(dist), MASK_BUCKET)
    return idx.reshape(SEL_N_DELTA * SEL_TQ, SEL_TK).astype(np.int32)


def _sel_attn_kernel(q_ref, m_ref, k_ref, v_ref, oh_ref, nb_ref, g_ref, o_ref, qa, m_scr, acc):
    tq, tk = SEL_TQ, SEL_TK
    ratio = tk // tq
    i = pl.program_id(2)
    kd = i // ratio
    sel_mask = m_ref[...]
    for hd in range(NSA_HPG):
        qa[hd * tq:(hd + 1) * tq, 0:LANES] = q_ref[:, hd * LANES:(hd + 1) * LANES]
        qa[hd * tq:(hd + 1) * tq, LANES:2 * LANES] = sel_mask
    m_scr[...] = jnp.full(m_scr.shape, -jnp.inf, F32)
    acc[...] = jnp.zeros_like(acc)
    n_kt = k_ref.shape[0] // tk
    ones = jnp.ones((tk, LANES), BF16)

    def update(st, tiles):
        scores, values = [], []
        for kt, valid, delta in tiles:
            k0 = pl.multiple_of(jnp.clip(kt, 0, n_kt - 1) * tk, tk)
            ka = jnp.concatenate([k_ref[pl.ds(k0, tk), :], oh_ref[pl.ds(k0, tk), :]], axis=1)
            s = _dot_nt(qa[...], ka)
            if valid is not None:
                s = s + jnp.where(valid, 0.0, NEG_MASK)
            if delta is not None:
                s = s + jnp.concatenate([nb_ref[hd, delta] for hd in range(NSA_HPG)], axis=0)
            scores.append(s)
            values.append(jnp.concatenate([v_ref[pl.ds(k0, tk), :], ones], axis=1))
        s_max = scores[0]
        for s in scores[1:]:
            s_max = jnp.maximum(s_max, s)
        m_prev = m_scr[st]
        m_new = jnp.maximum(m_prev, jnp.max(s_max, axis=-1, keepdims=True))
        alpha = jnp.exp2(m_prev - m_new)
        m_rep = jnp.concatenate([m_new] * (tk // LANES), axis=1)
        pv = None
        for s, va in zip(scores, values):
            part = _dot(jnp.exp2(s - m_rep).astype(BF16), va)
            pv = part if pv is None else pv + part
        acc[st] = jnp.concatenate([alpha, alpha], axis=1) * acc[st] + pv
        m_scr[st] = m_new

    n_far = jnp.maximum(kd - (SEL_N_NEAR - 1), 0)
    pair = 2 * SEL_STREAMS

    def far_pairs(it, carry):
        for st in range(SEL_STREAMS):
            kt = it * pair + 2 * st
            update(st, [(kt, None, None), (kt + 1, None, None)])
        return carry

    lax.fori_loop(0, n_far // pair, far_pairs, 0)
    base = (n_far // pair) * pair
    rem = n_far - base

    @pl.when(rem >= SEL_STREAMS)
    def _():
        for st in range(SEL_STREAMS):
            update(st, [(base + st, None, None)])

    base = base + jnp.where(rem >= SEL_STREAMS, SEL_STREAMS, 0)
    left = n_far - base
    slots = [(base + f, f < left, None) for f in range(SEL_STREAMS - 1)]
    for back in range(SEL_N_NEAR - 1, -1, -1):
        kt = kd - back
        slots.append((kt, kt >= 0, (i - kd * ratio) + back * ratio))
    for st in range(SEL_STREAMS):
        update(st, slots[st::SEL_STREAMS])

    m_all = m_scr[0]
    for st in range(1, SEL_STREAMS):
        m_all = jnp.maximum(m_all, m_scr[st])
    tot = jnp.zeros((NSA_HPG * tq, 2 * LANES), F32)
    for st in range(SEL_STREAMS):
        w = jnp.exp2(m_scr[st] - m_all)
        tot = tot + jnp.concatenate([w, w], axis=1) * acc[st]
    gate = g_ref[...]
    out = tot[:, :LANES] / jnp.maximum(tot[:, LANES:], 1e-30)
    for hd in range(NSA_HPG):
        o_ref[:, hd * LANES:(hd + 1) * LANES] = (
            out[hd * tq:(hd + 1) * tq, :] * gate[:, NSA_HPG + hd:NSA_HPG + hd + 1]).astype(o_ref.dtype)


def _sel_attn(q, sel_mask, kv_tok, onehot, table, gates):
    bsz, seq, _ = q.shape
    g = NSA_KV_GROUPS
    dh = NSA_HEAD_DIM
    qw = NSA_HPG * dh
    tq, tk = SEL_TQ, SEL_TK
    rows = NSA_HPG * tq
    return pl.pallas_call(
        _sel_attn_kernel,
        grid=(bsz, g, seq // tq),
        in_specs=[pl.BlockSpec((None, tq, qw), lambda b, gi, i: (b, i, gi)),
                  pl.BlockSpec((None, None, tq, LANES), lambda b, gi, i: (b, gi, i, 0)),
                  pl.BlockSpec((None, seq, dh), lambda b, gi, i: (b, 0, gi)),
                  pl.BlockSpec((None, seq, dh), lambda b, gi, i: (b, 0, g + gi)),
                  pl.BlockSpec((seq, LANES), lambda b, gi, i: (0, 0)),
                  pl.BlockSpec((NSA_HPG, SEL_N_DELTA, tq, tk), lambda b, gi, i: (gi, 0, 0, 0)),
                  pl.BlockSpec((None, tq, LANES), lambda b, gi, i: (b, i, gi))],
        out_specs=pl.BlockSpec((None, tq, qw), lambda b, gi, i: (b, i, gi)),
        out_shape=jax.ShapeDtypeStruct((bsz, seq, g * qw), BF16),
        scratch_shapes=[pltpu.VMEM((rows, 2 * LANES), BF16), pltpu.VMEM((SEL_STREAMS, rows, LANES), F32),
                        pltpu.VMEM((SEL_STREAMS, rows, 2 * dh), F32)],
        compiler_params=_cparams("parallel", "parallel", "parallel"),
        name="nsa_sel_attn",
    )(q, sel_mask, kv_tok, kv_tok, onehot, table, gates)


WIN_TQ = 128
WIN_SUB = 8
WIN_KEYS = WINDOW + WIN_TQ


def _win_table_idx():
    ql = np.arange(WIN_TQ)[:, None]
    kl = np.arange(WIN_KEYS)[None, :]
    dist = ql + WINDOW - kl
    return np.where((dist >= 0) & (dist < WINDOW), _t5_bucket_np(dist), MASK_BUCKET).astype(np.int32)


def _win_attn_kernel(seq, q_ref, k_ref, v_ref, wb_ref, g_ref, o_ref, kpad, vpad, qs):
    tq = WIN_TQ
    i = pl.program_id(2)

    @pl.when(i == 0)
    def _():
        kpad[0:WINDOW, :] = jnp.zeros((WINDOW, LANES), BF16)
        vpad[0:WINDOW, :] = jnp.zeros((WINDOW, LANES), BF16)
        kpad[WINDOW:WINDOW + seq, :] = k_ref[...]
        vpad[WINDOW:WINDOW + seq, :] = v_ref[...]

    bias = wb_ref[...].reshape(NSA_HPG * tq, WIN_KEYS)

    def step(header_visible):
        for sub in range(WIN_SUB):
            rows = slice(sub * tq, (sub + 1) * tq)
            t0 = pl.multiple_of((i * WIN_SUB + sub) * tq, tq)
            for hd in range(NSA_HPG):
                qs[sub, hd * tq:(hd + 1) * tq, :] = q_ref[rows, hd * LANES:(hd + 1) * LANES]
            keys = kpad[pl.ds(t0, WIN_KEYS), :]
            vals = vpad[pl.ds(t0, WIN_KEYS), :]
            s = _dot_nt(qs[sub], keys) + bias
            if header_visible:
                kl = lax.broadcasted_iota(jnp.int32, s.shape, 1)
                s = jnp.where(kl >= WINDOW - t0, s, NEG_MASK)
            e = jnp.exp2(s - jnp.max(s, axis=-1, keepdims=True))
            out = _dot(e.astype(BF16), vals) * (1.0 / jnp.sum(e, axis=-1, keepdims=True))
            gate = g_ref[rows, :]
            for hd in range(NSA_HPG):
                o_ref[rows, hd * LANES:(hd + 1) * LANES] = (
                    out[hd * tq:(hd + 1) * tq, :]
                    * gate[:, 2 * NSA_HPG + hd:2 * NSA_HPG + hd + 1]).astype(o_ref.dtype)

    n_header_steps = -(-WINDOW // (WIN_SUB * tq))
    pl.when(i < n_header_steps)(functools.partial(step, True))
    pl.when(i >= n_header_steps)(functools.partial(step, False))


def _win_attn(q, kv_tok, table, gates):
    bsz, seq, _ = q.shape
    g = NSA_KV_GROUPS
    dh = NSA_HEAD_DIM
    qw = NSA_HPG * dh
    tq = WIN_TQ
    rows = tq * WIN_SUB
    return pl.pallas_call(
        functools.partial(_win_attn_kernel, seq),
        grid=(bsz, g, seq // rows),
        in_specs=[pl.BlockSpec((None, rows, qw), lambda b, gi, i: (b, i, gi)),
                  pl.BlockSpec((None, seq, dh), lambda b, gi, i: (b, 0, 2 * g + gi)),
                  pl.BlockSpec((None, seq, dh), lambda b, gi, i: (b, 0, 3 * g + gi)),
                  pl.BlockSpec((NSA_HPG, tq, WIN_KEYS), lambda b, gi, i: (gi, 0, 0)),
                  pl.BlockSpec((None, rows, LANES), lambda b, gi, i: (b, i, gi))],
        out_specs=pl.BlockSpec((None, rows, qw), lambda b, gi, i: (b, i, gi)),
        out_shape=jax.ShapeDtypeStruct((bsz, seq, g * qw), BF16),
        scratch_shapes=[pltpu.VMEM((seq + WINDOW, dh), BF16), pltpu.VMEM((seq + WINDOW, dh), BF16),
                        pltpu.VMEM((WIN_SUB, NSA_HPG * tq, dh), BF16)],
        compiler_params=_cparams("parallel", "parallel", "arbitrary"),
        name="nsa_win_attn",
    )(q, kv_tok, kv_tok, table, gates)


def _nsa_mixer(h, bsz, seq, rel_bias, w_in, cmp_pe, cmp_w1, cmp_b1, cmp_w2):
    g, hpg, dh = NSA_KV_GROUPS, NSA_HPG, NSA_HEAD_DIM
    qw = NSA_HEADS * dh
    kvw = g * dh
    n_main = qw + 2 * N_BRANCH * kvw
    scale = dh ** -0.5 * LOG2E
    q, cmp_tok, kv_tok = _proj(
        h, w_in[:, :n_main].astype(BF16),
        [(qw, lambda acc: acc * scale, BF16), (2 * kvw, lambda acc: acc, F32), (4 * kvw, lambda acc: acc, BF16)],
        tn=1024, name="nsa_proj")
    src = np.zeros((g * LANES,), np.int32)
    valid = np.zeros((g * LANES,), bool)
    for gi in range(g):
        for br in range(N_BRANCH):
            for hd in range(hpg):
                src[gi * LANES + br * hpg + hd] = n_main + br * NSA_HEADS + gi * hpg + hd
                valid[gi * LANES + br * hpg + hd] = True
    w_gate = jnp.where(jnp.asarray(valid)[None, :], w_in[:, src], 0.0).astype(BF16)
    (gates,) = _proj(h, w_gate, [(g * LANES, lambda acc: _sigmoid(acc), F32)], name="nsa_gate_proj")

    q = q.reshape(bsz, seq, qw)
    gates = gates.reshape(bsz, seq, g * LANES)
    kv_tok = kv_tok.reshape(bsz, seq, 4 * kvw)
    kvc = _compress(cmp_tok.reshape(bsz, seq, 2 * kvw), cmp_pe, cmp_w1, cmp_b1, cmp_w2)

    n_top = min(SEL_TOP_N, seq // SEL_BLOCK)
    cmp_tq = 256
    cmp_table = _bias_table(rel_bias, _cmp_table_idx(cmp_tq), False, "nsa_cmp_bias")
    o_c, sel_val = _cmp_attn(q, kvc, cmp_table, gates, cmp_tq)
    sel_mask = _topk_mask(sel_val.reshape(bsz * g * seq, LANES), n_top).reshape(bsz, g, seq, LANES)

    sel_table = _bias_table(rel_bias, _sel_table_idx(), True, "nsa_sel_bias")
    sel_table = sel_table.reshape(NSA_HEADS, SEL_N_DELTA, SEL_TQ, SEL_TK)
    onehot = (jnp.arange(seq, dtype=jnp.int32)[:, None] // SEL_BLOCK
              == jnp.arange(LANES, dtype=jnp.int32)[None, :]).astype(BF16)
    o_s = _sel_attn(q, sel_mask, kv_tok, onehot, sel_table, gates)

    win_table = _bias_table(rel_bias, _win_table_idx(), False, "nsa_win_bias")
    o_w = _win_attn(q, kv_tok, win_table, gates)
    t = bsz * seq
    return [o_c.reshape(t, qw), o_s.reshape(t, qw), o_w.reshape(t, qw)]


def _rotary_epi(scale, acc, cos, sin):
    half = cos.shape[-1]
    outs = []
    for hd in range(acc.shape[1] // (2 * half)):
        x1 = acc[:, 2 * hd * half:(2 * hd + 1) * half]
        x2 = acc[:, (2 * hd + 1) * half:(2 * hd + 2) * half]
        outs += [x1 * cos - x2 * sin, x1 * sin + x2 * cos]
    rot = jnp.concatenate(outs, axis=1)
    return rot * scale if scale != 1.0 else rot


RET_HPS = 2
RET_CHUNK = 256


def _ret_kernel(tc, q_ref, k_ref, v_ref, g_ref, dec_ref, xi_ref, zeta_ref, cd_ref, gw_ref, gb_ref,
                o_ref, state):
    i = pl.program_id(2)
    dk = q_ref.shape[-1] // RET_HPS
    dv = v_ref.shape[-1] // RET_HPS

    @pl.when(i == 0)
    def _():
        state[...] = jnp.zeros_like(state)

    for c in range(tc // RET_CHUNK):
        rows = slice(c * RET_CHUNK, (c + 1) * RET_CHUNK)
        for hh in range(RET_HPS):
            qk_cols = slice(hh * dk, (hh + 1) * dk)
            v_cols = slice(hh * dv, (hh + 1) * dv)
            qc = q_ref[rows, qk_cols]
            kc = k_ref[rows, qk_cols]
            vc = v_ref[rows, v_cols]
            inner = _dot_nt(qc, kc) * dec_ref[hh]
            st = state[hh]
            out = _dot(inner.astype(BF16), vc) + _dot(qc, st.astype(BF16)) * xi_ref[hh]
            kz = (kc.astype(F32) * zeta_ref[hh]).astype(BF16)
            state[hh] = st * cd_ref[hh] + _dot_tn(kz, vc)
            mu = jnp.mean(out, axis=-1, keepdims=True)
            oc = out - mu
            var = jnp.mean(oc * oc, axis=-1, keepdims=True)
            y = (oc * lax.rsqrt(var + GN_EPS)) * gw_ref[:, v_cols] + gb_ref[:, v_cols]
            o_ref[rows, v_cols] = (g_ref[rows, v_cols] * y).astype(BF16)


def _ret_core(q, k, v, g, gn_w, gn_b, tc=512):
    bsz, seq, qk_total = q.shape
    v_total = v.shape[-1]
    nh = RET_HEADS
    dk = qk_total // nh
    dv = v_total // nh
    cs = RET_CHUNK
    hps = RET_HPS
    log_g = jnp.log1p(-jnp.exp2(-5.0 - jnp.arange(nh, dtype=F32)))
    idx = jnp.arange(cs, dtype=F32)
    diff = idx[:, None] - idx[None, :]
    decay = jnp.where(diff >= 0, jnp.exp(jnp.maximum(diff, 0.0) * log_g[:, None, None]), 0.0)
    xi = jnp.exp((idx + 1.0) * log_g[:, None])[:, :, None]
    zeta = jnp.exp((cs - 1.0 - idx) * log_g[:, None])[:, :, None]
    chunk_decay = jnp.exp(cs * log_g).reshape(nh, 1, 1)
    return pl.pallas_call(
        functools.partial(_ret_kernel, tc),
        grid=(bsz, nh // hps, seq // tc),
        in_specs=[pl.BlockSpec((None, tc, hps * dk), lambda b, hp, i: (b, i, hp)),
                  pl.BlockSpec((None, tc, hps * dk), lambda b, hp, i: (b, i, hp)),
                  pl.BlockSpec((None, tc, hps * dv), lambda b, hp, i: (b, i, hp)),
                  pl.BlockSpec((None, tc, hps * dv), lambda b, hp, i: (b, i, hp)),
                  pl.BlockSpec((hps, cs, cs), lambda b, hp, i: (hp, 0, 0)),
                  pl.BlockSpec((hps, cs, 1), lambda b, hp, i: (hp, 0, 0)),
                  pl.BlockSpec((hps, cs, 1), lambda b, hp, i: (hp, 0, 0)),
                  pl.BlockSpec((hps, 1, 1), lambda b, hp, i: (hp, 0, 0)),
                  pl.BlockSpec((1, hps * dv), lambda b, hp, i: (0, hp)),
                  pl.BlockSpec((1, hps * dv), lambda b, hp, i: (0, hp))],
        out_specs=pl.BlockSpec((None, tc, hps * dv), lambda b, hp, i: (b, i, hp)),
        out_shape=jax.ShapeDtypeStruct((bsz, seq, v_total), BF16),
        scratch_shapes=[pltpu.VMEM((hps, dk, dv), F32)],
        compiler_params=_cparams("parallel", "parallel", "arbitrary"),
        name="ret_core",
    )(q, k, v, g, decay, xi, zeta, chunk_decay, gn_w.reshape(1, v_total), gn_b.reshape(1, v_total))


def _ret_mixer(h, bsz, seq, w_in, gn_w, gn_b, tm=1024):
    qk_total = w_in.shape[0]
    v_total = (w_in.shape[1] - 2 * qk_total) // 2
    dk = qk_total // RET_HEADS
    half = dk // 2
    pos = jnp.arange(seq, dtype=F32)
    inv = ROPE_BASE ** (-jnp.arange(half, dtype=F32) / half)
    ang = pos[:, None] * inv[None, :]
    cos, sin = jnp.cos(ang), jnp.sin(ang)
    tpb = seq // tm
    rope_spec = pl.BlockSpec((tm, half), lambda i, j: (i % tpb, 0))
    q, k, v, g = _proj(
        h, w_in.astype(BF16),
        [(qk_total, functools.partial(_rotary_epi, 1.0), BF16),
         (qk_total, functools.partial(_rotary_epi, dk ** -0.5), BF16),
         (v_total, lambda acc, cos, sin: acc, BF16),
         (v_total, lambda acc, cos, sin: _silu(acc), BF16)],
        extra=(cos, sin), extra_specs=(rope_spec, rope_spec), tm=tm, tn=1024, name="ret_proj")
    a = _ret_core(q.reshape(bsz, seq, qk_total), k.reshape(bsz, seq, qk_total),
                  v.reshape(bsz, seq, v_total), g.reshape(bsz, seq, v_total), gn_w, gn_b)
    return [a.reshape(bsz * seq, v_total)]


def kernel(x, c, rel_bias, ada_w, ada_b, ln_w, ln_b, mlp_w1, mlp_w2, lru_w_in, lru_conv_w, lru_conv_b,
           lru_gate_w, lru_gate_b, lru_lambda, lru_w_out, nsa_w_in, nsa_cmp_pe, nsa_cmp_w1, nsa_cmp_b1,
           nsa_cmp_w2, nsa_w_out, ret_w_in, ret_gn_w, ret_gn_b, ret_w_out):
    bsz, seq, d = x.shape
    depth = ada_w.shape[0]
    mods = _ada_mods(c, ada_w, ada_b)
    xf = x.reshape(bsz * seq, d)
    h = _modulate(xf, mods, 0, seq)
    for layer in range(depth):
        mixer, inst = layer % N_MIXERS, layer // N_MIXERS
        sub = 2 * layer
        if mixer == 0:
            a_list = _lru_mixer(h, bsz, seq, lru_w_in[inst], lru_conv_w[inst], lru_conv_b[inst],
                                lru_gate_w[inst], lru_gate_b[inst], lru_lambda[inst])
            w_out = lru_w_out[inst]
        elif mixer == 1:
            a_list = _nsa_mixer(h, bsz, seq, rel_bias, nsa_w_in[inst], nsa_cmp_pe[inst], nsa_cmp_w1[inst],
                                nsa_cmp_b1[inst], nsa_cmp_w2[inst])
            w_out = nsa_w_out[inst]
        else:
            a_list = _ret_mixer(h, bsz, seq, ret_w_in[inst], ret_gn_w[inst], ret_gn_b[inst])
            w_out = ret_w_out[inst]
        xf, h = _out_ln(a_list, w_out.astype(BF16), xf, mods, sub, sub + 1, ln_w[layer, 0], ln_b[layer, 0], seq)
        sub_next = sub + 2 if layer + 1 < depth else None
        xf, h = _mlp(h, mlp_w1[layer].astype(BF16), mlp_w2[layer].astype(BF16), xf, mods, sub + 1, sub_next,
                     ln_w[layer, 1], ln_b[layer, 1], seq)
    return xf.reshape(bsz, seq, d)
```
